```python
import math
import jax, jax.numpy as jnp
from jax import lax
import numpy as np

D_MODEL = 1024
BATCH = 32
SEQ = 256
DEPTH = 2
DEC_BATCH = 4
DEC_SEQ = 4096
PAST_LEN = 512

GRID_W = 64
CHUNK = 64
SUB = 16
EPS = 1e-6
D_FF = 2816
N_DIR = 2
N_BRANCH = 3
M_HEADS = 4
M_DK = 64
M_DV = 128
G_HEADS = 4
G_DK = 64
G_DV = 128
G_RANK = 16
G_TEMP = 16.0
H_HEADS = 4
H_DE = 64
H_DV = 128
MIX_W = 512

IN_SIZES = (
    M_HEADS * M_DK, M_HEADS * M_DK, M_HEADS * M_DV, M_HEADS * M_DV, N_DIR * 2 * M_HEADS,
    G_HEADS * G_DK, G_HEADS * G_DK, G_HEADS * G_DV, G_HEADS * G_DV, N_DIR * G_RANK,
    H_HEADS * H_DE, N_DIR * H_HEADS * H_DE, H_HEADS * H_DV, H_HEADS * H_DV,
    N_BRANCH * D_MODEL,
)
IN_COLS = sum(IN_SIZES)

kernel_name = "bidir_mlstm_gla_hgrn2_prefix_diffusion_step"


def rmsnorm(x, w):
    x32 = x.astype(jnp.float32)
    y = x32 * lax.rsqrt(jnp.mean(x32 * x32, axis=-1, keepdims=True) + EPS)
    return y.astype(x.dtype) * w


def head_rmsnorm(x, w, n_heads):
    b, t, wd = x.shape
    xh = x.reshape(b, t, n_heads, wd // n_heads).astype(jnp.float32)
    xh = xh * lax.rsqrt(jnp.mean(xh * xh, axis=-1, keepdims=True) + EPS)
    return xh.reshape(b, t, wd) * w.astype(jnp.float32)


def to_heads(x, n_heads):
    b, t, _ = x.shape
    return x.reshape(b, t, n_heads, -1).transpose(0, 2, 1, 3)


def from_heads(x):
    b, h, t, d = x.shape
    return x.transpose(0, 2, 1, 3).reshape(b, t, h * d)


def dir_stack(fwd, bwd):
    return jnp.concatenate([fwd, jnp.flip(bwd, axis=2)], axis=1)


def dir_merge(o, n_heads):
    return o[:, :n_heads] + jnp.flip(o[:, n_heads:], axis=2)


def to_chunks(x):
    b, h, t = x.shape[:3]
    x = x.reshape(b, h, t // CHUNK, CHUNK, *x.shape[3:])
    return jnp.moveaxis(x, 2, 0)


def from_chunks(y):
    y = jnp.moveaxis(y, 0, 2)
    b, h, n, l = y.shape[:4]
    return y.reshape(b, h, n * l, *y.shape[4:])


def gated_linear_scan(q, k, v, log_a, s0):
    nb = CHUNK // SUB
    tri = jnp.tril(jnp.ones((SUB, SUB), bool))
    blk_lower = jnp.tril(jnp.ones((nb, nb), bool), -1)

    def step(s, inp):
        qc, kc, vc, la = inp
        b, h, l, dk = qc.shape
        dv = vc.shape[-1]
        g = jnp.cumsum(la, axis=2)
        o_inter = jnp.einsum('bhld,bhde->bhle', qc * jnp.exp(g), s)
        qs = qc.reshape(b, h, nb, SUB, dk)
        ks = kc.reshape(b, h, nb, SUB, dk)
        gs = g.reshape(b, h, nb, SUB, dk)
        vs = vc.reshape(b, h, nb, SUB, dv)
        g_end = gs[:, :, :, -1]
        q_off = qs[:, :, :, None] * jnp.exp(jnp.minimum(gs[:, :, :, None] - g_end[:, :, None, :, None], 0.0))
        k_off = ks * jnp.exp(g_end[:, :, :, None] - gs)
        a_off = jnp.einsum('bhijtd,bhjsd->bhijts', q_off, k_off) * blk_lower[:, :, None, None]
        o_off = jnp.einsum('bhijts,bhjse->bhite', a_off, vs)
        decay = jnp.exp(jnp.minimum(gs[:, :, :, :, None] - gs[:, :, :, None, :], 0.0))
        a_diag = jnp.einsum('bhntd,bhntsd,bhnsd->bhnts', qs, decay, ks) * tri
        o_diag = jnp.einsum('bhnts,bhnse->bhnte', a_diag, vs)
        o = o_inter + (o_off + o_diag).reshape(b, h, l, dv)
        g_last = g[:, :, -1]
        s_new = jnp.exp(g_last)[..., None] * s + jnp.einsum(
            'bhld,bhle->bhde', kc * jnp.exp(g_last[:, :, None] - g), vc)
        return s_new, o

    s_fin, o = lax.scan(step, s0, (to_chunks(q), to_chunks(k), to_chunks(v), to_chunks(log_a)))
    return from_chunks(o), s_fin


def mlstm_scan(q, k, v, log_i, log_f, c0, n0, m0):
    tri = jnp.tril(jnp.ones((CHUNK, CHUNK), bool))

    def step(carry, inp):
        c, n, m = carry
        qc, kc, vc, lic, lfc = inp
        b = jnp.cumsum(lfc, axis=-1)
        d = jnp.where(tri, b[..., :, None] - b[..., None, :] + lic[..., None, :], -jnp.inf)
        inter = b + m[..., None]
        m_t = jnp.maximum(inter, jnp.max(d, axis=-1))
        w = jnp.exp(d - m_t[..., None])
        sc = jnp.einsum('bhtd,bhsd->bhts', qc, kc) * w
        e_inter = jnp.exp(inter - m_t)
        num = jnp.einsum('bhts,bhse->bhte', sc, vc) + e_inter[..., None] * jnp.einsum('bhtd,bhde->bhte', qc, c)
        den = jnp.sum(sc, axis=-1) + e_inter * jnp.einsum('bhtd,bhd->bht', qc, n)
        h = num / jnp.maximum(jnp.abs(den), jnp.exp(-m_t))[..., None]
        b_last = b[..., -1]
        lw = b_last[..., None] - b + lic
        m_new = jnp.maximum(b_last + m, jnp.max(lw, axis=-1))
        ws = jnp.exp(lw - m_new[..., None])
        dec = jnp.exp(b_last + m - m_new)
        c_new = dec[..., None, None] * c + jnp.einsum('bhs,bhsd,bhse->bhde', ws, kc, vc)
        n_new = dec[..., None] * n + jnp.einsum('bhs,bhsd->bhd', ws, kc)
        return (c_new, n_new, m_new), h

    (c1, n1, m1), h = lax.scan(step, (c0, n0, m0), (to_chunks(q), to_chunks(k), to_chunks(v),
                                                  to_chunks(log_i), to_chunks(log_f)))
    return from_chunks(h), (c1, n1, m1)


def swiglu(h, w_in, w_out):
    gate, up = jnp.split(jnp.einsum('btd,df->btf', h, w_in), 2, axis=-1)
    return jnp.einsum('btf,fd->btd', jax.nn.silu(gate) * up, w_out)


def grid_position(n_tokens):
    rows = n_tokens // GRID_W
    quarter = D_MODEL // 4
    freqs = jnp.exp(-math.log(10000.0) * jnp.arange(quarter, dtype=jnp.float32) / quarter)
    r = jnp.arange(rows, dtype=jnp.float32)[:, None] * freqs
    cl = jnp.arange(GRID_W, dtype=jnp.float32)[:, None] * freqs
    r_emb = jnp.concatenate([jnp.sin(r), jnp.cos(r)], axis=-1)
    c_emb = jnp.concatenate([jnp.sin(cl), jnp.cos(cl)], axis=-1)
    emb = jnp.concatenate([jnp.broadcast_to(r_emb[:, None], (rows, GRID_W, D_MODEL // 2)),
                           jnp.broadcast_to(c_emb[None], (rows, GRID_W, D_MODEL // 2))], axis=-1)
    return emb.reshape(rows * GRID_W, D_MODEL)


def mixer(h, st, w_in, gate_bias_m, gla_w_up, gla_b, lb, head_norm, w_branch, w_out):
    f32 = jnp.float32
    b, t, _ = h.shape
    proj = jnp.einsum('btd,dc->btc', h, w_in).astype(f32)
    split_at = [int(s) for s in np.cumsum(IN_SIZES)[:-1]]
    (mq, mk, mv, mo, mif, gq, gk, gv, gr, glr, hq, hf, hv, hg, mg) = jnp.split(proj, split_at, axis=-1)
    c0, n0, m0, sg0, sh0 = [s.astype(f32) for s in st]

    q = to_heads(mq, M_HEADS) * (M_DK ** -0.5)
    k = to_heads(mk, M_HEADS)
    v = to_heads(mv, M_HEADS)
    gates = mif.reshape(b, t, N_DIR, 2, M_HEADS) + gate_bias_m.astype(f32)
    log_i = gates[:, :, :, 0].transpose(0, 2, 3, 1)
    log_f = jax.nn.log_sigmoid(gates[:, :, :, 1]).transpose(0, 2, 3, 1)
    h_m, (c1, n1, m1) = mlstm_scan(dir_stack(q, q), dir_stack(k, k), dir_stack(v, v),
                                   dir_stack(log_i[:, 0], log_i[:, 1]), dir_stack(log_f[:, 0], log_f[:, 1]),
                                   c0, n0, m0)
    y_m = jax.nn.sigmoid(mo) * head_rmsnorm(from_heads(dir_merge(h_m, M_HEADS)), head_norm[0], M_HEADS)

    q = to_heads(gq, G_HEADS) * (G_DK ** -0.5)
    k = to_heads(gk, G_HEADS)
    v = to_heads(gv, G_HEADS)
    la = jax.nn.log_sigmoid(jnp.einsum('btzr,zrc->btzc', glr.reshape(b, t, N_DIR, G_RANK), gla_w_up.astype(f32))
                            + gla_b.astype(f32)) / G_TEMP
    o_g, sg1 = gated_linear_scan(dir_stack(q, q), dir_stack(k, k), dir_stack(v, v),
                                 dir_stack(to_heads(la[:, :, 0], G_HEADS), to_heads(la[:, :, 1], G_HEADS)), sg0)
    y_g = jax.nn.silu(gr) * head_rmsnorm(from_heads(dir_merge(o_g, G_HEADS)), head_norm[1], G_HEADS)

    z = hf.reshape(b, t, N_DIR, H_HEADS * H_DE)
    lb = lb.astype(f32)
    log_fh = jnp.log(lb + (1.0 - lb) * jax.nn.sigmoid(z))
    key_h = (1.0 - lb) * jax.nn.sigmoid(-z)
    q = to_heads(hq, H_HEADS)
    i_v = to_heads(jax.nn.silu(hv), H_HEADS)
    o_h, sh1 = gated_linear_scan(dir_stack(q, q),
                                 dir_stack(to_heads(key_h[:, :, 0], H_HEADS), to_heads(key_h[:, :, 1], H_HEADS)),
                                 dir_stack(i_v, i_v),
                                 dir_stack(to_heads(log_fh[:, :, 0], H_HEADS), to_heads(log_fh[:, :, 1], H_HEADS)),
                                 sh0)
    y_h = jax.nn.silu(hg) * head_rmsnorm(from_heads(dir_merge(o_h, H_HEADS)), head_norm[2], H_HEADS)

    ys = jnp.stack([y_m, y_g, y_h], axis=2).astype(h.dtype)
    branch = jnp.einsum('btnc,ncd->btnd', ys, w_branch)
    merge_gate = jax.nn.sigmoid(mg).reshape(b, t, N_BRANCH, D_MODEL).astype(h.dtype)
    out = jnp.einsum('btd,de->bte', jnp.sum(merge_gate * branch, axis=2), w_out)
    return out.astype(h.dtype), (c1, n1, m1, sg1, sh1)


def trunk_layer(x, mod, pos, st, norm_pre, norm_post, w_ffn_in, w_ffn_out, w_in, gate_bias_m,
                gla_w_up, gla_b, lb, head_norm, w_branch, w_out):
    md = [mod[:, :, i] for i in range(9)]

    def modulate(x_, i, j):
        return rmsnorm(x_, norm_pre[j]) * (1.0 + md[i + 1]) + md[i]

    h = modulate(x, 0, 0)
    x = x + 0.5 * md[2] * rmsnorm(swiglu(h, w_ffn_in[0], w_ffn_out[0]), norm_post[0])
    h = modulate(x, 3, 1)
    if pos is not None:
        h = h + pos
    y, st = mixer(h, st, w_in, gate_bias_m, gla_w_up, gla_b, lb, head_norm, w_branch, w_out)
    x = x + md[5] * rmsnorm(y, norm_post[1])
    h = modulate(x, 6, 2)
    x = x + 0.5 * md[8] * rmsnorm(swiglu(h, w_ffn_in[1], w_ffn_out[1]), norm_post[2])
    return x, st


def setup_inputs(seed: int = 0) -> dict:
    key = jax.random.key(seed)
    ks = jax.random.split(key, 23)

    def nrm(k, shape, s):
        return jax.random.normal(k, shape, jnp.float32) * s

    gb = nrm(ks[16], (DEPTH, N_DIR, 2, M_HEADS), 1.0)
    mlstm_gate_bias = gb * jnp.array([0.1, 0.5], jnp.float32)[:, None] + jnp.array([0.0, 3.0], jnp.float32)[:, None]
    return {
        "x_prompt": nrm(ks[0], (BATCH, SEQ, D_MODEL), 1.0),
        "x_sample": nrm(ks[1], (DEC_BATCH, DEC_SEQ, D_MODEL), 1.0),
        "c": nrm(ks[2], (DEC_BATCH, D_MODEL), 1.0),
        "state_mlstm_C": nrm(ks[3], (DEC_BATCH, DEPTH, N_DIR, M_HEADS, M_DK, M_DV), 0.1),
        "state_mlstm_n": nrm(ks[4], (DEC_BATCH, DEPTH, N_DIR, M_HEADS, M_DK), 0.1),
        "state_mlstm_m": nrm(ks[5], (DEC_BATCH, DEPTH, N_DIR, M_HEADS), 0.5),
        "state_gla_S": nrm(ks[6], (DEC_BATCH, DEPTH, N_DIR, G_HEADS, G_DK, G_DV), 0.3),
        "state_hgrn_S": nrm(ks[7], (DEC_BATCH, DEPTH, N_DIR, H_HEADS, H_DE, H_DV), 0.3),
        "c_ctx": nrm(ks[8], (D_MODEL,), 1.0),
        "w_ada": nrm(ks[9], (DEPTH, D_MODEL, 9 * D_MODEL), 0.5 * D_MODEL ** -0.5),
        "b_ada": nrm(ks[10], (DEPTH, 9 * D_MODEL), 0.02),
        "norm_pre": 1.0 + nrm(ks[11], (DEPTH, 3, D_MODEL), 0.1),
        "norm_post": 1.0 + nrm(ks[12], (DEPTH, 3, D_MODEL), 0.1),
        "w_ffn_in": nrm(ks[13], (DEPTH, 2, D_MODEL, 2 * D_FF), D_MODEL ** -0.5),
        "w_ffn_out": nrm(ks[14], (DEPTH, 2, D_FF, D_MODEL), D_FF ** -0.5),
        "w_in": nrm(ks[15], (DEPTH, D_MODEL, IN_COLS), D_MODEL ** -0.5),
        "mlstm_gate_bias": mlstm_gate_bias,
        "gla_w_up": nrm(ks[17], (DEPTH, N_DIR, G_RANK, G_HEADS * G_DK), G_RANK ** -0.5),
        "gla_b": nrm(ks[18], (DEPTH, N_DIR, G_HEADS * G_DK), 0.1),
        "hgrn_gamma": nrm(ks[19], (DEPTH, H_HEADS * H_DE), 1.0),
        "head_norm": 1.0 + nrm(ks[20], (DEPTH, N_BRANCH, MIX_W), 0.1),
        "w_branch": nrm(ks[21], (DEPTH, N_BRANCH, MIX_W, D_MODEL), MIX_W ** -0.5),
        "w_out": nrm(ks[22], (DEPTH, D_MODEL, D_MODEL), D_MODEL ** -0.5),
    }


def reference(x_prompt, x_sample, c, state_mlstm_C, state_mlstm_n, state_mlstm_m, state_gla_S, state_hgrn_S,
              c_ctx, w_ada, b_ada, norm_pre, norm_post, w_ffn_in, w_ffn_out, w_in, mlstm_gate_bias,
              gla_w_up, gla_b, hgrn_gamma, head_norm, w_branch, w_out):
    f32 = jnp.float32
    bp = x_prompt.shape[0]
    bs, ts = x_sample.shape[:2]
    p_gamma = jax.nn.softmax(hgrn_gamma.astype(f32), axis=0)
    lb_all = jnp.cumsum(p_gamma, axis=0) - p_gamma[0]
    pos = grid_position(ts).astype(x_sample.dtype)
    silu_ctx = jax.nn.silu(c_ctx)[None]
    silu_c = jax.nn.silu(c)

    xp, xs = x_prompt, x_sample
    new_c, new_n, new_m, new_sg, new_sh = [], [], [], [], []
    for l in range(DEPTH):
        lp = (norm_pre[l], norm_post[l], w_ffn_in[l], w_ffn_out[l], w_in[l], mlstm_gate_bias[l],
              gla_w_up[l], gla_b[l], lb_all[l], head_norm[l], w_branch[l], w_out[l])
        mod_ctx = (silu_ctx @ w_ada[l] + b_ada[l]).reshape(1, 1, 9, D_MODEL)
        st0 = (jnp.zeros((bp, N_DIR * M_HEADS, M_DK, M_DV), f32),
               jnp.zeros((bp, N_DIR * M_HEADS, M_DK), f32),
               jnp.zeros((bp, N_DIR * M_HEADS), f32),
               jnp.zeros((bp, N_DIR * G_HEADS, G_DK, G_DV), f32),
               jnp.zeros((bp, N_DIR * H_HEADS, H_DE, H_DV), f32))
        xp, (c1, n1, m1, sg1, sh1) = trunk_layer(xp, mod_ctx, None, st0, *lp)
        new_c.append(c1.reshape(bp, N_DIR, M_HEADS, M_DK, M_DV))
        new_n.append(n1.reshape(bp, N_DIR, M_HEADS, M_DK))
        new_m.append(m1.reshape(bp, N_DIR, M_HEADS))
        new_sg.append(sg1.reshape(bp, N_DIR, G_HEADS, G_DK, G_DV))
        new_sh.append(sh1.reshape(bp, N_DIR, H_HEADS, H_DE, H_DV))
        mod_lat = (silu_c @ w_ada[l] + b_ada[l]).reshape(bs, 1, 9, D_MODEL)
        st_lat = (state_mlstm_C[:, l].reshape(bs, N_DIR * M_HEADS, M_DK, M_DV),
                  state_mlstm_n[:, l].reshape(bs, N_DIR * M_HEADS, M_DK),
                  state_mlstm_m[:, l].reshape(bs, N_DIR * M_HEADS),
                  state_gla_S[:, l].reshape(bs, N_DIR * G_HEADS, G_DK, G_DV),
                  state_hgrn_S[:, l].reshape(bs, N_DIR * H_HEADS, H_DE, H_DV))
        xs, _ = trunk_layer(xs, mod_lat, pos, st_lat, *lp)

    new_mlstm_C = jnp.stack(new_c, axis=1)
    new_mlstm_n = jnp.stack(new_n, axis=1)
    new_mlstm_m = jnp.stack(new_m, axis=1)
    new_gla_S = jnp.stack(new_sg, axis=1)
    new_hgrn_S = jnp.stack(new_sh, axis=1)
    return (xp, xs, new_mlstm_C, new_mlstm_n, new_mlstm_m, new_gla_S, new_hgrn_S)
```

```python
import functools
import math

import numpy as np
import jax
import jax.numpy as jnp
from jax import lax
from jax.experimental import pallas as pl
from jax.experimental.pallas import tpu as pltpu

F32 = jnp.float32
BF16 = jnp.bfloat16
HIGHEST = lax.Precision.HIGHEST

D_MODEL = 1024
D_FF = 2816
GRID_W = 64
CHUNK = 64
EPS = 1e-6
N_DIR = 2
HEADS = 4
DK = 64
DV = 128
QK_W = HEADS * DK
MIX_W = HEADS * DV
G_RANK = 16
G_TEMP = 16.0
N_MIF = N_DIR * 2 * HEADS
N_GLR = N_DIR * G_RANK
LANE = 128
N_LEVELS = 7

OFF_MQ, OFF_MK, OFF_MV = 0, 256, 512
OFF_GQ, OFF_GK, OFF_GV = 1024, 1280, 1536
OFF_HQ, OFF_HF, OFF_HV = 2048, 2304, 2816
OFF_SM = 3328
P_W = OFF_SM + LANE
GATE_W = 3 * MIX_W + 3 * D_MODEL

IN_SIZES = (256, 256, 512, 512, N_MIF, 256, 256, 512, 512, N_GLR, 256, 512, 512, 512, 3 * D_MODEL)

FFN_CHUNK = 256
PROJ_CHUNK = 1152
VMEM_LIMIT = 56 * 1024 * 1024


def _const_spec(shape, grid_rank, single=True):
    zeros = (0,) * len(shape)
    if grid_rank == 1:
        imap = lambda i: zeros
    else:
        imap = lambda i, j: zeros
    if single:
        return pl.BlockSpec(shape, imap, pipeline_mode=pl.Buffered(1))
    return pl.BlockSpec(shape, imap)


def _rms(x, w):
    ms = jnp.mean(x * x, axis=-1, keepdims=True)
    return x * lax.rsqrt(ms + EPS) * w


def _modulated(x, mod_ref, npre_ref, mi):
    shift = mod_ref[0, mi:mi + 1, :]
    scale = mod_ref[0, mi + 1:mi + 2, :]
    return _rms(x, npre_ref[...]) * (1.0 + scale) + shift


def _row_tile(t_total, seq_len, per_batch_mod):
    for tm in (512, 256, 128, 64):
        if t_total % tm == 0 and (not per_batch_mod or seq_len % tm == 0):
            return tm
    raise ValueError("token count must be a multiple of 64")


def _ada_kernel(c_ref, w_ref, b_ref, o_ref):
    cv = c_ref[...]
    s = cv * jax.nn.sigmoid(cv)
    o_ref[0] = jnp.dot(s, w_ref[0], precision=HIGHEST, preferred_element_type=F32) + b_ref[0]


def _ada(cvec, w_ada, b_ada):
    depth, _, n = w_ada.shape
    tn = 1536
    rows = cvec.shape[0]
    return pl.pallas_call(
        _ada_kernel,
        grid=(depth, n // tn),
        in_specs=[
            pl.BlockSpec((rows, D_MODEL), lambda l, j: (0, 0)),
            pl.BlockSpec((1, D_MODEL, tn), lambda l, j: (l, 0, j)),
            pl.BlockSpec((1, 1, tn), lambda l, j: (l, 0, j)),
        ],
        out_specs=pl.BlockSpec((1, rows, tn), lambda l, j: (l, 0, j)),
        out_shape=jax.ShapeDtypeStruct((depth, rows, n), F32),
        compiler_params=pltpu.CompilerParams(
            dimension_semantics=("arbitrary", "arbitrary"), vmem_limit_bytes=VMEM_LIMIT),
        name="ada",
    )(cvec, w_ada, b_ada.reshape(depth, 1, n))


def _ffn_kernel(x_ref, mod_ref, npre_ref, npost_ref, w1_ref, w2_ref, o_ref, acc_ref, *, mi):
    x = x_ref[...]
    h = _modulated(x, mod_ref, npre_ref, mi).astype(BF16)
    for k in range(D_FF // FFN_CHUNK):
        lo = k * FFN_CHUNK
        g = jnp.dot(h, w1_ref[:, lo:lo + FFN_CHUNK], preferred_element_type=F32)
        u = jnp.dot(h, w1_ref[:, D_FF + lo:D_FF + lo + FFN_CHUNK], preferred_element_type=F32)
        a = (g * jax.nn.sigmoid(g) * u).astype(BF16)
        part = jnp.dot(a, w2_ref[lo:lo + FFN_CHUNK, :], preferred_element_type=F32)
        if k == 0:
            acc_ref[...] = part
        else:
            acc_ref[...] += part
    gate = mod_ref[0, mi + 2:mi + 3, :]
    o_ref[...] = x + (0.5 * gate) * _rms(acc_ref[...], npost_ref[...])


def _ffn(x, mod, npre, npost, w1, w2, *, mi, seq_len):
    t_total = x.shape[0]
    per_batch = mod.shape[0] > 1
    tm = _row_tile(t_total, seq_len, per_batch)
    mod_map = (lambda i: ((i * tm) // seq_len, 0, 0)) if per_batch else (lambda i: (0, 0, 0))
    return pl.pallas_call(
        functools.partial(_ffn_kernel, mi=mi),
        grid=(t_total // tm,),
        in_specs=[
            pl.BlockSpec((tm, D_MODEL), lambda i: (i, 0)),
            pl.BlockSpec((1, 9, D_MODEL), mod_map),
            _const_spec((1, D_MODEL), 1),
            _const_spec((1, D_MODEL), 1),
            _const_spec((D_MODEL, 2 * D_FF), 1),
            _const_spec((D_FF, D_MODEL), 1),
        ],
        out_specs=pl.BlockSpec((tm, D_MODEL), lambda i: (i, 0)),
        out_shape=jax.ShapeDtypeStruct((t_total, D_MODEL), F32),
        scratch_shapes=[pltpu.VMEM((tm, D_MODEL), F32)],
        compiler_params=pltpu.CompilerParams(
            dimension_semantics=("arbitrary",), vmem_limit_bytes=VMEM_LIMIT),
        name="ffn",
    )(x, mod, npre, npost, w1, w2)


def _proj_kernel(*refs, has_pos):
    if has_pos:
        x_ref, mod_ref, npre_ref, pos_ref, w_ref, wt_ref, o_ref, ot_ref = refs
    else:
        x_ref, mod_ref, npre_ref, w_ref, wt_ref, o_ref, ot_ref = refs
    h = _modulated(x_ref[...], mod_ref, npre_ref, 3)
    if has_pos:
        h = h + pos_ref[...]
    hb = h.astype(BF16)
    for j in range(P_W // PROJ_CHUNK):
        lo = j * PROJ_CHUNK
        o_ref[:, lo:lo + PROJ_CHUNK] = jnp.dot(hb, w_ref[:, lo:lo + PROJ_CHUNK], preferred_element_type=F32)
    ot_ref[...] = lax.dot_general(wt_ref[...], hb, (((1,), (1,)), ((), ())), preferred_element_type=F32)


def _proj(x, mod, npre, pos, w_scan, w_mif_t, *, seq_len):
    t_total = x.shape[0]
    per_batch = mod.shape[0] > 1
    tm = _row_tile(t_total, seq_len, per_batch or pos is not None)
    mod_map = (lambda i: ((i * tm) // seq_len, 0, 0)) if per_batch else (lambda i: (0, 0, 0))
    in_specs = [
        pl.BlockSpec((tm, D_MODEL), lambda i: (i, 0)),
        pl.BlockSpec((1, 9, D_MODEL), mod_map),
        _const_spec((1, D_MODEL), 1),
    ]
    args = [x, mod, npre]
    if pos is not None:
        tiles_per_seq = seq_len // tm
        in_specs.append(pl.BlockSpec((tm, D_MODEL), lambda i: (i % tiles_per_seq, 0)))
        args.append(pos)
    in_specs += [_const_spec((D_MODEL, P_W), 1), _const_spec((N_MIF, D_MODEL), 1)]
    args += [w_scan, w_mif_t]
    return pl.pallas_call(
        functools.partial(_proj_kernel, has_pos=pos is not None),
        grid=(t_total // tm,),
        in_specs=in_specs,
        out_specs=[pl.BlockSpec((tm, P_W), lambda i: (i, 0)),
                   pl.BlockSpec((N_MIF, tm), lambda i: (0, i))],
        out_shape=[jax.ShapeDtypeStruct((t_total, P_W), F32),
                   jax.ShapeDtypeStruct((N_MIF, t_total), F32)],
        compiler_params=pltpu.CompilerParams(
            dimension_semantics=("arbitrary",), vmem_limit_bytes=VMEM_LIMIT),
        name="proj",
    )(*args)


def _merge_kernel(*refs, has_pos):
    if has_pos:
        (x_ref, mod_ref, npre_ref, npost_ref, pos_ref, of_ref, ob_ref,
         wg_ref, hn_ref, wb_ref, wo_ref, o_ref) = refs
    else:
        (x_ref, mod_ref, npre_ref, npost_ref, of_ref, ob_ref,
         wg_ref, hn_ref, wb_ref, wo_ref, o_ref) = refs
    x = x_ref[...]
    h = _modulated(x, mod_ref, npre_ref, 3)
    if has_pos:
        h = h + pos_ref[...]
    hb = h.astype(BF16)
    merged = None
    for n in range(3):
        o = of_ref[:, n * MIX_W:(n + 1) * MIX_W] + ob_ref[:, n * MIX_W:(n + 1) * MIX_W]
        parts = []
        for hh in range(HEADS):
            oh = o[:, hh * DV:(hh + 1) * DV]
            ms = jnp.mean(oh * oh, axis=-1, keepdims=True)
            parts.append(oh * lax.rsqrt(ms + EPS))
        on = jnp.concatenate(parts, axis=1) * hn_ref[n:n + 1, :]
        gpre = jnp.dot(hb, wg_ref[:, n * MIX_W:(n + 1) * MIX_W], preferred_element_type=F32)
        sg = jax.nn.sigmoid(gpre)
        act = sg if n == 0 else gpre * sg
        ys = (act * on).astype(BF16)
        br = jnp.dot(ys, wb_ref[n], preferred_element_type=F32)
        lo = 3 * MIX_W + n * D_MODEL
        mg = jax.nn.sigmoid(jnp.dot(hb, wg_ref[:, lo:lo + D_MODEL], preferred_element_type=F32))
        merged = mg * br if merged is None else merged + mg * br
    out = jnp.dot(merged.astype(BF16), wo_ref[...], preferred_element_type=F32)
    gate = mod_ref[0, 5:6, :]
    o_ref[...] = x + gate * _rms(out, npost_ref[...])


def _merge(x, mod, npre, npost, pos, o_f, o_b, w_gate, head_norm, w_branch, w_out, *, seq_len):
    t_total = x.shape[0]
    per_batch = mod.shape[0] > 1
    tm = min(256, _row_tile(t_total, seq_len, per_batch or pos is not None))
    mod_map = (lambda i: ((i * tm) // seq_len, 0, 0)) if per_batch else (lambda i: (0, 0, 0))
    in_specs = [
        pl.BlockSpec((tm, D_MODEL), lambda i: (i, 0)),
        pl.BlockSpec((1, 9, D_MODEL), mod_map),
        _const_spec((1, D_MODEL), 1),
        _const_spec((1, D_MODEL), 1),
    ]
    args = [x, mod, npre, npost]
    if pos is not None:
        tiles_per_seq = seq_len // tm
        in_specs.append(pl.BlockSpec((tm, D_MODEL), lambda i: (i % tiles_per_seq, 0)))
        args.append(pos)
    in_specs += [
        pl.BlockSpec((tm, 3 * MIX_W), lambda i: (i, 0)),
        pl.BlockSpec((tm, 3 * MIX_W), lambda i: (i, 0)),
        _const_spec((D_MODEL, GATE_W), 1),
        _const_spec((3, MIX_W), 1),
        _const_spec((3, MIX_W, D_MODEL), 1),
        _const_spec((D_MODEL, D_MODEL), 1),
    ]
    args += [o_f, o_b, w_gate, head_norm, w_branch, w_out]
    return pl.pallas_call(
        functools.partial(_merge_kernel, has_pos=pos is not None),
        grid=(t_total // tm,),
        in_specs=in_specs,
        out_specs=pl.BlockSpec((tm, D_MODEL), lambda i: (i, 0)),
        out_shape=jax.ShapeDtypeStruct((t_total, D_MODEL), F32),
        compiler_params=pltpu.CompilerParams(
            dimension_semantics=("arbitrary",), vmem_limit_bytes=VMEM_LIMIT),
        name="merge",
    )(*args)


def _scan_constants():
    t = np.arange(CHUNK)
    lower = (t[None, :] <= t[:, None])
    lm = np.stack([lower, lower.T]).astype(np.float32)
    lvl = np.zeros((N_DIR, N_LEVELS, CHUNK, CHUNK), np.float32)
    lvl[:, 0] = np.eye(CHUNK)
    for p in range(1, N_LEVELS):
        half = 1 << (p - 1)
        same = (t[:, None] >> p) == (t[None, :] >> p)
        fwd = same & ((t[:, None] & half) != 0) & ((t[None, :] & half) == 0)
        lvl[0, p] = fwd
        lvl[1, p] = fwd.T
    tri = lvl.sum(axis=1)
    assert np.array_equal(tri[0], lower) and np.array_equal(tri[1], lower.T)
    lvl = np.tile(lvl, (1, 1, 1, HEADS))
    tri = np.tile(tri, (1, 1, HEADS))
    head_blk = np.kron(np.eye(HEADS), np.ones((CHUNK, CHUNK)))
    cumbd = np.stack([np.kron(np.eye(HEADS), lower.T), np.kron(np.eye(HEADS), lower)]).astype(np.float32)
    expf = np.zeros((N_DIR, LANE, QK_W), np.float32)
    expi = np.zeros((N_DIR, LANE, QK_W), np.float32)
    for d in range(N_DIR):
        for h in range(HEADS):
            expi[d, d * 2 * HEADS + h, h * DK:(h + 1) * DK] = 1.0
            expf[d, d * 2 * HEADS + HEADS + h, h * DK:(h + 1) * DK] = 1.0
    col_sel = np.zeros((QK_W, LANE), np.float32)
    for h in range(HEADS):
        col_sel[h * DK:(h + 1) * DK, h] = 1.0
    return dict(
        lm=jnp.asarray(lm), lvl=jnp.asarray(lvl), tri=jnp.asarray(tri), cumbd=jnp.asarray(cumbd),
        expf=jnp.asarray(expf), expi=jnp.asarray(expi),
        hmk=jnp.asarray(head_blk, dtype=BF16), onesbd=jnp.asarray(col_sel, dtype=BF16),
        cm=jnp.asarray(col_sel),
    )


def _log_sigmoid(x):
    return jnp.minimum(x, 0.0) - jnp.log(1.0 + jnp.exp(-jnp.abs(x)))


def _shift_rows(x, s):
    return pltpu.roll(x, s % CHUNK, 0)


def _block_diag(blocks):
    rows = []
    for h, b in enumerate(blocks):
        z = jnp.zeros_like(b)
        rows.append(jnp.concatenate([b if j == h else z for j in range(HEADS)], axis=1))
    return jnp.concatenate(rows, axis=0)


def _head_scores(qm, km, hmk_ref):
    kb = km.astype(BF16)
    kbd = jnp.concatenate([kb] * HEADS, axis=0) * hmk_ref[...]
    return lax.dot_general(qm.astype(BF16), kbd, (((1,), (1,)), ((), ())), preferred_element_type=F32)


def _gated_dir(q, k, v, la, s_ref, d, lm_ref, lvl_ref, hmk_ref):
    g = jnp.dot(lm_ref[d], la, precision=HIGHEST, preferred_element_type=F32)
    last = CHUNK - 1 if d == 0 else 0
    g_last = g[last:last + 1, :]
    row = lax.broadcasted_iota(jnp.int32, (CHUNK, QK_W), 0)
    a = _head_scores(q, k, hmk_ref) * lvl_ref[d, 0]
    z = g
    for p in range(1, N_LEVELS):
        half = 1 << (p - 1)
        upper = (row & half) != 0
        if d == 0:
            ref = jnp.where(upper, z, _shift_rows(z, -half))
            if p < N_LEVELS - 1:
                z = jnp.where(upper, _shift_rows(z, half), z)
        else:
            ref = jnp.where(upper, _shift_rows(z, half), z)
            if p < N_LEVELS - 1:
                z = jnp.where(upper, z, _shift_rows(z, -half))
        e = jnp.exp(-jnp.abs(g - ref))
        a = a + _head_scores(q * e, k * e, hmk_ref) * lvl_ref[d, p]
    qg = q * jnp.exp(g)
    kg = k * jnp.exp(g_last - g)
    vb = v.astype(BF16)
    s_old = [s_ref[d * HEADS + h] for h in range(HEADS)]
    rhs = jnp.concatenate([
        _block_diag([vb[:, h * DV:(h + 1) * DV] for h in range(HEADS)]),
        _block_diag([s.astype(BF16) for s in s_old]),
    ], axis=0)
    lhs = jnp.concatenate([a.astype(BF16), qg.astype(BF16)], axis=1)
    o = jnp.dot(lhs, rhs, preferred_element_type=F32)
    ds = lax.dot_general(kg.astype(BF16), vb, (((0,), (0,)), ((), ())), preferred_element_type=F32)
    gl_col = lax.dot_general(la, jnp.ones((CHUNK, DV), F32), (((0,), (0,)), ((), ())),
                             precision=HIGHEST, preferred_element_type=F32)
    dec = jnp.exp(gl_col)
    for h in range(HEADS):
        s_ref[d * HEADS + h] = (dec[h * DK:(h + 1) * DK, :] * s_old[h]
                                + ds[h * DK:(h + 1) * DK, h * DV:(h + 1) * DV])
    return o


def _mlstm_dir(q, k, v, sm, grow, c_ref, n_ref, m_ref, d, cst, biasc_ref, biasr_ref):
    lm_ref, tri_ref, cumbd_ref, expf_ref, expi_ref, hmk_ref, onesbd_ref, cm_ref = cst
    neg_inf = float("-inf")
    gates = sm + biasc_ref[...]
    lf_exp = jnp.dot(_log_sigmoid(gates), expf_ref[d], precision=HIGHEST, preferred_element_type=F32)
    li_exp = jnp.dot(gates, expi_ref[d], precision=HIGHEST, preferred_element_type=F32)
    b_exp = jnp.dot(lm_ref[d], lf_exp, precision=HIGHEST, preferred_element_type=F32)
    gr = grow + biasr_ref[...]
    li_row = gr[2 * d:2 * d + 1, :]
    b_rows = jnp.dot(_log_sigmoid(gr), cumbd_ref[d], precision=HIGHEST, preferred_element_type=F32)
    b_row = b_rows[2 * d + 1:2 * d + 2, :]
    m_prev = m_ref[d:d + 1, :]
    inter = b_exp + m_prev
    dmat = jnp.where(tri_ref[d] > 0.0, b_exp - b_row + li_row, neg_inf)
    lane_head = lax.broadcasted_iota(jnp.int32, (CHUNK, QK_W), 1) // DK
    mt = inter
    for h in range(HEADS):
        sel = lane_head == h
        rm = jnp.max(jnp.where(sel, dmat, neg_inf), axis=-1, keepdims=True)
        mt = jnp.where(sel, jnp.maximum(inter, rm), mt)
    sc = _head_scores(q, k, hmk_ref) * jnp.exp(dmat - mt)
    e_int = jnp.exp(inter - mt)
    vb = v.astype(BF16)
    c_old = [c_ref[d * HEADS + h] for h in range(HEADS)]
    n_old = n_ref[d]
    rhs_top = jnp.concatenate([_block_diag([vb[:, h * DV:(h + 1) * DV] for h in range(HEADS)]),
                               onesbd_ref[...]], axis=1)
    rhs_bot = jnp.concatenate([_block_diag([cc.astype(BF16) for cc in c_old]), n_old.astype(BF16)], axis=1)
    lhs = jnp.concatenate([sc.astype(BF16), (q * e_int).astype(BF16)], axis=1)
    nd = jnp.dot(lhs, jnp.concatenate([rhs_top, rhs_bot], axis=0), preferred_element_type=F32)
    outs = []
    for h in range(HEADS):
        den = nd[:, MIX_W + h:MIX_W + h + 1]
        mth = mt[:, h * DK:h * DK + 1]
        outs.append(nd[:, h * DV:(h + 1) * DV] / jnp.maximum(jnp.abs(den), jnp.exp(-mth)))
    o = jnp.concatenate(outs, axis=1)
    last = CHUNK - 1 if d == 0 else 0
    b_last = b_exp[last:last + 1, :]
    lw = b_last - b_exp + li_exp
    m_new = jnp.maximum(b_last + m_prev, jnp.max(lw, axis=0, keepdims=True))
    ws = jnp.exp(lw - m_new)
    dec_row = jnp.exp(b_last + m_prev - m_new)
    vaug = jnp.concatenate([vb, jnp.ones((CHUNK, LANE), BF16)], axis=1)
    dc = lax.dot_general((k * ws).astype(BF16), vaug, (((0,), (0,)), ((), ())), preferred_element_type=F32)
    row_head = lax.broadcasted_iota(jnp.int32, (QK_W, LANE), 0) // DK
    dec_col = jnp.zeros((QK_W, LANE), F32)
    for h in range(HEADS):
        dec_col = jnp.where(row_head == h, dec_row[:, h * DK:h * DK + 1], dec_col)
    for h in range(HEADS):
        c_ref[d * HEADS + h] = (dec_col[h * DK:(h + 1) * DK, :] * c_old[h]
                                + dc[h * DK:(h + 1) * DK, h * DV:(h + 1) * DV])
    n_ref[d] = dec_col * n_old + cm_ref[...] * dc[:, MIX_W:MIX_W + LANE]
    m_ref[d:d + 1, :] = m_new
    return o


def _scan_kernel(*refs, layer, has_init, emit_state):
    (pf_ref, pb_ref, grf_ref, grb_ref, lm_ref, lvl_ref, tri_ref, cumbd_ref, expf_ref, expi_ref,
     hmk_ref, onesbd_ref, cm_ref, biasc_ref, biasr_ref, wup_ref, glab_ref, gamma_ref) = refs[:18]
    pos = 18
    if has_init:
        c0_ref, n0_ref, m0_ref, g0_ref, h0_ref = refs[pos:pos + 5]
        pos += 5
    of_ref, ob_ref = refs[pos:pos + 2]
    pos += 2
    if emit_state:
        cout_ref, nout_ref, mout_ref, gout_ref, hout_ref = refs[pos:pos + 5]
        pos += 5
    c_scr, n_scr, m_scr, g_scr, h_scr = refs[pos:pos + 5]

    ci = pl.program_id(1)

    @pl.when(ci == 0)
    def _():
        if has_init:
            c_scr[...] = c0_ref[0]
            g_scr[...] = g0_ref[0]
            h_scr[...] = h0_ref[0]
            for d in range(N_DIR):
                n_scr[d] = cm_ref[...] * n0_ref[0, d]
            m_scr[...] = jnp.zeros(m_scr.shape, F32)
            m_scr[0:N_DIR, :] = m0_ref[0]
        else:
            c_scr[...] = jnp.zeros(c_scr.shape, F32)
            g_scr[...] = jnp.zeros(g_scr.shape, F32)
            h_scr[...] = jnp.zeros(h_scr.shape, F32)
            n_scr[...] = jnp.zeros(n_scr.shape, F32)
            m_scr[...] = jnp.zeros(m_scr.shape, F32)

    gam = gamma_ref[...]
    ge = jnp.exp(gam - jnp.max(gam, axis=0, keepdims=True))
    pg = ge / jnp.sum(ge, axis=0, keepdims=True)
    cs = pg[0:1, :]
    for j in range(1, layer + 1):
        cs = cs + pg[j:j + 1, :]
    lb = cs - pg[0:1, :]

    mcst = (lm_ref, tri_ref, cumbd_ref, expf_ref, expi_ref, hmk_ref, onesbd_ref, cm_ref)
    for d, (p_ref, gr_ref, o_ref) in enumerate(((pf_ref, grf_ref, of_ref), (pb_ref, grb_ref, ob_ref))):
        sm = p_ref[:, OFF_SM:OFF_SM + LANE]
        o_m = _mlstm_dir(p_ref[:, OFF_MQ:OFF_MQ + QK_W], p_ref[:, OFF_MK:OFF_MK + QK_W],
                         p_ref[:, OFF_MV:OFF_MV + MIX_W], sm, gr_ref[0], c_scr, n_scr, m_scr, d,
                         mcst, biasc_ref, biasr_ref)
        o_ref[:, 0:MIX_W] = o_m
        pre = jnp.dot(sm, wup_ref[d], precision=HIGHEST, preferred_element_type=F32) + glab_ref[d:d + 1, :]
        la_g = _log_sigmoid(pre) * (1.0 / G_TEMP)
        o_g = _gated_dir(p_ref[:, OFF_GQ:OFF_GQ + QK_W], p_ref[:, OFF_GK:OFF_GK + QK_W],
                         p_ref[:, OFF_GV:OFF_GV + MIX_W], la_g, g_scr, d, lm_ref, lvl_ref, hmk_ref)
        o_ref[:, MIX_W:2 * MIX_W] = o_g
        zz = p_ref[:, OFF_HF + d * QK_W:OFF_HF + (d + 1) * QK_W]
        ez = jnp.exp(-jnp.abs(zz))
        rz = 1.0 / (1.0 + ez)
        pos_z = zz >= 0.0
        sig = jnp.where(pos_z, rz, ez * rz)
        nsig = jnp.where(pos_z, ez * rz, rz)
        la_h = jnp.log(lb + (1.0 - lb) * sig)
        key_h = (1.0 - lb) * nsig
        hv = p_ref[:, OFF_HV:OFF_HV + MIX_W]
        o_h = _gated_dir(p_ref[:, OFF_HQ:OFF_HQ + QK_W], key_h, hv * jax.nn.sigmoid(hv), la_h,
                         h_scr, d, lm_ref, lvl_ref, hmk_ref)
        o_ref[:, 2 * MIX_W:3 * MIX_W] = o_h

    if emit_state:
        @pl.when(ci == pl.num_programs(1) - 1)
        def _():
            cout_ref[0] = c_scr[...]
            gout_ref[0] = g_scr[...]
            hout_ref[0] = h_scr[...]
            for d in range(N_DIR):
                nout_ref[0, d] = jnp.sum(n_scr[d], axis=-1, keepdims=True)
            mout_ref[0] = m_scr[0:N_DIR, :]


def _scan(p, g_rows, cst, biasc, biasr, wup, glab, gamma, init, *, layer, nbatch, seq_len, emit_state):
    t_total = p.shape[0]
    nc = seq_len // CHUNK
    has_init = init is not None
    fwd = lambda b, c: (b * nc + c, 0)
    bwd = lambda b, c: (b * nc + nc - 1 - c, 0)
    fwd3 = lambda b, c: (b * nc + c, 0, 0)
    bwd3 = lambda b, c: (b * nc + nc - 1 - c, 0, 0)
    const_args = [cst["lm"], cst["lvl"], cst["tri"], cst["cumbd"], cst["expf"], cst["expi"],
                  cst["hmk"], cst["onesbd"], cst["cm"], biasc, biasr, wup, glab, gamma]
    in_specs = [
        pl.BlockSpec((CHUNK, P_W), fwd),
        pl.BlockSpec((CHUNK, P_W), bwd),
        pl.BlockSpec((1, 8, QK_W), fwd3),
        pl.BlockSpec((1, 8, QK_W), bwd3),
    ] + [_const_spec(a.shape, 2) for a in const_args]
    args = [p, p, g_rows, g_rows] + const_args
    state_blk = (1, N_DIR * HEADS, DK, DV)
    state_map = lambda b, c: (b, 0, 0, 0)
    state_specs = [
        pl.BlockSpec(state_blk, state_map),
        pl.BlockSpec((1, N_DIR, QK_W, 1), state_map),
        pl.BlockSpec((1, N_DIR, QK_W), lambda b, c: (b, 0, 0)),
        pl.BlockSpec(state_blk, state_map),
        pl.BlockSpec(state_blk, state_map),
    ]
    if has_init:
        in_specs += state_specs
        args += list(init)
    out_specs = [pl.BlockSpec((CHUNK, 3 * MIX_W), fwd), pl.BlockSpec((CHUNK, 3 * MIX_W), bwd)]
    out_shape = [jax.ShapeDtypeStruct((t_total, 3 * MIX_W), F32)] * 2
    if emit_state:
        out_specs += state_specs
        out_shape += [
            jax.ShapeDtypeStruct((nbatch, N_DIR * HEADS, DK, DV), F32),
            jax.ShapeDtypeStruct((nbatch, N_DIR, QK_W, 1), F32),
            jax.ShapeDtypeStruct((nbatch, N_DIR, QK_W), F32),
            jax.ShapeDtypeStruct((nbatch, N_DIR * HEADS, DK, DV), F32),
            jax.ShapeDtypeStruct((nbatch, N_DIR * HEADS, DK, DV), F32),
        ]
    return pl.pallas_call(
        functools.partial(_scan_kernel, layer=layer, has_init=has_init, emit_state=emit_state),
        grid=(nbatch, nc),
        in_specs=in_specs,
        out_specs=out_specs,
        out_shape=out_shape,
        scratch_shapes=[
            pltpu.VMEM((N_DIR * HEADS, DK, DV), F32),
            pltpu.VMEM((N_DIR, QK_W, LANE), F32),
            pltpu.VMEM((8, QK_W), F32),
            pltpu.VMEM((N_DIR * HEADS, DK, DV), F32),
            pltpu.VMEM((N_DIR * HEADS, DK, DV), F32),
        ],
        compiler_params=pltpu.CompilerParams(
            dimension_semantics=("arbitrary", "arbitrary"), vmem_limit_bytes=VMEM_LIMIT),
        name="scan",
    )(*args)


def _grid_position(n_tokens):
    rows = n_tokens // GRID_W
    quarter = D_MODEL // 4
    freqs = jnp.exp(-math.log(10000.0) * jnp.arange(quarter, dtype=F32) / quarter)
    r = jnp.arange(rows, dtype=F32)[:, None] * freqs
    cl = jnp.arange(GRID_W, dtype=F32)[:, None] * freqs
    r_emb = jnp.concatenate([jnp.sin(r), jnp.cos(r)], axis=-1)
    c_emb = jnp.concatenate([jnp.sin(cl), jnp.cos(cl)], axis=-1)
    emb = jnp.concatenate([jnp.broadcast_to(r_emb[:, None], (rows, GRID_W, D_MODEL // 2)),
                           jnp.broadcast_to(c_emb[None], (rows, GRID_W, D_MODEL // 2))], axis=-1)
    return emb.reshape(rows * GRID_W, D_MODEL)


def _layer_weights(w_in_l, gate_bias_l, gla_w_up_l):
    offs = np.concatenate([[0], np.cumsum(IN_SIZES)])
    cols = [w_in_l[:, int(offs[i]):int(offs[i + 1])] for i in range(len(IN_SIZES))]
    (mq, mk, mv, mo, mif, gq, gk, gv, gr, glr, hq, hf, hv, hg, mg) = cols
    pad = jnp.zeros((D_MODEL, LANE - N_MIF - N_GLR), F32)
    w_scan = jnp.concatenate([mq * (DK ** -0.5), mk, mv, gq * (DK ** -0.5), gk, gv, hq, hf, hv, mif, glr, pad],
                             axis=1).astype(BF16)
    w_mif_t = mif.T.astype(BF16)
    w_gate = jnp.concatenate([mo, gr, hg, mg], axis=1).astype(BF16)
    bias = gate_bias_l.reshape(N_MIF)
    biasc = jnp.concatenate([bias, jnp.zeros((LANE - N_MIF,), F32)])[None, :]
    biasr = jnp.concatenate([jnp.repeat(gate_bias_l.reshape(2 * N_DIR, HEADS), CHUNK, axis=1),
                             jnp.zeros((8 - 2 * N_DIR, QK_W), F32)], axis=0)
    wup = jnp.zeros((N_DIR, LANE, QK_W), F32)
    for d in range(N_DIR):
        wup = wup.at[d, N_MIF + d * G_RANK:N_MIF + (d + 1) * G_RANK, :].set(gla_w_up_l[d])
    return w_scan, w_mif_t, w_gate, biasc, biasr, wup


def _gate_rows(g_t, t_total):
    nchunks = t_total // CHUNK
    g = g_t.reshape(N_DIR, 2, HEADS, nchunks, CHUNK).transpose(3, 0, 1, 2, 4).reshape(nchunks, 2 * N_DIR, QK_W)
    return jnp.concatenate([g, jnp.zeros((nchunks, 8 - 2 * N_DIR, QK_W), F32)], axis=1)


def _trunk_layer(x, mod, pos, init, lw, cst, *, layer, nbatch, seq_len, emit_state):
    x = _ffn(x, mod, lw["npre"][0], lw["npost"][0], lw["w1"][0], lw["w2"][0], mi=0, seq_len=seq_len)
    p, g_t = _proj(x, mod, lw["npre"][1], pos, lw["w_scan"], lw["w_mif_t"], seq_len=seq_len)
    res = _scan(p, _gate_rows(g_t, x.shape[0]), cst, lw["biasc"], lw["biasr"], lw["wup"], lw["glab"],
                lw["gamma"], init, layer=layer, nbatch=nbatch, seq_len=seq_len, emit_state=emit_state)
    x = _merge(x, mod, lw["npre"][1], lw["npost"][1], pos, res[0], res[1], lw["w_gate"], lw["head_norm"],
               lw["w_branch"], lw["w_out"], seq_len=seq_len)
    x = _ffn(x, mod, lw["npre"][2], lw["npost"][2], lw["w1"][1], lw["w2"][1], mi=6, seq_len=seq_len)
    return x, res[2:]


def kernel(x_prompt, x_sample, c, state_mlstm_C, state_mlstm_n, state_mlstm_m, state_gla_S, state_hgrn_S,
           c_ctx, w_ada, b_ada, norm_pre, norm_post, w_ffn_in, w_ffn_out, w_in, mlstm_gate_bias,
           gla_w_up, gla_b, hgrn_gamma, head_norm, w_branch, w_out):
    bp, tp, _ = x_prompt.shape
    bs, ts, _ = x_sample.shape
    depth = w_in.shape[0]
    cst = _scan_constants()

    n_c = 1 + bs
    rows = -(-n_c // 8) * 8
    cvec = jnp.concatenate([c_ctx[None, :], c, jnp.zeros((rows - n_c, D_MODEL), F32)], axis=0)
    mod_all = _ada(cvec, w_ada, b_ada)[:, :n_c].reshape(depth, n_c, 9, D_MODEL)

    pos = _grid_position(ts).astype(F32)
    xp = x_prompt.reshape(bp * tp, D_MODEL)
    xs = x_sample.reshape(bs * ts, D_MODEL)
    new_states = []
    for l in range(depth):
        w_scan, w_mif_t, w_gate, biasc, biasr, wup = _layer_weights(w_in[l], mlstm_gate_bias[l], gla_w_up[l])
        lw = dict(
            npre=[norm_pre[l, j][None, :] for j in range(3)],
            npost=[norm_post[l, j][None, :] for j in range(3)],
            w1=[w_ffn_in[l, j].astype(BF16) for j in range(2)],
            w2=[w_ffn_out[l, j].astype(BF16) for j in range(2)],
            w_scan=w_scan, w_mif_t=w_mif_t, w_gate=w_gate, biasc=biasc, biasr=biasr, wup=wup,
            glab=gla_b[l], gamma=hgrn_gamma,
            head_norm=head_norm[l], w_branch=w_branch[l].astype(BF16), w_out=w_out[l].astype(BF16),
        )
        xp, st = _trunk_layer(xp, mod_all[l, 0:1], None, None, lw, cst,
                              layer=l, nbatch=bp, seq_len=tp, emit_state=True)
        new_states.append(st)
        init = (
            state_mlstm_C[:, l].reshape(bs, N_DIR * HEADS, DK, DV),
            state_mlstm_n[:, l].reshape(bs, N_DIR, QK_W, 1),
            jnp.repeat(state_mlstm_m[:, l], DK, axis=-1),
            state_gla_S[:, l].reshape(bs, N_DIR * HEADS, DK, DV),
            state_hgrn_S[:, l].reshape(bs, N_DIR * HEADS, DK, DV),
        )
        xs, _ = _trunk_layer(xs, mod_all[l, 1:], pos, init, lw, cst,
                             layer=l, nbatch=bs, seq_len=ts, emit_state=False)

    blk = (bp, N_DIR, HEADS, DK, DV)
    new_c = jnp.stack([st[0].reshape(blk) for st in new_states], axis=1)
    new_n = jnp.stack([st[1].reshape(bp, N_DIR, HEADS, DK) for st in new_states], axis=1)
    new_m = jnp.stack([st[2][:, :, ::DK] for st in new_states], axis=1)
    new_g = jnp.stack([st[3].reshape(blk) for st in new_states], axis=1)
    new_h = jnp.stack([st[4].reshape(blk) for st in new_states], axis=1)
    return (xp.reshape(bp, tp, D_MODEL), xs.reshape(bs, ts, D_MODEL), new_c, new_n, new_m, new_g, new_h)
```

```python
import functools
import math

import numpy as np
import jax
import jax.numpy as jnp
from jax import lax
from jax.experimental import pallas as pl
from jax.experimental.pallas import tpu as pltpu

F32 = jnp.float32
BF16 = jnp.bfloat16
HIGHEST = lax.Precision.HIGHEST

D_MODEL = 1024
D_FF = 2816
GRID_W = 64
CHUNK = 64
EPS = 1e-6
N_DIR = 2
HEADS = 4
DK = 64
DV = 128
QK_W = HEADS * DK
MIX_W = HEADS * DV
G_RANK = 16
G_TEMP = 16.0
N_MIF = N_DIR * 2 * HEADS
N_GLR = N_DIR * G_RANK
LANE = 128
N_LEVELS = 7
LOG2E = 1.4426950408889634

OFF_MQ, OFF_MK, OFF_MV = 0, 256, 512
OFF_GQ, OFF_GK, OFF_GV = 1024, 1280, 1536
OFF_HQ, OFF_HF, OFF_HV = 2048, 2304, 2816
OFF_SM = 3328
P_W = OFF_SM + LANE
GATE_W = 3 * MIX_W + 3 * D_MODEL

IN_SIZES = (256, 256, 512, 512, N_MIF, 256, 256, 512, 512, N_GLR, 256, 512, 512, 512, 3 * D_MODEL)

FFN_CHUNK = 256
PROJ_CHUNK = 1152
VMEM_LIMIT = 56 * 1024 * 1024


def _const_spec(shape, grid_rank, single=True):
    zeros = (0,) * len(shape)
    if grid_rank == 1:
        imap = lambda i: zeros
    else:
        imap = lambda i, j: zeros
    if single:
        return pl.BlockSpec(shape, imap, pipeline_mode=pl.Buffered(1))
    return pl.BlockSpec(shape, imap)


def _rms(x, w):
    ms = jnp.mean(x * x, axis=-1, keepdims=True)
    return x * lax.rsqrt(ms + EPS) * w


def _modulated(x, mod_ref, npre_ref, mi):
    shift = mod_ref[0, mi:mi + 1, :]
    scale = mod_ref[0, mi + 1:mi + 2, :]
    return _rms(x, npre_ref[...]) * (1.0 + scale) + shift


def _row_tile(t_total, seq_len, per_batch_mod):
    for tm in (512, 256, 128, 64):
        if t_total % tm == 0 and (not per_batch_mod or seq_len % tm == 0):
            return tm
    raise ValueError("token count must be a multiple of 64")


def _ada_kernel(c_ref, w_ref, b_ref, o_ref):
    cv = c_ref[...]
    s = cv * jax.nn.sigmoid(cv)
    o_ref[0] = jnp.dot(s, w_ref[0], precision=HIGHEST, preferred_element_type=F32) + b_ref[0]


def _ada(cvec, w_ada, b_ada):
    depth, _, n = w_ada.shape
    tn = 1536
    rows = cvec.shape[0]
    return pl.pallas_call(
        _ada_kernel,
        grid=(depth, n // tn),
        in_specs=[
            pl.BlockSpec((rows, D_MODEL), lambda l, j: (0, 0)),
            pl.BlockSpec((1, D_MODEL, tn), lambda l, j: (l, 0, j)),
            pl.BlockSpec((1, 1, tn), lambda l, j: (l, 0, j)),
        ],
        out_specs=pl.BlockSpec((1, rows, tn), lambda l, j: (l, 0, j)),
        out_shape=jax.ShapeDtypeStruct((depth, rows, n), F32),
        compiler_params=pltpu.CompilerParams(
            dimension_semantics=("arbitrary", "arbitrary"), vmem_limit_bytes=VMEM_LIMIT),
        name="ada",
    )(cvec, w_ada, b_ada.reshape(depth, 1, n))


def _ffn_kernel(x_ref, mod_ref, npre_ref, npost_ref, w1_ref, w2_ref, o_ref, acc_ref, *, mi):
    x = x_ref[...]
    h = _modulated(x, mod_ref, npre_ref, mi).astype(BF16)
    for k in range(D_FF // FFN_CHUNK):
        lo = k * FFN_CHUNK
        g = jnp.dot(h, w1_ref[:, lo:lo + FFN_CHUNK], preferred_element_type=F32)
        u = jnp.dot(h, w1_ref[:, D_FF + lo:D_FF + lo + FFN_CHUNK], preferred_element_type=F32)
        a = (g * jax.nn.sigmoid(g) * u).astype(BF16)
        part = jnp.dot(a, w2_ref[lo:lo + FFN_CHUNK, :], preferred_element_type=F32)
        if k == 0:
            acc_ref[...] = part
        else:
            acc_ref[...] += part
    gate = mod_ref[0, mi + 2:mi + 3, :]
    o_ref[...] = x + (0.5 * gate) * _rms(acc_ref[...], npost_ref[...])


def _ffn(x, mod, npre, npost, w1, w2, *, mi, seq_len):
    t_total = x.shape[0]
    per_batch = mod.shape[0] > 1
    tm = _row_tile(t_total, seq_len, per_batch)
    mod_map = (lambda i: ((i * tm) // seq_len, 0, 0)) if per_batch else (lambda i: (0, 0, 0))
    return pl.pallas_call(
        functools.partial(_ffn_kernel, mi=mi),
        grid=(t_total // tm,),
        in_specs=[
            pl.BlockSpec((tm, D_MODEL), lambda i: (i, 0)),
            pl.BlockSpec((1, 9, D_MODEL), mod_map),
            _const_spec((1, D_MODEL), 1),
            _const_spec((1, D_MODEL), 1),
            _const_spec((D_MODEL, 2 * D_FF), 1),
            _const_spec((D_FF, D_MODEL), 1),
        ],
        out_specs=pl.BlockSpec((tm, D_MODEL), lambda i: (i, 0)),
        out_shape=jax.ShapeDtypeStruct((t_total, D_MODEL), F32),
        scratch_shapes=[pltpu.VMEM((tm, D_MODEL), F32)],
        compiler_params=pltpu.CompilerParams(
            dimension_semantics=("arbitrary",), vmem_limit_bytes=VMEM_LIMIT),
        name="ffn",
    )(x, mod, npre, npost, w1, w2)


def _proj_kernel(*refs, has_pos):
    if has_pos:
        x_ref, mod_ref, npre_ref, pos_ref, w_ref, wt_ref, o_ref, ot_ref = refs
    else:
        x_ref, mod_ref, npre_ref, w_ref, wt_ref, o_ref, ot_ref = refs
    h = _modulated(x_ref[...], mod_ref, npre_ref, 3)
    if has_pos:
        h = h + pos_ref[...]
    hb = h.astype(BF16)
    for j in range(P_W // PROJ_CHUNK):
        lo = j * PROJ_CHUNK
        o_ref[:, lo:lo + PROJ_CHUNK] = jnp.dot(hb, w_ref[:, lo:lo + PROJ_CHUNK], preferred_element_type=F32)
    ot_ref[...] = lax.dot_general(wt_ref[...], hb, (((1,), (1,)), ((), ())), preferred_element_type=F32)


def _proj(x, mod, npre, pos, w_scan, w_mif_t, *, seq_len):
    t_total = x.shape[0]
    per_batch = mod.shape[0] > 1
    tm = _row_tile(t_total, seq_len, per_batch or pos is not None)
    mod_map = (lambda i: ((i * tm) // seq_len, 0, 0)) if per_batch else (lambda i: (0, 0, 0))
    in_specs = [
        pl.BlockSpec((tm, D_MODEL), lambda i: (i, 0)),
        pl.BlockSpec((1, 9, D_MODEL), mod_map),
        _const_spec((1, D_MODEL), 1),
    ]
    args = [x, mod, npre]
    if pos is not None:
        tiles_per_seq = seq_len // tm
        in_specs.append(pl.BlockSpec((tm, D_MODEL), lambda i: (i % tiles_per_seq, 0)))
        args.append(pos)
    in_specs += [_const_spec((D_MODEL, P_W), 1), _const_spec((N_MIF, D_MODEL), 1)]
    args += [w_scan, w_mif_t]
    return pl.pallas_call(
        functools.partial(_proj_kernel, has_pos=pos is not None),
        grid=(t_total // tm,),
        in_specs=in_specs,
        out_specs=[pl.BlockSpec((tm, P_W), lambda i: (i, 0)),
                   pl.BlockSpec((N_MIF, tm), lambda i: (0, i))],
        out_shape=[jax.ShapeDtypeStruct((t_total, P_W), F32),
                   jax.ShapeDtypeStruct((N_MIF, t_total), F32)],
        compiler_params=pltpu.CompilerParams(
            dimension_semantics=("arbitrary",), vmem_limit_bytes=VMEM_LIMIT),
        name="proj",
    )(*args)


def _merge_kernel(*refs, has_pos):
    if has_pos:
        (x_ref, mod_ref, npre_ref, npost_ref, pos_ref, of_ref, ob_ref,
         wg_ref, hn_ref, wb_ref, wo_ref, o_ref) = refs
    else:
        (x_ref, mod_ref, npre_ref, npost_ref, of_ref, ob_ref,
         wg_ref, hn_ref, wb_ref, wo_ref, o_ref) = refs
    x = x_ref[...]
    h = _modulated(x, mod_ref, npre_ref, 3)
    if has_pos:
        h = h + pos_ref[...]
    hb = h.astype(BF16)
    merged = None
    for n in range(3):
        o = of_ref[:, n * MIX_W:(n + 1) * MIX_W] + ob_ref[:, n * MIX_W:(n + 1) * MIX_W]
        parts = []
        for hh in range(HEADS):
            oh = o[:, hh * DV:(hh + 1) * DV]
            ms = jnp.mean(oh * oh, axis=-1, keepdims=True)
            parts.append(oh * lax.rsqrt(ms + EPS))
        on = jnp.concatenate(parts, axis=1) * hn_ref[n:n + 1, :]
        gpre = jnp.dot(hb, wg_ref[:, n * MIX_W:(n + 1) * MIX_W], preferred_element_type=F32)
        sg = jax.nn.sigmoid(gpre)
        act = sg if n == 0 else gpre * sg
        ys = (act * on).astype(BF16)
        br = jnp.dot(ys, wb_ref[n], preferred_element_type=F32)
        lo = 3 * MIX_W + n * D_MODEL
        mg = jax.nn.sigmoid(jnp.dot(hb, wg_ref[:, lo:lo + D_MODEL], preferred_element_type=F32))
        merged = mg * br if merged is None else merged + mg * br
    out = jnp.dot(merged.astype(BF16), wo_ref[...], preferred_element_type=F32)
    gate = mod_ref[0, 5:6, :]
    o_ref[...] = x + gate * _rms(out, npost_ref[...])


def _merge(x, mod, npre, npost, pos, o_f, o_b, w_gate, head_norm, w_branch, w_out, *, seq_len):
    t_total = x.shape[0]
    per_batch = mod.shape[0] > 1
    tm = min(256, _row_tile(t_total, seq_len, per_batch or pos is not None))
    mod_map = (lambda i: ((i * tm) // seq_len, 0, 0)) if per_batch else (lambda i: (0, 0, 0))
    in_specs = [
        pl.BlockSpec((tm, D_MODEL), lambda i: (i, 0)),
        pl.BlockSpec((1, 9, D_MODEL), mod_map),
        _const_spec((1, D_MODEL), 1),
        _const_spec((1, D_MODEL), 1),
    ]
    args = [x, mod, npre, npost]
    if pos is not None:
        tiles_per_seq = seq_len // tm
        in_specs.append(pl.BlockSpec((tm, D_MODEL), lambda i: (i % tiles_per_seq, 0)))
        args.append(pos)
    in_specs += [
        pl.BlockSpec((tm, 3 * MIX_W), lambda i: (i, 0)),
        pl.BlockSpec((tm, 3 * MIX_W), lambda i: (i, 0)),
        _const_spec((D_MODEL, GATE_W), 1),
        _const_spec((3, MIX_W), 1),
        _const_spec((3, MIX_W, D_MODEL), 1),
        _const_spec((D_MODEL, D_MODEL), 1),
    ]
    args += [o_f, o_b, w_gate, head_norm, w_branch, w_out]
    return pl.pallas_call(
        functools.partial(_merge_kernel, has_pos=pos is not None),
        grid=(t_total // tm,),
        in_specs=in_specs,
        out_specs=pl.BlockSpec((tm, D_MODEL), lambda i: (i, 0)),
        out_shape=jax.ShapeDtypeStruct((t_total, D_MODEL), F32),
        compiler_params=pltpu.CompilerParams(
            dimension_semantics=("arbitrary",), vmem_limit_bytes=VMEM_LIMIT),
        name="merge",
    )(*args)


def _scan_constants():
    t = np.arange(CHUNK)
    lower = (t[None, :] <= t[:, None])
    lm = np.stack([lower, lower.T]).astype(np.float32)
    lvl = np.zeros((N_DIR, N_LEVELS, CHUNK, CHUNK), np.float32)
    lvl[:, 0] = np.eye(CHUNK)
    for p in range(1, N_LEVELS):
        half = 1 << (p - 1)
        same = (t[:, None] >> p) == (t[None, :] >> p)
        fwd = same & ((t[:, None] & half) != 0) & ((t[None, :] & half) == 0)
        lvl[0, p] = fwd
        lvl[1, p] = fwd.T
    tri = lvl.sum(axis=1)
    assert np.array_equal(tri[0], lower) and np.array_equal(tri[1], lower.T)
    lvl = np.tile(lvl, (1, 1, 1, HEADS))
    tri = np.tile(tri, (1, 1, HEADS))
    pair_blk = np.kron(np.eye(2), np.ones((CHUNK, DK)))
    sel = np.zeros((N_DIR, N_MIF, 2 * QK_W), np.float32)
    for d in range(N_DIR):
        for h in range(HEADS):
            sel[d, d * 2 * HEADS + HEADS + h, h * DK:(h + 1) * DK] = 1.0
            sel[d, d * 2 * HEADS + h, QK_W + h * DK:QK_W + (h + 1) * DK] = 1.0
    col_sel = np.zeros((QK_W, LANE), np.float32)
    for h in range(HEADS):
        col_sel[h * DK:(h + 1) * DK, h] = 1.0
    ones3 = np.zeros((16, LANE), np.float32)
    ones3[:3] = 1.0
    return dict(
        lm=jnp.asarray(lm, dtype=BF16), lvl=jnp.asarray(lvl), tri=jnp.asarray(tri),
        sel3=jnp.asarray(np.tile(sel, (1, 3, 1)), dtype=BF16),
        hm2=jnp.asarray(pair_blk, dtype=BF16), onesbd=jnp.asarray(col_sel, dtype=BF16),
        cm=jnp.asarray(col_sel), ones3=jnp.asarray(ones3, dtype=BF16),
    )


_TN = (((0,), (0,)), ((), ()))
_NT = (((1,), (1,)), ((), ()))


def _log_sigmoid(x):
    return jnp.minimum(x, 0.0) - jnp.log(1.0 + jnp.exp(-jnp.abs(x)))


def _shift_rows(x, s):
    return pltpu.roll(x, s % CHUNK, 0)


def _split3(x):
    hi = x.astype(BF16)
    r1 = x - hi.astype(F32)
    mid = r1.astype(BF16)
    lo = (r1 - mid.astype(F32)).astype(BF16)
    return hi, mid, lo


def _cumsum_rows_issue(x, lm_bf):
    return jnp.dot(lm_bf, jnp.concatenate(_split3(x), axis=1), preferred_element_type=F32)


def _cumsum_rows_finish(c):
    w = c.shape[1] // 3
    return (c[:, :w] + c[:, w:2 * w]) + c[:, 2 * w:]


def _to_column(row, ones3_ref):
    hi, mid, lo = _split3(row)
    stack = jnp.concatenate([hi.astype(F32), mid.astype(F32), lo.astype(F32),
                             jnp.zeros((13, row.shape[1]), F32)], axis=0).astype(BF16)
    return lax.dot_general(stack, ones3_ref[...], _TN, preferred_element_type=F32)


def _block_diag2(a, b):
    return jnp.concatenate([jnp.concatenate([a, jnp.zeros_like(b)], axis=1),
                            jnp.concatenate([jnp.zeros_like(a), b], axis=1)], axis=0)


def _pair_scores(xq, xk, hm2_ref):
    outs = []
    for j in range(2):
        kj = xk[:, j * LANE:(j + 1) * LANE]
        kbd = jnp.concatenate([kj, kj], axis=0) * hm2_ref[...]
        outs.append(lax.dot_general(xq[:, j * LANE:(j + 1) * LANE], kbd, _NT, preferred_element_type=F32))
    return jnp.concatenate(outs, axis=1)


def _gated_dir(q, k, v, la, s_ref, d, cst, o_ref, o_lo):
    csum = _cumsum_rows_issue(la, cst["lm"][d])
    vb = v.astype(BF16)
    s_old = [s_ref[d * HEADS + h] for h in range(HEADS)]
    yield
    g2 = _cumsum_rows_finish(csum) * LOG2E
    last = CHUNK - 1 if d == 0 else 0
    g2_last = g2[last:last + 1, :]
    row = lax.broadcasted_iota(jnp.int32, (CHUNK, QK_W), 0)
    scores = [_pair_scores(q.astype(BF16), k.astype(BF16), cst["hm2"])]
    z = g2
    for p in range(1, N_LEVELS):
        half = 1 << (p - 1)
        upper = (row & half) != 0
        if d == 0:
            ref = jnp.where(upper, z, _shift_rows(z, -half))
            if p < N_LEVELS - 1:
                z = jnp.where(upper, _shift_rows(z, half), z)
            w = jnp.where(upper, q, k)
        else:
            ref = jnp.where(upper, _shift_rows(z, half), z)
            if p < N_LEVELS - 1:
                z = jnp.where(upper, z, _shift_rows(z, -half))
            w = jnp.where(upper, k, q)
        x = (w * jnp.exp2(-jnp.abs(g2 - ref))).astype(BF16)
        scores.append(_pair_scores(x, x, cst["hm2"]))
    qg = (q * jnp.exp2(g2)).astype(BF16)
    kg = (k * jnp.exp2(g2_last - g2)).astype(BF16)
    dec = _to_column(jnp.exp2(g2_last), cst["ones3"])
    dss = [lax.dot_general(kg[:, j * LANE:(j + 1) * LANE], vb[:, 2 * j * DV:(2 * j + 2) * DV], _TN,
                           preferred_element_type=F32) for j in range(2)]
    yield
    a = scores[0] * cst["lvl"][d, 0]
    for p in range(1, N_LEVELS):
        a = a + scores[p] * cst["lvl"][d, p]
    ab = a.astype(BF16)
    outs = []
    for j in range(2):
        rhs = jnp.concatenate([
            _block_diag2(vb[:, 2 * j * DV:(2 * j + 1) * DV], vb[:, (2 * j + 1) * DV:(2 * j + 2) * DV]),
            _block_diag2(s_old[2 * j].astype(BF16), s_old[2 * j + 1].astype(BF16)),
        ], axis=0)
        lhs = jnp.concatenate([ab[:, j * LANE:(j + 1) * LANE], qg[:, j * LANE:(j + 1) * LANE]], axis=1)
        outs.append(jnp.dot(lhs, rhs, preferred_element_type=F32))
    yield
    for j in range(2):
        for h2 in range(2):
            h = 2 * j + h2
            s_ref[d * HEADS + h] = (dec[h * DK:(h + 1) * DK, :] * s_old[h]
                                    + dss[j][h2 * DK:(h2 + 1) * DK, h2 * DV:(h2 + 1) * DV])
    o_ref[:, o_lo:o_lo + MIX_W] = jnp.concatenate(outs, axis=1)


def _mlstm_dir(q, k, v, g16, c_ref, n_ref, m_ref, d, cst, o_ref):
    neg_inf = float("-inf")
    pre = g16 + cst["bias16"][...]
    cum3 = jnp.dot(jnp.concatenate(_split3(_log_sigmoid(pre)), axis=0), cst["lm"][1 - d],
                   preferred_element_type=F32)
    qk = _pair_scores(q.astype(BF16), k.astype(BF16), cst["hm2"])
    vb = v.astype(BF16)
    c_old = [c_ref[d * HEADS + h] for h in range(HEADS)]
    n_old = n_ref[d]
    yield
    cum = (cum3[0:16] + cum3[16:32]) + cum3[32:48]
    is_f = (lax.broadcasted_iota(jnp.int32, (N_MIF, CHUNK), 0) & HEADS) != 0
    y = jnp.where(is_f, cum, pre)
    be = lax.dot_general(jnp.concatenate(_split3(y), axis=0), cst["sel3"][d], _TN,
                         preferred_element_type=F32)
    yield
    b_exp, li_exp = be[:, :QK_W], be[:, QK_W:]
    r0 = d * 2 * HEADS
    li_row = jnp.concatenate([y[r0 + h:r0 + h + 1, :] for h in range(HEADS)], axis=1)
    b_row = jnp.concatenate([y[r0 + HEADS + h:r0 + HEADS + h + 1, :] for h in range(HEADS)], axis=1)
    m_prev = m_ref[d:d + 1, :]
    inter = b_exp + m_prev
    dmat = jnp.where(cst["tri"][d] > 0.0, b_exp - b_row + li_row, neg_inf)
    lane_head = lax.broadcasted_iota(jnp.int32, (CHUNK, QK_W), 1) // DK
    mt = inter
    for h in range(HEADS):
        sel = lane_head == h
        rm = jnp.max(jnp.where(sel, dmat, neg_inf), axis=-1, keepdims=True)
        mt = jnp.where(sel, jnp.maximum(inter, rm), mt)
    sc = (qk * jnp.exp(dmat - mt)).astype(BF16)
    qe = (q * jnp.exp(inter - mt)).astype(BF16)
    last = CHUNK - 1 if d == 0 else 0
    b_last = b_exp[last:last + 1, :]
    lw = b_last - b_exp + li_exp
    m_new = jnp.maximum(b_last + m_prev, jnp.max(lw, axis=0, keepdims=True))
    kw = (k * jnp.exp(lw - m_new)).astype(BF16)
    dec_col = _to_column(jnp.exp(b_last + m_prev - m_new), cst["ones3"])
    ones_blk = jnp.ones((CHUNK, LANE), BF16)
    nds, dcs = [], []
    for j in range(2):
        rows = slice(j * LANE, (j + 1) * LANE)
        v_pair = vb[:, 2 * j * DV:(2 * j + 2) * DV]
        top = jnp.concatenate([_block_diag2(v_pair[:, :DV], v_pair[:, DV:]), cst["onesbd"][rows, :]], axis=1)
        bot = jnp.concatenate([_block_diag2(c_old[2 * j].astype(BF16), c_old[2 * j + 1].astype(BF16)),
                               n_old[rows, :].astype(BF16)], axis=1)
        lhs = jnp.concatenate([sc[:, rows], qe[:, rows]], axis=1)
        nds.append(jnp.dot(lhs, jnp.concatenate([top, bot], axis=0), preferred_element_type=F32))
        dcs.append(lax.dot_general(kw[:, rows], jnp.concatenate([v_pair, ones_blk], axis=1), _TN,
                                   preferred_element_type=F32))
    yield
    outs = []
    for j in range(2):
        rows = slice(j * LANE, (j + 1) * LANE)
        nd, dc = nds[j], dcs[j]
        for h2 in range(2):
            h = 2 * j + h2
            den = nd[:, 2 * DV + h:2 * DV + h + 1]
            mth = mt[:, h * DK:h * DK + 1]
            outs.append(nd[:, h2 * DV:(h2 + 1) * DV] / jnp.maximum(jnp.abs(den), jnp.exp(-mth)))
            c_ref[d * HEADS + h] = (dec_col[h * DK:(h + 1) * DK, :] * c_old[h]
                                    + dc[h2 * DK:(h2 + 1) * DK, h2 * DV:(h2 + 1) * DV])
        n_ref[d, rows, :] = dec_col[rows, :] * n_old[rows, :] + cst["cm"][rows, :] * dc[:, 2 * DV:]
    m_ref[d:d + 1, :] = m_new
    o_ref[:, 0:MIX_W] = jnp.concatenate(outs, axis=1)


_DONE = object()
_SCAN_CONSTS = ("lm", "lvl", "tri", "sel3", "hm2", "onesbd", "cm", "ones3", "bias16", "wup3", "glab", "gamma")


def _scan_kernel(*refs, layer, has_init, emit_state):
    pf_ref, pb_ref, gf_ref, gb_ref = refs[:4]
    pos = 4
    cst = dict(zip(_SCAN_CONSTS, refs[pos:pos + len(_SCAN_CONSTS)]))
    pos += len(_SCAN_CONSTS)
    if has_init:
        c0_ref, n0_ref, m0_ref, g0_ref, h0_ref = refs[pos:pos + 5]
        pos += 5
    of_ref, ob_ref = refs[pos:pos + 2]
    pos += 2
    if emit_state:
        cout_ref, nout_ref, mout_ref, gout_ref, hout_ref = refs[pos:pos + 5]
        pos += 5
    c_scr, n_scr, m_scr, g_scr, h_scr = refs[pos:pos + 5]

    ci = pl.program_id(1)

    @pl.when(ci == 0)
    def _():
        if has_init:
            c_scr[...] = c0_ref[0]
            g_scr[...] = g0_ref[0]
            h_scr[...] = h0_ref[0]
            for d in range(N_DIR):
                n_scr[d] = cst["cm"][...] * n0_ref[0, d]
            m_scr[...] = jnp.zeros(m_scr.shape, F32)
            m_scr[0:N_DIR, :] = m0_ref[0]
        else:
            c_scr[...] = jnp.zeros(c_scr.shape, F32)
            g_scr[...] = jnp.zeros(g_scr.shape, F32)
            h_scr[...] = jnp.zeros(h_scr.shape, F32)
            n_scr[...] = jnp.zeros(n_scr.shape, F32)
            m_scr[...] = jnp.zeros(m_scr.shape, F32)

    gam = cst["gamma"][...]
    ge = jnp.exp(gam - jnp.max(gam, axis=0, keepdims=True))
    pg = ge / jnp.sum(ge, axis=0, keepdims=True)
    cs = pg[0:1, :]
    for j in range(1, layer + 1):
        cs = cs + pg[j:j + 1, :]
    lb = cs - pg[0:1, :]

    units = []
    for d, (p_ref, g_ref, o_ref) in enumerate(((pf_ref, gf_ref, of_ref), (pb_ref, gb_ref, ob_ref))):
        units.append(_mlstm_dir(
            p_ref[:, OFF_MQ:OFF_MQ + QK_W], p_ref[:, OFF_MK:OFF_MK + QK_W], p_ref[:, OFF_MV:OFF_MV + MIX_W],
            g_ref[0], c_scr, n_scr, m_scr, d, cst, o_ref))
        sm = p_ref[:, OFF_SM:OFF_SM + LANE]
        sm_hi = sm.astype(BF16)
        sm_lo = (sm - sm_hi.astype(F32)).astype(BF16)
        pre = jnp.dot(jnp.concatenate([sm_hi, sm_lo, sm_hi], axis=1), cst["wup3"][d],
                      preferred_element_type=F32) + cst["glab"][d:d + 1, :]
        units.append(_gated_dir(
            p_ref[:, OFF_GQ:OFF_GQ + QK_W], p_ref[:, OFF_GK:OFF_GK + QK_W], p_ref[:, OFF_GV:OFF_GV + MIX_W],
            _log_sigmoid(pre) * (1.0 / G_TEMP), g_scr, d, cst, o_ref, MIX_W))
        zz = p_ref[:, OFF_HF + d * QK_W:OFF_HF + (d + 1) * QK_W]
        ez = jnp.exp(-jnp.abs(zz))
        rz = 1.0 / (1.0 + ez)
        pos_z = zz >= 0.0
        sig = jnp.where(pos_z, rz, ez * rz)
        nsig = jnp.where(pos_z, ez * rz, rz)
        hv = p_ref[:, OFF_HV:OFF_HV + MIX_W]
        units.append(_gated_dir(
            p_ref[:, OFF_HQ:OFF_HQ + QK_W], (1.0 - lb) * nsig, hv * jax.nn.sigmoid(hv),
            jnp.log(lb + (1.0 - lb) * sig), h_scr, d, cst, o_ref, 2 * MIX_W))
    while units:
        units = [u for u in units if next(u, _DONE) is not _DONE]

    if emit_state:
        @pl.when(ci == pl.num_programs(1) - 1)
        def _():
            cout_ref[0] = c_scr[...]
            gout_ref[0] = g_scr[...]
            hout_ref[0] = h_scr[...]
            for d in range(N_DIR):
                nout_ref[0, d] = jnp.sum(n_scr[d], axis=-1, keepdims=True)
            mout_ref[0] = m_scr[0:N_DIR, :]


def _scan(p, g_rows, cst, init, *, layer, nbatch, seq_len, emit_state):
    t_total = p.shape[0]
    nc = seq_len // CHUNK
    has_init = init is not None
    fwd = lambda b, c: (b * nc + c, 0)
    bwd = lambda b, c: (b * nc + nc - 1 - c, 0)
    fwd3 = lambda b, c: (b * nc + c, 0, 0)
    bwd3 = lambda b, c: (b * nc + nc - 1 - c, 0, 0)
    const_args = [cst[name] for name in _SCAN_CONSTS]
    in_specs = [
        pl.BlockSpec((CHUNK, P_W), fwd),
        pl.BlockSpec((CHUNK, P_W), bwd),
        pl.BlockSpec((1, N_MIF, CHUNK), fwd3),
        pl.BlockSpec((1, N_MIF, CHUNK), bwd3),
    ] + [_const_spec(a.shape, 2) for a in const_args]
    args = [p, p, g_rows, g_rows] + const_args
    state_blk = (1, N_DIR * HEADS, DK, DV)
    state_map = lambda b, c: (b, 0, 0, 0)
    state_specs = [
        pl.BlockSpec(state_blk, state_map),
        pl.BlockSpec((1, N_DIR, QK_W, 1), state_map),
        pl.BlockSpec((1, N_DIR, QK_W), lambda b, c: (b, 0, 0)),
        pl.BlockSpec(state_blk, state_map),
        pl.BlockSpec(state_blk, state_map),
    ]
    if has_init:
        in_specs += state_specs
        args += list(init)
    out_specs = [pl.BlockSpec((CHUNK, 3 * MIX_W), fwd), pl.BlockSpec((CHUNK, 3 * MIX_W), bwd)]
    out_shape = [jax.ShapeDtypeStruct((t_total, 3 * MIX_W), F32)] * 2
    if emit_state:
        out_specs += state_specs
        out_shape += [
            jax.ShapeDtypeStruct((nbatch, N_DIR * HEADS, DK, DV), F32),
            jax.ShapeDtypeStruct((nbatch, N_DIR, QK_W, 1), F32),
            jax.ShapeDtypeStruct((nbatch, N_DIR, QK_W), F32),
            jax.ShapeDtypeStruct((nbatch, N_DIR * HEADS, DK, DV), F32),
            jax.ShapeDtypeStruct((nbatch, N_DIR * HEADS, DK, DV), F32),
        ]
    return pl.pallas_call(
        functools.partial(_scan_kernel, layer=layer, has_init=has_init, emit_state=emit_state),
        grid=(nbatch, nc),
        in_specs=in_specs,
        out_specs=out_specs,
        out_shape=out_shape,
        scratch_shapes=[
            pltpu.VMEM((N_DIR * HEADS, DK, DV), F32),
            pltpu.VMEM((N_DIR, QK_W, LANE), F32),
            pltpu.VMEM((8, QK_W), F32),
            pltpu.VMEM((N_DIR * HEADS, DK, DV), F32),
            pltpu.VMEM((N_DIR * HEADS, DK, DV), F32),
        ],
        compiler_params=pltpu.CompilerParams(
            dimension_semantics=("arbitrary", "arbitrary"), vmem_limit_bytes=VMEM_LIMIT),
        name="scan",
    )(*args)


def _grid_position(n_tokens):
    rows = n_tokens // GRID_W
    quarter = D_MODEL // 4
    freqs = jnp.exp(-math.log(10000.0) * jnp.arange(quarter, dtype=F32) / quarter)
    r = jnp.arange(rows, dtype=F32)[:, None] * freqs
    cl = jnp.arange(GRID_W, dtype=F32)[:, None] * freqs
    r_emb = jnp.concatenate([jnp.sin(r), jnp.cos(r)], axis=-1)
    c_emb = jnp.concatenate([jnp.sin(cl), jnp.cos(cl)], axis=-1)
    emb = jnp.concatenate([jnp.broadcast_to(r_emb[:, None], (rows, GRID_W, D_MODEL // 2)),
                           jnp.broadcast_to(c_emb[None], (rows, GRID_W, D_MODEL // 2))], axis=-1)
    return emb.reshape(rows * GRID_W, D_MODEL)


def _layer_weights(w_in_l, gate_bias_l, gla_w_up_l):
    offs = np.concatenate([[0], np.cumsum(IN_SIZES)])
    cols = [w_in_l[:, int(offs[i]):int(offs[i + 1])] for i in range(len(IN_SIZES))]
    (mq, mk, mv, mo, mif, gq, gk, gv, gr, glr, hq, hf, hv, hg, mg) = cols
    pad = jnp.zeros((D_MODEL, LANE - N_MIF - N_GLR), F32)
    w_scan = jnp.concatenate([mq * (DK ** -0.5), mk, mv, gq * (DK ** -0.5), gk, gv, hq, hf, hv, mif, glr, pad],
                             axis=1).astype(BF16)
    w_mif_t = mif.T.astype(BF16)
    w_gate = jnp.concatenate([mo, gr, hg, mg], axis=1).astype(BF16)
    bias16 = jnp.broadcast_to(gate_bias_l.reshape(N_MIF, 1), (N_MIF, CHUNK))
    wup = jnp.zeros((N_DIR, LANE, QK_W), F32)
    for d in range(N_DIR):
        wup = wup.at[d, N_MIF + d * G_RANK:N_MIF + (d + 1) * G_RANK, :].set(gla_w_up_l[d])
    wup_hi = wup.astype(BF16)
    wup_lo = (wup - wup_hi.astype(F32)).astype(BF16)
    wup3 = jnp.concatenate([wup_hi, wup_hi, wup_lo], axis=1)
    return w_scan, w_mif_t, w_gate, bias16, wup3


def _gate_rows(g_t, t_total):
    return g_t.reshape(N_MIF, t_total // CHUNK, CHUNK).transpose(1, 0, 2)


def _trunk_layer(x, mod, pos, init, lw, cst, *, layer, nbatch, seq_len, emit_state):
    x = _ffn(x, mod, lw["npre"][0], lw["npost"][0], lw["w1"][0], lw["w2"][0], mi=0, seq_len=seq_len)
    p, g_t = _proj(x, mod, lw["npre"][1], pos, lw["w_scan"], lw["w_mif_t"], seq_len=seq_len)
    scan_cst = dict(cst, bias16=lw["bias16"], wup3=lw["wup3"], glab=lw["glab"], gamma=lw["gamma"])
    res = _scan(p, _gate_rows(g_t, x.shape[0]), scan_cst, init,
                layer=layer, nbatch=nbatch, seq_len=seq_len, emit_state=emit_state)
    x = _merge(x, mod, lw["npre"][1], lw["npost"][1], pos, res[0], res[1], lw["w_gate"], lw["head_norm"],
               lw["w_branch"], lw["w_out"], seq_len=seq_len)
    x = _ffn(x, mod, lw["npre"][2], lw["npost"][2], lw["w1"][1], lw["w2"][1], mi=6, seq_len=seq_len)
    return x, res[2:]


def kernel(x_prompt, x_sample, c, state_mlstm_C, state_mlstm_n, state_mlstm_m, state_gla_S, state_hgrn_S,
           c_ctx, w_ada, b_ada, norm_pre, norm_post, w_ffn_in, w_ffn_out, w_in, mlstm_gate_bias,
           gla_w_up, gla_b, hgrn_gamma, head_norm, w_branch, w_out):
    bp, tp, _ = x_prompt.shape
    bs, ts, _ = x_sample.shape
    depth = w_in.shape[0]
    cst = _scan_constants()

    n_c = 1 + bs
    rows = -(-n_c // 8) * 8
    cvec = jnp.concatenate([c_ctx[None, :], c, jnp.zeros((rows - n_c, D_MODEL), F32)], axis=0)
    mod_all = _ada(cvec, w_ada, b_ada)[:, :n_c].reshape(depth, n_c, 9, D_MODEL)

    pos = _grid_position(ts).astype(F32)
    xp = x_prompt.reshape(bp * tp, D_MODEL)
    xs = x_sample.reshape(bs * ts, D_MODEL)
    new_states = []
    for l in range(depth):
        w_scan, w_mif_t, w_gate, bias16, wup3 = _layer_weights(w_in[l], mlstm_gate_bias[l], gla_w_up[l])
        lw = dict(
            npre=[norm_pre[l, j][None, :] for j in range(3)],
            npost=[norm_post[l, j][None, :] for j in range(3)],
            w1=[w_ffn_in[l, j].astype(BF16) for j in range(2)],
            w2=[w_ffn_out[l, j].astype(BF16) for j in range(2)],
            w_scan=w_scan, w_mif_t=w_mif_t, w_gate=w_gate, bias16=bias16, wup3=wup3,
            glab=gla_b[l], gamma=hgrn_gamma,
            head_norm=head_norm[l], w_branch=w_branch[l].astype(BF16), w_out=w_out[l].astype(BF16),
        )
        xp, st = _trunk_layer(xp, mod_all[l, 0:1], None, None, lw, cst,
                              layer=l, nbatch=bp, seq_len=tp, emit_state=True)
        new_states.append(st)
        init = (
            state_mlstm_C[:, l].reshape(bs, N_DIR * HEADS, DK, DV),
            state_mlstm_n[:, l].reshape(bs, N_DIR, QK_W, 1),
            jnp.repeat(state_mlstm_m[:, l], DK, axis=-1),
            state_gla_S[:, l].reshape(bs, N_DIR * HEADS, DK, DV),
            state_hgrn_S[:, l].reshape(bs, N_DIR * HEADS, DK, DV),
        )
        xs, _ = _trunk_layer(xs, mod_all[l, 1:], pos, init, lw, cst,
                             layer=l, nbatch=bs, seq_len=ts, emit_state=False)

    blk = (bp, N_DIR, HEADS, DK, DV)
    new_c = jnp.stack([st[0].reshape(blk) for st in new_states], axis=1)
    new_n = jnp.stack([st[1].reshape(bp, N_DIR, HEADS, DK) for st in new_states], axis=1)
    new_m = jnp.stack([st[2][:, :, ::DK] for st in new_states], axis=1)
    new_g = jnp.stack([st[3].reshape(blk) for st in new_states], axis=1)
    new_h = jnp.stack([st[4].reshape(blk) for st in new_states], axis=1)
    return (xp.reshape(bp, tp, D_MODEL), xs.reshape(bs, ts, D_MODEL), new_c, new_n, new_m, new_g, new_h)
```

```python
import functools
import math

import numpy as np
import jax
import jax.numpy as jnp
from jax import lax
from jax.experimental import pallas as pl
from jax.experimental.pallas import tpu as pltpu

F32 = jnp.float32
BF16 = jnp.bfloat16
HIGHEST = lax.Precision.HIGHEST

D_MODEL = 1024
D_FF = 2816
GRID_W = 64
CHUNK = 64
EPS = 1e-6
N_DIR = 2
HEADS = 4
DK = 64
DV = 128
QK_W = HEADS * DK
MIX_W = HEADS * DV
G_RANK = 16
G_TEMP = 16.0
N_MIF = N_DIR * 2 * HEADS
N_GLR = N_DIR * G_RANK
LANE = 128
N_LEVELS = 7
LOG2E = 1.4426950408889634
SEQ_PER_STEP = 2

OFF_MQ, OFF_MK, OFF_MV = 0, 256, 512
OFF_GQ, OFF_GK, OFF_GV = 1024, 1280, 1536
OFF_HQ, OFF_HF, OFF_HV = 2048, 2304, 2816
OFF_SM = 3328
P_W = OFF_SM + LANE
GATE_W = 3 * MIX_W + 3 * D_MODEL

IN_SIZES = (256, 256, 512, 512, N_MIF, 256, 256, 512, 512, N_GLR, 256, 512, 512, 512, 3 * D_MODEL)

FFN_CHUNK = 256
PROJ_CHUNK = 1152
VMEM_LIMIT = 56 * 1024 * 1024


def _const_spec(shape, grid_rank, lead=()):
    index = tuple(lead) + (0,) * len(shape)
    block = (None,) * len(lead) + tuple(shape)
    if grid_rank == 1:
        imap = lambda i: index
    else:
        imap = lambda i, j: index
    return pl.BlockSpec(block, imap, pipeline_mode=pl.Buffered(1))


def _rms(x, w):
    ms = jnp.mean(x * x, axis=-1, keepdims=True)
    return x * lax.rsqrt(ms + EPS) * w


def _modulated(x, mod_ref, npre_ref, mi):
    shift = mod_ref[0, mi:mi + 1, :]
    scale = mod_ref[0, mi + 1:mi + 2, :]
    return _rms(x, npre_ref[...]) * (1.0 + scale) + shift


def _row_tile(t_total, seq_len, per_batch_mod):
    for tm in (512, 256, 128, 64):
        if t_total % tm == 0 and (not per_batch_mod or seq_len % tm == 0):
            return tm
    raise ValueError("token count must be a multiple of 64")


def _ada_kernel(c_ref, w_ref, b_ref, o_ref):
    cv = c_ref[...]
    s = cv * jax.nn.sigmoid(cv)
    o_ref[0] = jnp.dot(s, w_ref[0], precision=HIGHEST, preferred_element_type=F32) + b_ref[0]


def _ada(cvec, w_ada, b_ada):
    depth, _, n = w_ada.shape
    tn = 1536
    rows = cvec.shape[0]
    return pl.pallas_call(
        _ada_kernel,
        grid=(depth, n // tn),
        in_specs=[
            pl.BlockSpec((rows, D_MODEL), lambda l, j: (0, 0)),
            pl.BlockSpec((1, D_MODEL, tn), lambda l, j: (l, 0, j)),
            pl.BlockSpec((1, 1, tn), lambda l, j: (l, 0, j)),
        ],
        out_specs=pl.BlockSpec((1, rows, tn), lambda l, j: (l, 0, j)),
        out_shape=jax.ShapeDtypeStruct((depth, rows, n), F32),
        compiler_params=pltpu.CompilerParams(
            dimension_semantics=("arbitrary", "arbitrary"), vmem_limit_bytes=VMEM_LIMIT),
        name="ada",
    )(cvec, w_ada, b_ada.reshape(depth, 1, n))


def _ffn_kernel(x_ref, mod_ref, npre_ref, npost_ref, w1_ref, w2_ref, o_ref, acc_ref, *, mi):
    x = x_ref[...]
    h = _modulated(x, mod_ref, npre_ref, mi).astype(BF16)
    for k in range(D_FF // FFN_CHUNK):
        lo = k * FFN_CHUNK
        g = jnp.dot(h, w1_ref[:, lo:lo + FFN_CHUNK], preferred_element_type=F32)
        u = jnp.dot(h, w1_ref[:, D_FF + lo:D_FF + lo + FFN_CHUNK], preferred_element_type=F32)
        a = (g * jax.nn.sigmoid(g) * u).astype(BF16)
        part = jnp.dot(a, w2_ref[lo:lo + FFN_CHUNK, :], preferred_element_type=F32)
        if k == 0:
            acc_ref[...] = part
        else:
            acc_ref[...] += part
    gate = mod_ref[0, mi + 2:mi + 3, :]
    o_ref[...] = x + (0.5 * gate) * _rms(acc_ref[...], npost_ref[...])


def _ffn(x, mod, npre, npost, w1, w2, *, layer, half, seq_len):
    mi, ni = 6 * half, 2 * half
    t_total = x.shape[0]
    per_batch = mod.shape[0] > 1
    tm = _row_tile(t_total, seq_len, per_batch)
    mod_map = (lambda i: ((i * tm) // seq_len, 0, 0)) if per_batch else (lambda i: (0, 0, 0))
    return pl.pallas_call(
        functools.partial(_ffn_kernel, mi=mi),
        grid=(t_total // tm,),
        in_specs=[
            pl.BlockSpec((tm, D_MODEL), lambda i: (i, 0)),
            pl.BlockSpec((1, 9, D_MODEL), mod_map),
            _const_spec((1, D_MODEL), 1, (layer, ni)),
            _const_spec((1, D_MODEL), 1, (layer, ni)),
            _const_spec((D_MODEL, 2 * D_FF), 1, (layer, half)),
            _const_spec((D_FF, D_MODEL), 1, (layer, half)),
        ],
        out_specs=pl.BlockSpec((tm, D_MODEL), lambda i: (i, 0)),
        out_shape=jax.ShapeDtypeStruct((t_total, D_MODEL), F32),
        scratch_shapes=[pltpu.VMEM((tm, D_MODEL), F32)],
        compiler_params=pltpu.CompilerParams(
            dimension_semantics=("arbitrary",), vmem_limit_bytes=VMEM_LIMIT),
        name="ffn",
    )(x, mod, npre, npost, w1, w2)


def _proj_kernel(*refs, has_pos):
    if has_pos:
        x_ref, mod_ref, npre_ref, pos_ref, w_ref, wt_ref, o_ref, ot_ref = refs
    else:
        x_ref, mod_ref, npre_ref, w_ref, wt_ref, o_ref, ot_ref = refs
    h = _modulated(x_ref[...], mod_ref, npre_ref, 3)
    if has_pos:
        h = h + pos_ref[...]
    hb = h.astype(BF16)
    for j in range(P_W // PROJ_CHUNK):
        lo = j * PROJ_CHUNK
        o_ref[:, lo:lo + PROJ_CHUNK] = jnp.dot(hb, w_ref[:, lo:lo + PROJ_CHUNK], preferred_element_type=F32)
    ot_ref[...] = lax.dot_general(wt_ref[...], hb, (((1,), (1,)), ((), ())), preferred_element_type=F32)


def _proj(x, mod, npre, pos, w_scan, w_mif_t, *, layer, seq_len):
    t_total = x.shape[0]
    per_batch = mod.shape[0] > 1
    tm = _row_tile(t_total, seq_len, per_batch or pos is not None)
    mod_map = (lambda i: ((i * tm) // seq_len, 0, 0)) if per_batch else (lambda i: (0, 0, 0))
    in_specs = [
        pl.BlockSpec((tm, D_MODEL), lambda i: (i, 0)),
        pl.BlockSpec((1, 9, D_MODEL), mod_map),
        _const_spec((1, D_MODEL), 1, (layer, 1)),
    ]
    args = [x, mod, npre]
    if pos is not None:
        tiles_per_seq = seq_len // tm
        in_specs.append(pl.BlockSpec((tm, D_MODEL), lambda i: (i % tiles_per_seq, 0)))
        args.append(pos)
    in_specs += [_const_spec((D_MODEL, P_W), 1, (layer,)), _const_spec((N_MIF, D_MODEL), 1, (layer,))]
    args += [w_scan, w_mif_t]
    return pl.pallas_call(
        functools.partial(_proj_kernel, has_pos=pos is not None),
        grid=(t_total // tm,),
        in_specs=in_specs,
        out_specs=[pl.BlockSpec((tm, P_W), lambda i: (i, 0)),
                   pl.BlockSpec((N_MIF, tm), lambda i: (0, i))],
        out_shape=[jax.ShapeDtypeStruct((t_total, P_W), F32),
                   jax.ShapeDtypeStruct((N_MIF, t_total), F32)],
        compiler_params=pltpu.CompilerParams(
            dimension_semantics=("arbitrary",), vmem_limit_bytes=VMEM_LIMIT),
        name="proj",
    )(*args)


def _merge_kernel(*refs, has_pos):
    if has_pos:
        (x_ref, mod_ref, npre_ref, npost_ref, pos_ref, of_ref, ob_ref,
         wg_ref, hn_ref, wb_ref, wo_ref, o_ref) = refs
    else:
        (x_ref, mod_ref, npre_ref, npost_ref, of_ref, ob_ref,
         wg_ref, hn_ref, wb_ref, wo_ref, o_ref) = refs
    x = x_ref[...]
    h = _modulated(x, mod_ref, npre_ref, 3)
    if has_pos:
        h = h + pos_ref[...]
    hb = h.astype(BF16)
    merged = None
    for n in range(3):
        o = of_ref[:, n * MIX_W:(n + 1) * MIX_W] + ob_ref[:, n * MIX_W:(n + 1) * MIX_W]
        parts = []
        for hh in range(HEADS):
            oh = o[:, hh * DV:(hh + 1) * DV]
            ms = jnp.mean(oh * oh, axis=-1, keepdims=True)
            parts.append(oh * lax.rsqrt(ms + EPS))
        on = jnp.concatenate(parts, axis=1) * hn_ref[n:n + 1, :]
        gpre = jnp.dot(hb, wg_ref[:, n * MIX_W:(n + 1) * MIX_W], preferred_element_type=F32)
        sg = jax.nn.sigmoid(gpre)
        act = sg if n == 0 else gpre * sg
        ys = (act * on).astype(BF16)
        br = jnp.dot(ys, wb_ref[n], preferred_element_type=F32)
        lo = 3 * MIX_W + n * D_MODEL
        mg = jax.nn.sigmoid(jnp.dot(hb, wg_ref[:, lo:lo + D_MODEL], preferred_element_type=F32))
        merged = mg * br if merged is None else merged + mg * br
    out = jnp.dot(merged.astype(BF16), wo_ref[...], preferred_element_type=F32)
    gate = mod_ref[0, 5:6, :]
    o_ref[...] = x + gate * _rms(out, npost_ref[...])


def _merge(x, mod, npre, npost, pos, o_f, o_b, w_gate, head_norm, w_branch, w_out, *, layer, seq_len):
    t_total = x.shape[0]
    per_batch = mod.shape[0] > 1
    tm = _row_tile(t_total, seq_len, per_batch or pos is not None)
    mod_map = (lambda i: ((i * tm) // seq_len, 0, 0)) if per_batch else (lambda i: (0, 0, 0))
    in_specs = [
        pl.BlockSpec((tm, D_MODEL), lambda i: (i, 0)),
        pl.BlockSpec((1, 9, D_MODEL), mod_map),
        _const_spec((1, D_MODEL), 1, (layer, 1)),
        _const_spec((1, D_MODEL), 1, (layer, 1)),
    ]
    args = [x, mod, npre, npost]
    if pos is not None:
        tiles_per_seq = seq_len // tm
        in_specs.append(pl.BlockSpec((tm, D_MODEL), lambda i: (i % tiles_per_seq, 0)))
        args.append(pos)
    in_specs += [
        pl.BlockSpec((tm, 3 * MIX_W), lambda i: (i, 0)),
        pl.BlockSpec((tm, 3 * MIX_W), lambda i: (i, 0)),
        _const_spec((D_MODEL, GATE_W), 1, (layer,)),
        _const_spec((3, MIX_W), 1, (layer,)),
        _const_spec((3, MIX_W, D_MODEL), 1, (layer,)),
        _const_spec((D_MODEL, D_MODEL), 1, (layer,)),
    ]
    args += [o_f, o_b, w_gate, head_norm, w_branch, w_out]
    return pl.pallas_call(
        functools.partial(_merge_kernel, has_pos=pos is not None),
        grid=(t_total // tm,),
        in_specs=in_specs,
        out_specs=pl.BlockSpec((tm, D_MODEL), lambda i: (i, 0)),
        out_shape=jax.ShapeDtypeStruct((t_total, D_MODEL), F32),
        compiler_params=pltpu.CompilerParams(
            dimension_semantics=("arbitrary",), vmem_limit_bytes=VMEM_LIMIT),
        name="merge",
    )(*args)


def _scan_constants():
    t = np.arange(CHUNK)
    lower = (t[None, :] <= t[:, None])
    lm = np.stack([lower, lower.T]).astype(np.float32)
    lvl = np.zeros((N_DIR, N_LEVELS, CHUNK, CHUNK), np.float32)
    lvl[:, 0] = np.eye(CHUNK)
    for p in range(1, N_LEVELS):
        half = 1 << (p - 1)
        same = (t[:, None] >> p) == (t[None, :] >> p)
        fwd = same & ((t[:, None] & half) != 0) & ((t[None, :] & half) == 0)
        lvl[0, p] = fwd
        lvl[1, p] = fwd.T
    tri = lvl.sum(axis=1)
    assert np.array_equal(tri[0], lower) and np.array_equal(tri[1], lower.T)
    lvl = np.tile(lvl, (1, 1, 1, HEADS))
    tri = np.tile(tri, (1, 1, HEADS))
    pair_blk = np.kron(np.eye(2), np.ones((CHUNK, DK)))
    sel = np.zeros((N_DIR, N_MIF, 2 * QK_W), np.float32)
    for d in range(N_DIR):
        for h in range(HEADS):
            sel[d, d * 2 * HEADS + HEADS + h, h * DK:(h + 1) * DK] = 1.0
            sel[d, d * 2 * HEADS + h, QK_W + h * DK:QK_W + (h + 1) * DK] = 1.0
    col_sel = np.zeros((QK_W, LANE), np.float32)
    for h in range(HEADS):
        col_sel[h * DK:(h + 1) * DK, h] = 1.0
    ones3 = np.zeros((16, LANE), np.float32)
    ones3[:3] = 1.0
    return dict(
        lm=jnp.asarray(lm, dtype=BF16), lvl=jnp.asarray(lvl), tri=jnp.asarray(tri),
        sel3=jnp.asarray(np.tile(sel, (1, 3, 1)), dtype=BF16),
        hm2=jnp.asarray(pair_blk, dtype=BF16), onesbd=jnp.asarray(col_sel, dtype=BF16),
        cm=jnp.asarray(col_sel), ones3=jnp.asarray(ones3, dtype=BF16),
    )


_TN = (((0,), (0,)), ((), ()))
_NT = (((1,), (1,)), ((), ()))


def _log_sigmoid(x):
    return jnp.minimum(x, 0.0) - jnp.log(1.0 + jnp.exp(-jnp.abs(x)))


def _split3(x):
    hi = x.astype(BF16)
    r1 = x - hi.astype(F32)
    mid = r1.astype(BF16)
    lo = (r1 - mid.astype(F32)).astype(BF16)
    return hi, mid, lo


def _cumsum_rows_issue(x, lm_bf):
    return jnp.dot(lm_bf, jnp.concatenate(_split3(x), axis=1), preferred_element_type=F32)


def _cumsum_rows_finish(c):
    w = c.shape[1] // 3
    return (c[:, :w] + c[:, w:2 * w]) + c[:, 2 * w:]


def _to_column(row, ones3_ref):
    hi, mid, lo = _split3(row)
    stack = jnp.concatenate([hi.astype(F32), mid.astype(F32), lo.astype(F32),
                             jnp.zeros((13, row.shape[1]), F32)], axis=0).astype(BF16)
    return lax.dot_general(stack, ones3_ref[...], _TN, preferred_element_type=F32)


def _block_diag2(a, b):
    return jnp.concatenate([jnp.concatenate([a, jnp.zeros_like(b)], axis=1),
                            jnp.concatenate([jnp.zeros_like(a), b], axis=1)], axis=0)


def _pair_scores(xq, xk, hm2_ref):
    outs = []
    for j in range(2):
        kj = xk[:, j * LANE:(j + 1) * LANE]
        kbd = jnp.concatenate([kj, kj], axis=0) * hm2_ref[...]
        outs.append(lax.dot_general(xq[:, j * LANE:(j + 1) * LANE], kbd, _NT, preferred_element_type=F32))
    return jnp.concatenate(outs, axis=1)


def _row_bcast(ref, r):
    return jnp.broadcast_to(ref[r:r + 1, :], (8, ref.shape[1]))


def _level_operand(p, d, g2g, qgrp, kgrp, gbuf, iota8):
    blk, half = 1 << p, 1 << (p - 1)
    mid = half if d == 0 else half - 1
    out = []
    for j in range(CHUNK // 8):
        r0 = 8 * j
        if blk >= 16:
            ref = _row_bcast(gbuf, (r0 // blk) * blk + mid)
            query = ((r0 & half) != 0) == (d == 0)
            dlt = g2g[j] - ref if query else ref - g2g[j]
            w = qgrp[j] if query else kgrp[j]
        else:
            ref = _row_bcast(gbuf, r0 + mid)
            for m in range(1, 8 // blk):
                ref = jnp.where(iota8 >= m * blk, _row_bcast(gbuf, r0 + m * blk + mid), ref)
            dlt = -jnp.abs(g2g[j] - ref)
            upper = (iota8 & half) != 0
            w = jnp.where(upper, qgrp[j], kgrp[j]) if d == 0 else jnp.where(upper, kgrp[j], qgrp[j])
        out.append(w * jnp.exp2(dlt))
    return jnp.concatenate(out, axis=0).astype(BF16)


def _gated_dir(q, k, v, la, s_ref, gbuf, slot, d, cst, o_ref, o_lo):
    csum = _cumsum_rows_issue(la, cst["lm"][d])
    vb = v.astype(BF16)
    s_old = [s_ref[slot * HEADS + h] for h in range(HEADS)]
    yield
    g2 = _cumsum_rows_finish(csum) * LOG2E
    last = CHUNK - 1 if d == 0 else 0
    g2_last = g2[last:last + 1, :]
    gbuf[...] = g2
    iota8 = lax.broadcasted_iota(jnp.int32, (8, QK_W), 0)
    groups = [slice(8 * j, 8 * j + 8) for j in range(CHUNK // 8)]
    g2g, qgrp, kgrp = [g2[r] for r in groups], [q[r] for r in groups], [k[r] for r in groups]
    scores = [_pair_scores(q.astype(BF16), k.astype(BF16), cst["hm2"])]
    for p in range(1, N_LEVELS):
        x = _level_operand(p, d, g2g, qgrp, kgrp, gbuf, iota8)
        scores.append(_pair_scores(x, x, cst["hm2"]))
    qg = (q * jnp.exp2(g2)).astype(BF16)
    kg = (k * jnp.exp2(g2_last - g2)).astype(BF16)
    dec = _to_column(jnp.exp2(g2_last), cst["ones3"])
    dss = [lax.dot_general(kg[:, j * LANE:(j + 1) * LANE], vb[:, 2 * j * DV:(2 * j + 2) * DV], _TN,
                           preferred_element_type=F32) for j in range(2)]
    yield
    a = scores[0] * cst["lvl"][d, 0]
    for p in range(1, N_LEVELS):
        a = a + scores[p] * cst["lvl"][d, p]
    ab = a.astype(BF16)
    outs = []
    for j in range(2):
        rhs = jnp.concatenate([
            _block_diag2(vb[:, 2 * j * DV:(2 * j + 1) * DV], vb[:, (2 * j + 1) * DV:(2 * j + 2) * DV]),
            _block_diag2(s_old[2 * j].astype(BF16), s_old[2 * j + 1].astype(BF16)),
        ], axis=0)
        lhs = jnp.concatenate([ab[:, j * LANE:(j + 1) * LANE], qg[:, j * LANE:(j + 1) * LANE]], axis=1)
        outs.append(jnp.dot(lhs, rhs, preferred_element_type=F32))
    yield
    for j in range(2):
        for h2 in range(2):
            h = 2 * j + h2
            s_ref[slot * HEADS + h] = (dec[h * DK:(h + 1) * DK, :] * s_old[h]
                                       + dss[j][h2 * DK:(h2 + 1) * DK, h2 * DV:(h2 + 1) * DV])
    o_ref[:, o_lo:o_lo + MIX_W] = jnp.concatenate(outs, axis=1)


def _mlstm_dir(q, k, v, g16, c_ref, n_ref, m_ref, slot, d, cst, o_ref):
    neg_inf = float("-inf")
    pre = g16 + cst["bias16"][...]
    cum3 = jnp.dot(jnp.concatenate(_split3(_log_sigmoid(pre)), axis=0), cst["lm"][1 - d],
                   preferred_element_type=F32)
    qk = _pair_scores(q.astype(BF16), k.astype(BF16), cst["hm2"])
    vb = v.astype(BF16)
    c_old = [c_ref[slot * HEADS + h] for h in range(HEADS)]
    n_old = n_ref[slot]
    yield
    cum = (cum3[0:16] + cum3[16:32]) + cum3[32:48]
    is_f = (lax.broadcasted_iota(jnp.int32, (N_MIF, CHUNK), 0) & HEADS) != 0
    y = jnp.where(is_f, cum, pre)
    be = lax.dot_general(jnp.concatenate(_split3(y), axis=0), cst["sel3"][d], _TN,
                         preferred_element_type=F32)
    yield
    b_exp, li_exp = be[:, :QK_W], be[:, QK_W:]
    r0 = d * 2 * HEADS
    li_row = jnp.concatenate([y[r0 + h:r0 + h + 1, :] for h in range(HEADS)], axis=1)
    b_row = jnp.concatenate([y[r0 + HEADS + h:r0 + HEADS + h + 1, :] for h in range(HEADS)], axis=1)
    m_prev = m_ref[slot:slot + 1, :]
    inter = b_exp + m_prev
    dmat = jnp.where(cst["tri"][d] > 0.0, b_exp - b_row + li_row, neg_inf)
    lane_head = lax.broadcasted_iota(jnp.int32, (CHUNK, QK_W), 1) // DK
    mt = inter
    for h in range(HEADS):
        sel = lane_head == h
        rm = jnp.max(jnp.where(sel, dmat, neg_inf), axis=-1, keepdims=True)
        mt = jnp.where(sel, jnp.maximum(inter, rm), mt)
    sc = (qk * jnp.exp(dmat - mt)).astype(BF16)
    qe = (q * jnp.exp(inter - mt)).astype(BF16)
    last = CHUNK - 1 if d == 0 else 0
    b_last = b_exp[last:last + 1, :]
    lw = b_last - b_exp + li_exp
    m_new = jnp.maximum(b_last + m_prev, jnp.max(lw, axis=0, keepdims=True))
    kw = (k * jnp.exp(lw - m_new)).astype(BF16)
    dec_col = _to_column(jnp.exp(b_last + m_prev - m_new), cst["ones3"])
    ones_blk = jnp.ones((CHUNK, LANE), BF16)
    nds, dcs = [], []
    for j in range(2):
        rows = slice(j * LANE, (j + 1) * LANE)
        v_pair = vb[:, 2 * j * DV:(2 * j + 2) * DV]
        top = jnp.concatenate([_block_diag2(v_pair[:, :DV], v_pair[:, DV:]), cst["onesbd"][rows, :]], axis=1)
        bot = jnp.concatenate([_block_diag2(c_old[2 * j].astype(BF16), c_old[2 * j + 1].astype(BF16)),
                               n_old[rows, :].astype(BF16)], axis=1)
        lhs = jnp.concatenate([sc[:, rows], qe[:, rows]], axis=1)
        nds.append(jnp.dot(lhs, jnp.concatenate([top, bot], axis=0), preferred_element_type=F32))
        dcs.append(lax.dot_general(kw[:, rows], jnp.concatenate([v_pair, ones_blk], axis=1), _TN,
                                   preferred_element_type=F32))
    yield
    outs = []
    for j in range(2):
        rows = slice(j * LANE, (j + 1) * LANE)
        nd, dc = nds[j], dcs[j]
        for h2 in range(2):
            h = 2 * j + h2
            den = nd[:, 2 * DV + h:2 * DV + h + 1]
            mth = mt[:, h * DK:h * DK + 1]
            outs.append(nd[:, h2 * DV:(h2 + 1) * DV] / jnp.maximum(jnp.abs(den), jnp.exp(-mth)))
            c_ref[slot * HEADS + h] = (dec_col[h * DK:(h + 1) * DK, :] * c_old[h]
                                       + dc[h2 * DK:(h2 + 1) * DK, h2 * DV:(h2 + 1) * DV])
        n_ref[slot, rows, :] = dec_col[rows, :] * n_old[rows, :] + cst["cm"][rows, :] * dc[:, 2 * DV:]
    m_ref[slot:slot + 1, :] = m_new
    o_ref[:, 0:MIX_W] = jnp.concatenate(outs, axis=1)


_DONE = object()
_PER_LAYER = ("bias16", "wup3", "glab")
_SCAN_CONSTS = ("lm", "lvl", "tri", "sel3", "hm2", "onesbd", "cm", "ones3", "bias16", "wup3", "glab", "gamma")


def _scan_kernel(*refs, layer, has_init, emit_state):
    pf_ref, pb_ref, gf_ref, gb_ref = refs[:4]
    pos = 4
    cst = dict(zip(_SCAN_CONSTS, refs[pos:pos + len(_SCAN_CONSTS)]))
    pos += len(_SCAN_CONSTS)
    if has_init:
        c0_ref, n0_ref, m0_ref, g0_ref, h0_ref = refs[pos:pos + 5]
        pos += 5
    of_ref, ob_ref = refs[pos:pos + 2]
    pos += 2
    if emit_state:
        cout_ref, nout_ref, mout_ref, gout_ref, hout_ref = refs[pos:pos + 5]
        pos += 5
    c_scr, n_scr, m_scr, g_scr, h_scr, gbuf = refs[pos:pos + 6]

    ci = pl.program_id(1)

    @pl.when(ci == 0)
    def _():
        if has_init:
            for q in range(SEQ_PER_STEP):
                lo, hi = q * N_DIR * HEADS, (q + 1) * N_DIR * HEADS
                c_scr[lo:hi] = c0_ref[q]
                g_scr[lo:hi] = g0_ref[q]
                h_scr[lo:hi] = h0_ref[q]
                for d in range(N_DIR):
                    n_scr[q * N_DIR + d] = cst["cm"][...] * n0_ref[q, d]
                m_scr[q * N_DIR:(q + 1) * N_DIR, :] = m0_ref[q]
        else:
            c_scr[...] = jnp.zeros(c_scr.shape, F32)
            g_scr[...] = jnp.zeros(g_scr.shape, F32)
            h_scr[...] = jnp.zeros(h_scr.shape, F32)
            n_scr[...] = jnp.zeros(n_scr.shape, F32)
            m_scr[...] = jnp.zeros(m_scr.shape, F32)

    gam = cst["gamma"][...]
    ge = jnp.exp(gam - jnp.max(gam, axis=0, keepdims=True))
    pg = ge / jnp.sum(ge, axis=0, keepdims=True)
    cs = pg[0:1, :]
    for j in range(1, layer + 1):
        cs = cs + pg[j:j + 1, :]
    lb = cs - pg[0:1, :]

    units = []
    for q in range(SEQ_PER_STEP):
        for d, (p4_ref, g4_ref, o4_ref) in enumerate(((pf_ref, gf_ref, of_ref), (pb_ref, gb_ref, ob_ref))):
            slot = q * N_DIR + d
            p_ref, g_ref, o_ref = p4_ref.at[q, 0], g4_ref.at[q], o4_ref.at[q, 0]
            units.append(_mlstm_dir(
                p_ref[:, OFF_MQ:OFF_MQ + QK_W], p_ref[:, OFF_MK:OFF_MK + QK_W], p_ref[:, OFF_MV:OFF_MV + MIX_W],
                g_ref[0], c_scr, n_scr, m_scr, slot, d, cst, o_ref))
            sm = p_ref[:, OFF_SM:OFF_SM + LANE]
            sm_hi = sm.astype(BF16)
            sm_lo = (sm - sm_hi.astype(F32)).astype(BF16)
            pre = jnp.dot(jnp.concatenate([sm_hi, sm_lo, sm_hi], axis=1), cst["wup3"][d],
                          preferred_element_type=F32) + cst["glab"][d:d + 1, :]
            units.append(_gated_dir(
                p_ref[:, OFF_GQ:OFF_GQ + QK_W], p_ref[:, OFF_GK:OFF_GK + QK_W], p_ref[:, OFF_GV:OFF_GV + MIX_W],
                _log_sigmoid(pre) * (1.0 / G_TEMP), g_scr, gbuf.at[2 * slot], slot, d, cst, o_ref, MIX_W))
            zz = p_ref[:, OFF_HF + d * QK_W:OFF_HF + (d + 1) * QK_W]
            ez = jnp.exp(-jnp.abs(zz))
            rz = 1.0 / (1.0 + ez)
            pos_z = zz >= 0.0
            sig = jnp.where(pos_z, rz, ez * rz)
            nsig = jnp.where(pos_z, ez * rz, rz)
            hv = p_ref[:, OFF_HV:OFF_HV + MIX_W]
            units.append(_gated_dir(
                p_ref[:, OFF_HQ:OFF_HQ + QK_W], (1.0 - lb) * nsig, hv * jax.nn.sigmoid(hv),
                jnp.log(lb + (1.0 - lb) * sig), h_scr, gbuf.at[2 * slot + 1], slot, d, cst, o_ref, 2 * MIX_W))
    while units:
        units = [u for u in units if next(u, _DONE) is not _DONE]

    if emit_state:
        @pl.when(ci == pl.num_programs(1) - 1)
        def _():
            for q in range(SEQ_PER_STEP):
                lo, hi = q * N_DIR * HEADS, (q + 1) * N_DIR * HEADS
                cout_ref[q] = c_scr[lo:hi]
                gout_ref[q] = g_scr[lo:hi]
                hout_ref[q] = h_scr[lo:hi]
                for d in range(N_DIR):
                    nout_ref[q, d] = jnp.sum(n_scr[q * N_DIR + d], axis=-1, keepdims=True)
                mout_ref[q] = m_scr[q * N_DIR:(q + 1) * N_DIR, :]


def _scan(p, g_rows, cst, init, *, layer, nbatch, seq_len, emit_state):
    t_total = p.shape[0]
    nc = seq_len // CHUNK
    assert nbatch % SEQ_PER_STEP == 0
    has_init = init is not None

    fwd = lambda b, c: (b, c, 0, 0)
    bwd = lambda b, c: (b, nc - 1 - c, 0, 0)
    p4 = p.reshape(nbatch, nc, CHUNK, P_W)
    g4 = g_rows.reshape(nbatch, nc, N_MIF, CHUNK)
    const_args = [cst[name] for name in _SCAN_CONSTS]
    in_specs = [
        pl.BlockSpec((SEQ_PER_STEP, 1, CHUNK, P_W), fwd),
        pl.BlockSpec((SEQ_PER_STEP, 1, CHUNK, P_W), bwd),
        pl.BlockSpec((SEQ_PER_STEP, 1, N_MIF, CHUNK), fwd),
        pl.BlockSpec((SEQ_PER_STEP, 1, N_MIF, CHUNK), bwd),
    ] + [_const_spec(a.shape[1:], 2, (layer,)) if name in _PER_LAYER else _const_spec(a.shape, 2)
         for name, a in zip(_SCAN_CONSTS, const_args)]
    args = [p4, p4, g4, g4] + const_args
    state_blk = (SEQ_PER_STEP, N_DIR * HEADS, DK, DV)
    state_map = lambda b, c: (b, 0, 0, 0)
    state_specs = [
        pl.BlockSpec(state_blk, state_map),
        pl.BlockSpec((SEQ_PER_STEP, N_DIR, QK_W, 1), state_map),
        pl.BlockSpec((SEQ_PER_STEP, N_DIR, QK_W), lambda b, c: (b, 0, 0)),
        pl.BlockSpec(state_blk, state_map),
        pl.BlockSpec(state_blk, state_map),
    ]
    if has_init:
        in_specs += state_specs
        args += list(init)
    out_specs = [pl.BlockSpec((SEQ_PER_STEP, 1, CHUNK, 3 * MIX_W), fwd),
                 pl.BlockSpec((SEQ_PER_STEP, 1, CHUNK, 3 * MIX_W), bwd)]
    out_shape = [jax.ShapeDtypeStruct((nbatch, nc, CHUNK, 3 * MIX_W), F32)] * 2
    if emit_state:
        out_specs += state_specs
        out_shape += [
            jax.ShapeDtypeStruct((nbatch, N_DIR * HEADS, DK, DV), F32),
            jax.ShapeDtypeStruct((nbatch, N_DIR, QK_W, 1), F32),
            jax.ShapeDtypeStruct((nbatch, N_DIR, QK_W), F32),
            jax.ShapeDtypeStruct((nbatch, N_DIR * HEADS, DK, DV), F32),
            jax.ShapeDtypeStruct((nbatch, N_DIR * HEADS, DK, DV), F32),
        ]
    n_slots = SEQ_PER_STEP * N_DIR
    res = pl.pallas_call(
        functools.partial(_scan_kernel, layer=layer, has_init=has_init, emit_state=emit_state),
        grid=(nbatch // SEQ_PER_STEP, nc),
        in_specs=in_specs,
        out_specs=out_specs,
        out_shape=out_shape,
        scratch_shapes=[
            pltpu.VMEM((n_slots * HEADS, DK, DV), F32),
            pltpu.VMEM((n_slots, QK_W, LANE), F32),
            pltpu.VMEM((8, QK_W), F32),
            pltpu.VMEM((n_slots * HEADS, DK, DV), F32),
            pltpu.VMEM((n_slots * HEADS, DK, DV), F32),
            pltpu.VMEM((2 * n_slots, CHUNK, QK_W), F32),
        ],
        compiler_params=pltpu.CompilerParams(
            dimension_semantics=("arbitrary", "arbitrary"), vmem_limit_bytes=VMEM_LIMIT),
        name="scan",
    )(*args)
    return [res[0].reshape(t_total, 3 * MIX_W), res[1].reshape(t_total, 3 * MIX_W)] + list(res[2:])


def _grid_position(n_tokens):
    rows = n_tokens // GRID_W
    quarter = D_MODEL // 4
    freqs = jnp.exp(-math.log(10000.0) * jnp.arange(quarter, dtype=F32) / quarter)
    r = jnp.arange(rows, dtype=F32)[:, None] * freqs
    cl = jnp.arange(GRID_W, dtype=F32)[:, None] * freqs
    r_emb = jnp.concatenate([jnp.sin(r), jnp.cos(r)], axis=-1)
    c_emb = jnp.concatenate([jnp.sin(cl), jnp.cos(cl)], axis=-1)
    emb = jnp.concatenate([jnp.broadcast_to(r_emb[:, None], (rows, GRID_W, D_MODEL // 2)),
                           jnp.broadcast_to(c_emb[None], (rows, GRID_W, D_MODEL // 2))], axis=-1)
    return emb.reshape(rows * GRID_W, D_MODEL)


def _prepare_weights(w_in, gate_bias, gla_w_up):
    depth = w_in.shape[0]
    offs = np.concatenate([[0], np.cumsum(IN_SIZES)])
    cols = [w_in[:, :, int(offs[i]):int(offs[i + 1])] for i in range(len(IN_SIZES))]
    (mq, mk, mv, mo, mif, gq, gk, gv, gr, glr, hq, hf, hv, hg, mg) = cols
    pad = jnp.zeros((depth, D_MODEL, LANE - N_MIF - N_GLR), F32)
    w_scan = jnp.concatenate([mq * (DK ** -0.5), mk, mv, gq * (DK ** -0.5), gk, gv, hq, hf, hv, mif, glr, pad],
                             axis=2).astype(BF16)
    w_mif_t = jnp.swapaxes(mif, 1, 2).astype(BF16)
    w_gate = jnp.concatenate([mo, gr, hg, mg], axis=2).astype(BF16)
    bias16 = jnp.broadcast_to(gate_bias.reshape(depth, N_MIF, 1), (depth, N_MIF, CHUNK))
    wup = jnp.zeros((depth, N_DIR, LANE, QK_W), F32)
    for d in range(N_DIR):
        wup = wup.at[:, d, N_MIF + d * G_RANK:N_MIF + (d + 1) * G_RANK, :].set(gla_w_up[:, d])
    wup_hi = wup.astype(BF16)
    wup_lo = (wup - wup_hi.astype(F32)).astype(BF16)
    wup3 = jnp.concatenate([wup_hi, wup_hi, wup_lo], axis=2)
    return w_scan, w_mif_t, w_gate, bias16, wup3


def _gate_rows(g_t, t_total):
    return g_t.reshape(N_MIF, t_total // CHUNK, CHUNK).transpose(1, 0, 2)


def _trunk_layer(x, mod, pos, init, wts, cst, *, layer, nbatch, seq_len, emit_state):
    x = _ffn(x, mod, wts["npre"], wts["npost"], wts["w1"], wts["w2"], layer=layer, half=0, seq_len=seq_len)
    p, g_t = _proj(x, mod, wts["npre"], pos, wts["w_scan"], wts["w_mif_t"], layer=layer, seq_len=seq_len)
    res = _scan(p, _gate_rows(g_t, x.shape[0]), cst, init,
                layer=layer, nbatch=nbatch, seq_len=seq_len, emit_state=emit_state)
    x = _merge(x, mod, wts["npre"], wts["npost"], pos, res[0], res[1], wts["w_gate"], wts["head_norm"],
               wts["w_branch"], wts["w_out"], layer=layer, seq_len=seq_len)
    x = _ffn(x, mod, wts["npre"], wts["npost"], wts["w1"], wts["w2"], layer=layer, half=1, seq_len=seq_len)
    return x, res[2:]


def kernel(x_prompt, x_sample, c, state_mlstm_C, state_mlstm_n, state_mlstm_m, state_gla_S, state_hgrn_S,
           c_ctx, w_ada, b_ada, norm_pre, norm_post, w_ffn_in, w_ffn_out, w_in, mlstm_gate_bias,
           gla_w_up, gla_b, hgrn_gamma, head_norm, w_branch, w_out):
    bp, tp, _ = x_prompt.shape
    bs, ts, _ = x_sample.shape
    depth = w_in.shape[0]

    n_c = 1 + bs
    rows = -(-n_c // 8) * 8
    cvec = jnp.concatenate([c_ctx[None, :], c, jnp.zeros((rows - n_c, D_MODEL), F32)], axis=0)
    mod_all = _ada(cvec, w_ada, b_ada)[:, :n_c].reshape(depth, n_c, 9, D_MODEL)

    w_scan, w_mif_t, w_gate, bias16, wup3 = _prepare_weights(w_in, mlstm_gate_bias, gla_w_up)
    wts = dict(
        npre=norm_pre.reshape(depth, 3, 1, D_MODEL), npost=norm_post.reshape(depth, 3, 1, D_MODEL),
        w1=w_ffn_in.astype(BF16), w2=w_ffn_out.astype(BF16),
        w_scan=w_scan, w_mif_t=w_mif_t, w_gate=w_gate,
        head_norm=head_norm, w_branch=w_branch.astype(BF16), w_out=w_out.astype(BF16),
    )
    cst = dict(_scan_constants(), bias16=bias16, wup3=wup3, glab=gla_b, gamma=hgrn_gamma)

    pos = _grid_position(ts).astype(F32)
    xp = x_prompt.reshape(bp * tp, D_MODEL)
    xs = x_sample.reshape(bs * ts, D_MODEL)
    new_states = []
    for l in range(depth):
        xp, st = _trunk_layer(xp, mod_all[l, 0:1], None, None, wts, cst,
                              layer=l, nbatch=bp, seq_len=tp, emit_state=True)
        new_states.append(st)
        init = (
            state_mlstm_C[:, l].reshape(bs, N_DIR * HEADS, DK, DV),
            state_mlstm_n[:, l].reshape(bs, N_DIR, QK_W, 1),
            jnp.repeat(state_mlstm_m[:, l], DK, axis=-1),
            state_gla_S[:, l].reshape(bs, N_DIR * HEADS, DK, DV),
            state_hgrn_S[:, l].reshape(bs, N_DIR * HEADS, DK, DV),
        )
        xs, _ = _trunk_layer(xs, mod_all[l, 1:], pos, init, wts, cst,
                             layer=l, nbatch=bs, seq_len=ts, emit_state=False)

    blk = (bp, N_DIR, HEADS, DK, DV)
    new_c = jnp.stack([st[0].reshape(blk) for st in new_states], axis=1)
    new_n = jnp.stack([st[1].reshape(bp, N_DIR, HEADS, DK) for st in new_states], axis=1)
    new_m = jnp.stack([st[2][:, :, ::DK] for st in new_states], axis=1)
    new_g = jnp.stack([st[3].reshape(blk) for st in new_states], axis=1)
    new_h = jnp.stack([st[4].reshape(blk) for st in new_states], axis=1)
    return (xp.reshape(bp, tp, D_MODEL), xs.reshape(bs, ts, D_MODEL), new_c, new_n, new_m, new_g, new_h)
```

```python
import functools
import math

import numpy as np
import jax
import jax.numpy as jnp
from jax import lax
from jax.experimental import pallas as pl
from jax.experimental.pallas import tpu as pltpu

F32 = jnp.float32
BF16 = jnp.bfloat16
HIGHEST = lax.Precision.HIGHEST

D_MODEL = 1024
D_FF = 2816
GRID_W = 64
CHUNK = 64
EPS = 1e-6
N_DIR = 2
HEADS = 4
DK = 64
DV = 128
QK_W = HEADS * DK
MIX_W = HEADS * DV
G_RANK = 16
G_TEMP = 16.0
N_MIF = N_DIR * 2 * HEADS
N_GLR = N_DIR * G_RANK
LANE = 128
N_LEVELS = 7
LOG2E = 1.4426950408889634
SEQ_PER_STEP = 4

OFF_MQ, OFF_MK, OFF_MV = 0, 256, 512
OFF_GQ, OFF_GK, OFF_GV = 1024, 1280, 1536
OFF_HQ, OFF_HF, OFF_HV = 2048, 2304, 2816
OFF_SM = 3328
P_W = OFF_SM + LANE
GATE_W = 3 * MIX_W + 3 * D_MODEL

IN_SIZES = (256, 256, 512, 512, N_MIF, 256, 256, 512, 512, N_GLR, 256, 512, 512, 512, 3 * D_MODEL)

FFN_CHUNK = 256
PROJ_CHUNK = 1152
VMEM_LIMIT = 56 * 1024 * 1024


def _const_spec(shape, grid_rank, lead=()):
    index = tuple(lead) + (0,) * len(shape)
    block = (None,) * len(lead) + tuple(shape)
    if grid_rank == 1:
        imap = lambda i: index
    else:
        imap = lambda i, j: index
    return pl.BlockSpec(block, imap, pipeline_mode=pl.Buffered(1))


def _rms(x, w):
    ms = jnp.mean(x * x, axis=-1, keepdims=True)
    return x * lax.rsqrt(ms + EPS) * w


def _modulated(x, mod_ref, npre_ref, mi):
    shift = mod_ref[0, mi:mi + 1, :]
    scale = mod_ref[0, mi + 1:mi + 2, :]
    return _rms(x, npre_ref[...]) * (1.0 + scale) + shift


def _row_tile(t_total, seq_len, per_batch_mod):
    for tm in (512, 256, 128, 64):
        if t_total % tm == 0 and (not per_batch_mod or seq_len % tm == 0):
            return tm
    raise ValueError("token count must be a multiple of 64")


def _ada_kernel(c_ref, w_ref, b_ref, o_ref):
    cv = c_ref[...]
    s = cv * jax.nn.sigmoid(cv)
    o_ref[0] = jnp.dot(s, w_ref[0], precision=HIGHEST, preferred_element_type=F32) + b_ref[0]


def _ada(cvec, w_ada, b_ada):
    depth, _, n = w_ada.shape
    tn = 1536
    rows = cvec.shape[0]
    return pl.pallas_call(
        _ada_kernel,
        grid=(depth, n // tn),
        in_specs=[
            pl.BlockSpec((rows, D_MODEL), lambda l, j: (0, 0)),
            pl.BlockSpec((1, D_MODEL, tn), lambda l, j: (l, 0, j)),
            pl.BlockSpec((1, 1, tn), lambda l, j: (l, 0, j)),
        ],
        out_specs=pl.BlockSpec((1, rows, tn), lambda l, j: (l, 0, j)),
        out_shape=jax.ShapeDtypeStruct((depth, rows, n), F32),
        compiler_params=pltpu.CompilerParams(
            dimension_semantics=("arbitrary", "arbitrary"), vmem_limit_bytes=VMEM_LIMIT),
        name="ada",
    )(cvec, w_ada, b_ada.reshape(depth, 1, n))


def _ffn_kernel(x_ref, mod_ref, npre_ref, npost_ref, w1_ref, w2_ref, o_ref, acc_ref, *, mi):
    x = x_ref[...]
    h = _modulated(x, mod_ref, npre_ref, mi).astype(BF16)
    for k in range(D_FF // FFN_CHUNK):
        lo = k * FFN_CHUNK
        g = jnp.dot(h, w1_ref[:, lo:lo + FFN_CHUNK], preferred_element_type=F32)
        u = jnp.dot(h, w1_ref[:, D_FF + lo:D_FF + lo + FFN_CHUNK], preferred_element_type=F32)
        a = (g * jax.nn.sigmoid(g) * u).astype(BF16)
        part = jnp.dot(a, w2_ref[lo:lo + FFN_CHUNK, :], preferred_element_type=F32)
        if k == 0:
            acc_ref[...] = part
        else:
            acc_ref[...] += part
    gate = mod_ref[0, mi + 2:mi + 3, :]
    o_ref[...] = x + (0.5 * gate) * _rms(acc_ref[...], npost_ref[...])


def _ffn(x, mod, npre, npost, w1, w2, *, layer, half, seq_len):
    mi, ni = 6 * half, 2 * half
    t_total = x.shape[0]
    per_batch = mod.shape[0] > 1
    tm = _row_tile(t_total, seq_len, per_batch)
    mod_map = (lambda i: ((i * tm) // seq_len, 0, 0)) if per_batch else (lambda i: (0, 0, 0))
    return pl.pallas_call(
        functools.partial(_ffn_kernel, mi=mi),
        grid=(t_total // tm,),
        in_specs=[
            pl.BlockSpec((tm, D_MODEL), lambda i: (i, 0)),
            pl.BlockSpec((1, 9, D_MODEL), mod_map),
            _const_spec((1, D_MODEL), 1, (layer, ni)),
            _const_spec((1, D_MODEL), 1, (layer, ni)),
            _const_spec((D_MODEL, 2 * D_FF), 1, (layer, half)),
            _const_spec((D_FF, D_MODEL), 1, (layer, half)),
        ],
        out_specs=pl.BlockSpec((tm, D_MODEL), lambda i: (i, 0)),
        out_shape=jax.ShapeDtypeStruct((t_total, D_MODEL), F32),
        scratch_shapes=[pltpu.VMEM((tm, D_MODEL), F32)],
        compiler_params=pltpu.CompilerParams(
            dimension_semantics=("arbitrary",), vmem_limit_bytes=VMEM_LIMIT),
        name="ffn",
    )(x, mod, npre, npost, w1, w2)


def _proj_kernel(*refs, has_pos):
    if has_pos:
        x_ref, mod_ref, npre_ref, pos_ref, w_ref, wt_ref, o_ref, ot_ref = refs
    else:
        x_ref, mod_ref, npre_ref, w_ref, wt_ref, o_ref, ot_ref = refs
    h = _modulated(x_ref[...], mod_ref, npre_ref, 3)
    if has_pos:
        h = h + pos_ref[...]
    hb = h.astype(BF16)
    for j in range(P_W // PROJ_CHUNK):
        lo = j * PROJ_CHUNK
        o_ref[:, lo:lo + PROJ_CHUNK] = jnp.dot(hb, w_ref[:, lo:lo + PROJ_CHUNK], preferred_element_type=F32)
    ot_ref[...] = lax.dot_general(wt_ref[...], hb, (((1,), (1,)), ((), ())), preferred_element_type=F32)


def _proj(x, mod, npre, pos, w_scan, w_mif_t, *, layer, seq_len):
    t_total = x.shape[0]
    per_batch = mod.shape[0] > 1
    tm = _row_tile(t_total, seq_len, per_batch or pos is not None)
    mod_map = (lambda i: ((i * tm) // seq_len, 0, 0)) if per_batch else (lambda i: (0, 0, 0))
    in_specs = [
        pl.BlockSpec((tm, D_MODEL), lambda i: (i, 0)),
        pl.BlockSpec((1, 9, D_MODEL), mod_map),
        _const_spec((1, D_MODEL), 1, (layer, 1)),
    ]
    args = [x, mod, npre]
    if pos is not None:
        tiles_per_seq = seq_len // tm
        in_specs.append(pl.BlockSpec((tm, D_MODEL), lambda i: (i % tiles_per_seq, 0)))
        args.append(pos)
    in_specs += [_const_spec((D_MODEL, P_W), 1, (layer,)), _const_spec((N_MIF, D_MODEL), 1, (layer,))]
    args += [w_scan, w_mif_t]
    return pl.pallas_call(
        functools.partial(_proj_kernel, has_pos=pos is not None),
        grid=(t_total // tm,),
        in_specs=in_specs,
        out_specs=[pl.BlockSpec((tm, P_W), lambda i: (i, 0)),
                   pl.BlockSpec((N_MIF, tm), lambda i: (0, i))],
        out_shape=[jax.ShapeDtypeStruct((t_total, P_W), F32),
                   jax.ShapeDtypeStruct((N_MIF, t_total), F32)],
        compiler_params=pltpu.CompilerParams(
            dimension_semantics=("arbitrary",), vmem_limit_bytes=VMEM_LIMIT),
        name="proj",
    )(*args)


def _merge_kernel(*refs, has_pos):
    if has_pos:
        (x_ref, mod_ref, npre_ref, npost_ref, pos_ref, of_ref, ob_ref,
         wg_ref, hn_ref, wb_ref, wo_ref, o_ref) = refs
    else:
        (x_ref, mod_ref, npre_ref, npost_ref, of_ref, ob_ref,
         wg_ref, hn_ref, wb_ref, wo_ref, o_ref) = refs
    x = x_ref[...]
    h = _modulated(x, mod_ref, npre_ref, 3)
    if has_pos:
        h = h + pos_ref[...]
    hb = h.astype(BF16)
    gpre = [jnp.dot(hb, wg_ref[:, n * MIX_W:(n + 1) * MIX_W], preferred_element_type=F32) for n in range(3)]
    mgpre = [jnp.dot(hb, wg_ref[:, 3 * MIX_W + n * D_MODEL:3 * MIX_W + (n + 1) * D_MODEL],
                     preferred_element_type=F32) for n in range(3)]
    ons = []
    for n in range(3):
        o = of_ref[:, n * MIX_W:(n + 1) * MIX_W] + ob_ref[:, n * MIX_W:(n + 1) * MIX_W]
        parts = []
        for hh in range(HEADS):
            oh = o[:, hh * DV:(hh + 1) * DV]
            ms = jnp.mean(oh * oh, axis=-1, keepdims=True)
            parts.append(oh * lax.rsqrt(ms + EPS))
        ons.append(jnp.concatenate(parts, axis=1) * hn_ref[n:n + 1, :])
    brs = []
    for n in range(3):
        sg = jax.nn.sigmoid(gpre[n])
        act = sg if n == 0 else gpre[n] * sg
        brs.append(jnp.dot((act * ons[n]).astype(BF16), wb_ref[n], preferred_element_type=F32))
    merged = None
    for n in range(3):
        term = jax.nn.sigmoid(mgpre[n]) * brs[n]
        merged = term if merged is None else merged + term
    out = jnp.dot(merged.astype(BF16), wo_ref[...], preferred_element_type=F32)
    gate = mod_ref[0, 5:6, :]
    o_ref[...] = x + gate * _rms(out, npost_ref[...])


def _merge(x, mod, npre, npost, pos, o_f, o_b, w_gate, head_norm, w_branch, w_out, *, layer, seq_len):
    t_total = x.shape[0]
    per_batch = mod.shape[0] > 1
    tm = _row_tile(t_total, seq_len, per_batch or pos is not None)
    mod_map = (lambda i: ((i * tm) // seq_len, 0, 0)) if per_batch else (lambda i: (0, 0, 0))
    in_specs = [
        pl.BlockSpec((tm, D_MODEL), lambda i: (i, 0)),
        pl.BlockSpec((1, 9, D_MODEL), mod_map),
        _const_spec((1, D_MODEL), 1, (layer, 1)),
        _const_spec((1, D_MODEL), 1, (layer, 1)),
    ]
    args = [x, mod, npre, npost]
    if pos is not None:
        tiles_per_seq = seq_len // tm
        in_specs.append(pl.BlockSpec((tm, D_MODEL), lambda i: (i % tiles_per_seq, 0)))
        args.append(pos)
    in_specs += [
        pl.BlockSpec((tm, 3 * MIX_W), lambda i: (i, 0)),
        pl.BlockSpec((tm, 3 * MIX_W), lambda i: (i, 0)),
        _const_spec((D_MODEL, GATE_W), 1, (layer,)),
        _const_spec((3, MIX_W), 1, (layer,)),
        _const_spec((3, MIX_W, D_MODEL), 1, (layer,)),
        _const_spec((D_MODEL, D_MODEL), 1, (layer,)),
    ]
    args += [o_f, o_b, w_gate, head_norm, w_branch, w_out]
    return pl.pallas_call(
        functools.partial(_merge_kernel, has_pos=pos is not None),
        grid=(t_total // tm,),
        in_specs=in_specs,
        out_specs=pl.BlockSpec((tm, D_MODEL), lambda i: (i, 0)),
        out_shape=jax.ShapeDtypeStruct((t_total, D_MODEL), F32),
        compiler_params=pltpu.CompilerParams(
            dimension_semantics=("arbitrary",), vmem_limit_bytes=VMEM_LIMIT),
        name="merge",
    )(*args)


def _scan_constants():
    t = np.arange(CHUNK)
    lower = (t[None, :] <= t[:, None])
    lm = np.stack([lower, lower.T]).astype(np.float32)
    lvl = np.zeros((N_DIR, N_LEVELS, CHUNK, CHUNK), np.float32)
    lvl[:, 0] = np.eye(CHUNK)
    for p in range(1, N_LEVELS):
        half = 1 << (p - 1)
        same = (t[:, None] >> p) == (t[None, :] >> p)
        fwd = same & ((t[:, None] & half) != 0) & ((t[None, :] & half) == 0)
        lvl[0, p] = fwd
        lvl[1, p] = fwd.T
    tri = lvl.sum(axis=1)
    assert np.array_equal(tri[0], lower) and np.array_equal(tri[1], lower.T)
    lvl = np.tile(lvl, (1, 1, 1, HEADS))
    tri = np.tile(tri, (1, 1, HEADS))
    pair_blk = np.kron(np.eye(2), np.ones((CHUNK, DK)))
    sel = np.zeros((N_DIR, N_MIF, 2 * QK_W), np.float32)
    for d in range(N_DIR):
        for h in range(HEADS):
            sel[d, d * 2 * HEADS + HEADS + h, h * DK:(h + 1) * DK] = 1.0
            sel[d, d * 2 * HEADS + h, QK_W + h * DK:QK_W + (h + 1) * DK] = 1.0
    col_sel = np.zeros((QK_W, LANE), np.float32)
    for h in range(HEADS):
        col_sel[h * DK:(h + 1) * DK, h] = 1.0
    ones3 = np.zeros((16, LANE), np.float32)
    ones3[:3] = 1.0
    return dict(
        lm=jnp.asarray(lm, dtype=BF16), lvl=jnp.asarray(lvl), tri=jnp.asarray(tri),
        sel3=jnp.asarray(np.tile(sel, (1, 3, 1)), dtype=BF16),
        hm2=jnp.asarray(pair_blk, dtype=BF16), onesbd=jnp.asarray(col_sel, dtype=BF16),
        cm=jnp.asarray(col_sel), ones3=jnp.asarray(ones3, dtype=BF16),
    )


_TN = (((0,), (0,)), ((), ()))
_NT = (((1,), (1,)), ((), ()))


def _log_sigmoid(x):
    return jnp.minimum(x, 0.0) - jnp.log(1.0 + jnp.exp(-jnp.abs(x)))


def _split3(x):
    hi = x.astype(BF16)
    r1 = x - hi.astype(F32)
    mid = r1.astype(BF16)
    lo = (r1 - mid.astype(F32)).astype(BF16)
    return hi, mid, lo


def _cumsum_rows_issue(x, lm_bf):
    return jnp.dot(lm_bf, jnp.concatenate(_split3(x), axis=1), preferred_element_type=F32)


def _cumsum_rows_finish(c):
    w = c.shape[1] // 3
    return (c[:, :w] + c[:, w:2 * w]) + c[:, 2 * w:]


def _to_column(row, ones3_ref):
    hi, mid, lo = _split3(row)
    stack = jnp.concatenate([hi.astype(F32), mid.astype(F32), lo.astype(F32),
                             jnp.zeros((13, row.shape[1]), F32)], axis=0).astype(BF16)
    return lax.dot_general(stack, ones3_ref[...], _TN, preferred_element_type=F32)


def _block_diag2(a, b):
    return jnp.concatenate([jnp.concatenate([a, jnp.zeros_like(b)], axis=1),
                            jnp.concatenate([jnp.zeros_like(a), b], axis=1)], axis=0)


def _pair_scores(xq, xk, hm2_ref):
    outs = []
    for j in range(2):
        kj = xk[:, j * LANE:(j + 1) * LANE]
        kbd = jnp.concatenate([kj, kj], axis=0) * hm2_ref[...]
        outs.append(lax.dot_general(xq[:, j * LANE:(j + 1) * LANE], kbd, _NT, preferred_element_type=F32))
    return jnp.concatenate(outs, axis=1)


def _row_bcast(ref, r):
    return jnp.broadcast_to(ref[r:r + 1, :], (8, ref.shape[1]))


def _level_operand(p, d, g2g, qgrp, kgrp, gbuf, iota8):
    blk, half = 1 << p, 1 << (p - 1)
    mid = half if d == 0 else half - 1
    out = []
    for j in range(CHUNK // 8):
        r0 = 8 * j
        if blk >= 16:
            ref = _row_bcast(gbuf, (r0 // blk) * blk + mid)
            query = ((r0 & half) != 0) == (d == 0)
            dlt = g2g[j] - ref if query else ref - g2g[j]
            w = qgrp[j] if query else kgrp[j]
        else:
            ref = _row_bcast(gbuf, r0 + mid)
            for m in range(1, 8 // blk):
                ref = jnp.where(iota8 >= m * blk, _row_bcast(gbuf, r0 + m * blk + mid), ref)
            dlt = -jnp.abs(g2g[j] - ref)
            upper = (iota8 & half) != 0
            w = jnp.where(upper, qgrp[j], kgrp[j]) if d == 0 else jnp.where(upper, kgrp[j], qgrp[j])
        out.append(w * jnp.exp2(dlt))
    return jnp.concatenate(out, axis=0).astype(BF16)


def _gated_dir(q, k, v, la, s_ref, gbuf, slot, d, cst, o_ref, o_lo):
    csum = _cumsum_rows_issue(la, cst["lm"][d])
    vb = v.astype(BF16)
    s_old = [s_ref[slot * HEADS + h] for h in range(HEADS)]
    yield
    g2 = _cumsum_rows_finish(csum) * LOG2E
    last = CHUNK - 1 if d == 0 else 0
    g2_last = g2[last:last + 1, :]
    gbuf[...] = g2
    iota8 = lax.broadcasted_iota(jnp.int32, (8, QK_W), 0)
    groups = [slice(8 * j, 8 * j + 8) for j in range(CHUNK // 8)]
    g2g, qgrp, kgrp = [g2[r] for r in groups], [q[r] for r in groups], [k[r] for r in groups]
    scores = [_pair_scores(q.astype(BF16), k.astype(BF16), cst["hm2"])]
    for p in range(1, N_LEVELS):
        x = _level_operand(p, d, g2g, qgrp, kgrp, gbuf, iota8)
        scores.append(_pair_scores(x, x, cst["hm2"]))
    qg = (q * jnp.exp2(g2)).astype(BF16)
    kg = (k * jnp.exp2(g2_last - g2)).astype(BF16)
    dec = _to_column(jnp.exp2(g2_last), cst["ones3"])
    dss = [lax.dot_general(kg[:, j * LANE:(j + 1) * LANE], vb[:, 2 * j * DV:(2 * j + 2) * DV], _TN,
                           preferred_element_type=F32) for j in range(2)]
    yield
    a = scores[0] * cst["lvl"][d, 0]
    for p in range(1, N_LEVELS):
        a = a + scores[p] * cst["lvl"][d, p]
    ab = a.astype(BF16)
    outs = []
    for j in range(2):
        rhs = jnp.concatenate([
            _block_diag2(vb[:, 2 * j * DV:(2 * j + 1) * DV], vb[:, (2 * j + 1) * DV:(2 * j + 2) * DV]),
            _block_diag2(s_old[2 * j].astype(BF16), s_old[2 * j + 1].astype(BF16)),
        ], axis=0)
        lhs = jnp.concatenate([ab[:, j * LANE:(j + 1) * LANE], qg[:, j * LANE:(j + 1) * LANE]], axis=1)
        outs.append(jnp.dot(lhs, rhs, preferred_element_type=F32))
    yield
    for j in range(2):
        for h2 in range(2):
            h = 2 * j + h2
            s_ref[slot * HEADS + h] = (dec[h * DK:(h + 1) * DK, :] * s_old[h]
                                       + dss[j][h2 * DK:(h2 + 1) * DK, h2 * DV:(h2 + 1) * DV])
    o_ref[:, o_lo:o_lo + MIX_W] = jnp.concatenate(outs, axis=1)


def _mlstm_dir(q, k, v, g16, c_ref, n_ref, m_ref, slot, d, cst, o_ref):
    neg_inf = float("-inf")
    pre = g16 + cst["bias16"][...]
    cum3 = jnp.dot(jnp.concatenate(_split3(_log_sigmoid(pre)), axis=0), cst["lm"][1 - d],
                   preferred_element_type=F32)
    qk = _pair_scores(q.astype(BF16), k.astype(BF16), cst["hm2"])
    vb = v.astype(BF16)
    c_old = [c_ref[slot * HEADS + h] for h in range(HEADS)]
    n_old = n_ref[slot]
    yield
    cum = (cum3[0:16] + cum3[16:32]) + cum3[32:48]
    is_f = (lax.broadcasted_iota(jnp.int32, (N_MIF, CHUNK), 0) & HEADS) != 0
    y = jnp.where(is_f, cum, pre)
    be = lax.dot_general(jnp.concatenate(_split3(y), axis=0), cst["sel3"][d], _TN,
                         preferred_element_type=F32)
    yield
    b_exp, li_exp = be[:, :QK_W], be[:, QK_W:]
    r0 = d * 2 * HEADS
    li_row = jnp.concatenate([y[r0 + h:r0 + h + 1, :] for h in range(HEADS)], axis=1)
    b_row = jnp.concatenate([y[r0 + HEADS + h:r0 + HEADS + h + 1, :] for h in range(HEADS)], axis=1)
    m_prev = m_ref[slot:slot + 1, :]
    inter = b_exp + m_prev
    dmat = jnp.where(cst["tri"][d] > 0.0, b_exp - b_row + li_row, neg_inf)
    lane_head = lax.broadcasted_iota(jnp.int32, (CHUNK, QK_W), 1) // DK
    mt = inter
    for h in range(HEADS):
        sel = lane_head == h
        rm = jnp.max(jnp.where(sel, dmat, neg_inf), axis=-1, keepdims=True)
        mt = jnp.where(sel, jnp.maximum(inter, rm), mt)
    sc = (qk * jnp.exp(dmat - mt)).astype(BF16)
    qe = (q * jnp.exp(inter - mt)).astype(BF16)
    last = CHUNK - 1 if d == 0 else 0
    b_last = b_exp[last:last + 1, :]
    lw = b_last - b_exp + li_exp
    m_new = jnp.maximum(b_last + m_prev, jnp.max(lw, axis=0, keepdims=True))
    kw = (k * jnp.exp(lw - m_new)).astype(BF16)
    dec_col = _to_column(jnp.exp(b_last + m_prev - m_new), cst["ones3"])
    ones_blk = jnp.ones((CHUNK, LANE), BF16)
    nds, dcs = [], []
    for j in range(2):
        rows = slice(j * LANE, (j + 1) * LANE)
        v_pair = vb[:, 2 * j * DV:(2 * j + 2) * DV]
        top = jnp.concatenate([_block_diag2(v_pair[:, :DV], v_pair[:, DV:]), cst["onesbd"][rows, :]], axis=1)
        bot = jnp.concatenate([_block_diag2(c_old[2 * j].astype(BF16), c_old[2 * j + 1].astype(BF16)),
                               n_old[rows, :].astype(BF16)], axis=1)
        lhs = jnp.concatenate([sc[:, rows], qe[:, rows]], axis=1)
        nds.append(jnp.dot(lhs, jnp.concatenate([top, bot], axis=0), preferred_element_type=F32))
        dcs.append(lax.dot_general(kw[:, rows], jnp.concatenate([v_pair, ones_blk], axis=1), _TN,
                                   preferred_element_type=F32))
    yield
    outs = []
    for j in range(2):
        rows = slice(j * LANE, (j + 1) * LANE)
        nd, dc = nds[j], dcs[j]
        for h2 in range(2):
            h = 2 * j + h2
            den = nd[:, 2 * DV + h:2 * DV + h + 1]
            mth = mt[:, h * DK:h * DK + 1]
            outs.append(nd[:, h2 * DV:(h2 + 1) * DV] / jnp.maximum(jnp.abs(den), jnp.exp(-mth)))
            c_ref[slot * HEADS + h] = (dec_col[h * DK:(h + 1) * DK, :] * c_old[h]
                                       + dc[h2 * DK:(h2 + 1) * DK, h2 * DV:(h2 + 1) * DV])
        n_ref[slot, rows, :] = dec_col[rows, :] * n_old[rows, :] + cst["cm"][rows, :] * dc[:, 2 * DV:]
    m_ref[slot:slot + 1, :] = m_new
    o_ref[:, 0:MIX_W] = jnp.concatenate(outs, axis=1)


_DONE = object()
_PER_LAYER = ("bias16", "wup3", "glab")
_SCAN_CONSTS = ("lm", "lvl", "tri", "sel3", "hm2", "onesbd", "cm", "ones3", "bias16", "wup3", "glab", "gamma")


def _scan_kernel(*refs, layer, has_init, emit_state):
    pf_ref, pb_ref, gf_ref, gb_ref = refs[:4]
    pos = 4
    cst = dict(zip(_SCAN_CONSTS, refs[pos:pos + len(_SCAN_CONSTS)]))
    pos += len(_SCAN_CONSTS)
    if has_init:
        c0_ref, n0_ref, m0_ref, g0_ref, h0_ref = refs[pos:pos + 5]
        pos += 5
    of_ref, ob_ref = refs[pos:pos + 2]
    pos += 2
    if emit_state:
        cout_ref, nout_ref, mout_ref, gout_ref, hout_ref = refs[pos:pos + 5]
        pos += 5
    c_scr, n_scr, m_scr, g_scr, h_scr, gbuf = refs[pos:pos + 6]

    ci = pl.program_id(1)

    @pl.when(ci == 0)
    def _():
        if has_init:
            for q in range(SEQ_PER_STEP):
                lo, hi = q * N_DIR * HEADS, (q + 1) * N_DIR * HEADS
                c_scr[lo:hi] = c0_ref[q]
                g_scr[lo:hi] = g0_ref[q]
                h_scr[lo:hi] = h0_ref[q]
                for d in range(N_DIR):
                    n_scr[q * N_DIR + d] = cst["cm"][...] * n0_ref[q, d]
                m_scr[q * N_DIR:(q + 1) * N_DIR, :] = m0_ref[q]
        else:
            c_scr[...] = jnp.zeros(c_scr.shape, F32)
            g_scr[...] = jnp.zeros(g_scr.shape, F32)
            h_scr[...] = jnp.zeros(h_scr.shape, F32)
            n_scr[...] = jnp.zeros(n_scr.shape, F32)
            m_scr[...] = jnp.zeros(m_scr.shape, F32)

    gam = cst["gamma"][...]
    ge = jnp.exp(gam - jnp.max(gam, axis=0, keepdims=True))
    pg = ge / jnp.sum(ge, axis=0, keepdims=True)
    cs = pg[0:1, :]
    for j in range(1, layer + 1):
        cs = cs + pg[j:j + 1, :]
    lb = cs - pg[0:1, :]

    units = []
    for q in range(SEQ_PER_STEP):
        for d, (p4_ref, g4_ref, o4_ref) in enumerate(((pf_ref, gf_ref, of_ref), (pb_ref, gb_ref, ob_ref))):
            slot = q * N_DIR + d
            p_ref, g_ref, o_ref = p4_ref.at[q, 0], g4_ref.at[q], o4_ref.at[q, 0]
            units.append(_mlstm_dir(
                p_ref[:, OFF_MQ:OFF_MQ + QK_W], p_ref[:, OFF_MK:OFF_MK + QK_W], p_ref[:, OFF_MV:OFF_MV + MIX_W],
                g_ref[0], c_scr, n_scr, m_scr, slot, d, cst, o_ref))
            sm = p_ref[:, OFF_SM:OFF_SM + LANE]
            sm_hi = sm.astype(BF16)
            sm_lo = (sm - sm_hi.astype(F32)).astype(BF16)
            pre = jnp.dot(jnp.concatenate([sm_hi, sm_lo, sm_hi], axis=1), cst["wup3"][d],
                          preferred_element_type=F32) + cst["glab"][d:d + 1, :]
            units.append(_gated_dir(
                p_ref[:, OFF_GQ:OFF_GQ + QK_W], p_ref[:, OFF_GK:OFF_GK + QK_W], p_ref[:, OFF_GV:OFF_GV + MIX_W],
                _log_sigmoid(pre) * (1.0 / G_TEMP), g_scr, gbuf.at[2 * slot], slot, d, cst, o_ref, MIX_W))
            zz = p_ref[:, OFF_HF + d * QK_W:OFF_HF + (d + 1) * QK_W]
            ez = jnp.exp(-jnp.abs(zz))
            rz = 1.0 / (1.0 + ez)
            pos_z = zz >= 0.0
            sig = jnp.where(pos_z, rz, ez * rz)
            nsig = jnp.where(pos_z, ez * rz, rz)
            hv = p_ref[:, OFF_HV:OFF_HV + MIX_W]
            units.append(_gated_dir(
                p_ref[:, OFF_HQ:OFF_HQ + QK_W], (1.0 - lb) * nsig, hv * jax.nn.sigmoid(hv),
                jnp.log(lb + (1.0 - lb) * sig), h_scr, gbuf.at[2 * slot + 1], slot, d, cst, o_ref, 2 * MIX_W))
    while units:
        units = [u for u in units if next(u, _DONE) is not _DONE]

    if emit_state:
        @pl.when(ci == pl.num_programs(1) - 1)
        def _():
            for q in range(SEQ_PER_STEP):
                lo, hi = q * N_DIR * HEADS, (q + 1) * N_DIR * HEADS
                cout_ref[q] = c_scr[lo:hi]
                gout_ref[q] = g_scr[lo:hi]
                hout_ref[q] = h_scr[lo:hi]
                for d in range(N_DIR):
                    nout_ref[q, d] = jnp.sum(n_scr[q * N_DIR + d], axis=-1, keepdims=True)
                mout_ref[q] = m_scr[q * N_DIR:(q + 1) * N_DIR, :]


def _scan(p, g_rows, cst, init, *, layer, nbatch, seq_len, emit_state):
    t_total = p.shape[0]
    nc = seq_len // CHUNK
    assert nbatch % SEQ_PER_STEP == 0
    has_init = init is not None

    fwd = lambda b, c: (b, c, 0, 0)
    bwd = lambda b, c: (b, nc - 1 - c, 0, 0)
    p4 = p.reshape(nbatch, nc, CHUNK, P_W)
    g4 = g_rows.reshape(nbatch, nc, N_MIF, CHUNK)
    const_args = [cst[name] for name in _SCAN_CONSTS]
    in_specs = [
        pl.BlockSpec((SEQ_PER_STEP, 1, CHUNK, P_W), fwd),
        pl.BlockSpec((SEQ_PER_STEP, 1, CHUNK, P_W), bwd),
        pl.BlockSpec((SEQ_PER_STEP, 1, N_MIF, CHUNK), fwd),
        pl.BlockSpec((SEQ_PER_STEP, 1, N_MIF, CHUNK), bwd),
    ] + [_const_spec(a.shape[1:], 2, (layer,)) if name in _PER_LAYER else _const_spec(a.shape, 2)
         for name, a in zip(_SCAN_CONSTS, const_args)]
    args = [p4, p4, g4, g4] + const_args
    state_blk = (SEQ_PER_STEP, N_DIR * HEADS, DK, DV)
    state_map = lambda b, c: (b, 0, 0, 0)
    state_specs = [
        pl.BlockSpec(state_blk, state_map),
        pl.BlockSpec((SEQ_PER_STEP, N_DIR, QK_W, 1), state_map),
        pl.BlockSpec((SEQ_PER_STEP, N_DIR, QK_W), lambda b, c: (b, 0, 0)),
        pl.BlockSpec(state_blk, state_map),
        pl.BlockSpec(state_blk, state_map),
    ]
    if has_init:
        in_specs += state_specs
        args += list(init)
    out_specs = [pl.BlockSpec((SEQ_PER_STEP, 1, CHUNK, 3 * MIX_W), fwd),
                 pl.BlockSpec((SEQ_PER_STEP, 1, CHUNK, 3 * MIX_W), bwd)]
    out_shape = [jax.ShapeDtypeStruct((nbatch, nc, CHUNK, 3 * MIX_W), F32)] * 2
    if emit_state:
        out_specs += state_specs
        out_shape += [
            jax.ShapeDtypeStruct((nbatch, N_DIR * HEADS, DK, DV), F32),
            jax.ShapeDtypeStruct((nbatch, N_DIR, QK_W, 1), F32),
            jax.ShapeDtypeStruct((nbatch, N_DIR, QK_W), F32),
            jax.ShapeDtypeStruct((nbatch, N_DIR * HEADS, DK, DV), F32),
            jax.ShapeDtypeStruct((nbatch, N_DIR * HEADS, DK, DV), F32),
        ]
    n_slots = SEQ_PER_STEP * N_DIR
    res = pl.pallas_call(
        functools.partial(_scan_kernel, layer=layer, has_init=has_init, emit_state=emit_state),
        grid=(nbatch // SEQ_PER_STEP, nc),
        in_specs=in_specs,
        out_specs=out_specs,
        out_shape=out_shape,
        scratch_shapes=[
            pltpu.VMEM((n_slots * HEADS, DK, DV), F32),
            pltpu.VMEM((n_slots, QK_W, LANE), F32),
            pltpu.VMEM((8, QK_W), F32),
            pltpu.VMEM((n_slots * HEADS, DK, DV), F32),
            pltpu.VMEM((n_slots * HEADS, DK, DV), F32),
            pltpu.VMEM((2 * n_slots, CHUNK, QK_W), F32),
        ],
        compiler_params=pltpu.CompilerParams(
            dimension_semantics=("arbitrary", "arbitrary"), vmem_limit_bytes=VMEM_LIMIT),
        name="scan",
    )(*args)
    return [res[0].reshape(t_total, 3 * MIX_W), res[1].reshape(t_total, 3 * MIX_W)] + list(res[2:])


def _grid_position(n_tokens):
    rows = n_tokens // GRID_W
    quarter = D_MODEL // 4
    freqs = jnp.exp(-math.log(10000.0) * jnp.arange(quarter, dtype=F32) / quarter)
    r = jnp.arange(rows, dtype=F32)[:, None] * freqs
    cl = jnp.arange(GRID_W, dtype=F32)[:, None] * freqs
    r_emb = jnp.concatenate([jnp.sin(r), jnp.cos(r)], axis=-1)
    c_emb = jnp.concatenate([jnp.sin(cl), jnp.cos(cl)], axis=-1)
    emb = jnp.concatenate([jnp.broadcast_to(r_emb[:, None], (rows, GRID_W, D_MODEL // 2)),
                           jnp.broadcast_to(c_emb[None], (rows, GRID_W, D_MODEL // 2))], axis=-1)
    return emb.reshape(rows * GRID_W, D_MODEL)


def _regroup_kernel(w_ref, ws_ref, wg_ref):
    w = w_ref[...]
    offs = np.concatenate([[0], np.cumsum(IN_SIZES)])
    cols = [w[:, int(offs[i]):int(offs[i + 1])] for i in range(len(IN_SIZES))]
    (mq, mk, mv, mo, mif, gq, gk, gv, gr, glr, hq, hf, hv, hg, mg) = cols
    pad = jnp.zeros((w.shape[0], LANE - N_MIF - N_GLR), F32)
    qs = DK ** -0.5
    ws_ref[...] = jnp.concatenate([mq * qs, mk, mv, gq * qs, gk, gv, hq, hf, hv, mif, glr, pad],
                                  axis=1).astype(BF16)
    wg_ref[...] = jnp.concatenate([mo, gr, hg, mg], axis=1).astype(BF16)


def _regroup(w_in):
    depth, _, n_in = w_in.shape
    tr = 256
    return pl.pallas_call(
        _regroup_kernel,
        grid=(depth, D_MODEL // tr),
        in_specs=[pl.BlockSpec((None, tr, n_in), lambda l, i: (l, i, 0))],
        out_specs=[pl.BlockSpec((None, tr, P_W), lambda l, i: (l, i, 0)),
                   pl.BlockSpec((None, tr, GATE_W), lambda l, i: (l, i, 0))],
        out_shape=[jax.ShapeDtypeStruct((depth, D_MODEL, P_W), BF16),
                   jax.ShapeDtypeStruct((depth, D_MODEL, GATE_W), BF16)],
        compiler_params=pltpu.CompilerParams(
            dimension_semantics=("arbitrary", "arbitrary"), vmem_limit_bytes=VMEM_LIMIT),
        name="regroup",
    )(w_in)


def _prepare_weights(w_in, gate_bias, gla_w_up):
    depth = w_in.shape[0]
    w_scan, w_gate = _regroup(w_in)
    mif_lo = sum(IN_SIZES[:4])
    w_mif_t = jnp.swapaxes(w_in[:, :, mif_lo:mif_lo + N_MIF], 1, 2).astype(BF16)
    bias16 =jnp.broadcast_to(gate_bias.reshape(depth, N_MIF, 1), (depth, N_MIF, CHUNK))
    wup = jnp.zeros((depth, N_DIR, LANE, QK_W), F32)
    for d in range(N_DIR):
        wup = wup.at[:, d, N_MIF + d * G_RANK:N_MIF + (d + 1) * G_RANK, :].set(gla_w_up[:, d])
    wup_hi = wup.astype(BF16)
    wup_lo = (wup - wup_hi.astype(F32)).astype(BF16)
    wup3 = jnp.concatenate([wup_hi, wup_hi, wup_lo], axis=2)
    return w_scan, w_mif_t, w_gate, bias16, wup3


def _gate_rows(g_t, t_total):
    return g_t.reshape(N_MIF, t_total // CHUNK, CHUNK).transpose(1, 0, 2)


def _trunk_layer(x, mod, pos, init, wts, cst, *, layer, nbatch, seq_len, emit_state):
    x = _ffn(x, mod, wts["npre"], wts["npost"], wts["w1"], wts["w2"], layer=layer, half=0, seq_len=seq_len)
    p, g_t = _proj(x, mod, wts["npre"], pos, wts["w_scan"], wts["w_mif_t"], layer=layer, seq_len=seq_len)
    res = _scan(p, _gate_rows(g_t, x.shape[0]), cst, init,
                layer=layer, nbatch=nbatch, seq_len=seq_len, emit_state=emit_state)
    x = _merge(x, mod, wts["npre"], wts["npost"], pos, res[0], res[1], wts["w_gate"], wts["head_norm"],
               wts["w_branch"], wts["w_out"], layer=layer, seq_len=seq_len)
    x = _ffn(x, mod, wts["npre"], wts["npost"], wts["w1"], wts["w2"], layer=layer, half=1, seq_len=seq_len)
    return x, res[2:]


def kernel(x_prompt, x_sample, c, state_mlstm_C, state_mlstm_n, state_mlstm_m, state_gla_S, state_hgrn_S,
           c_ctx, w_ada, b_ada, norm_pre, norm_post, w_ffn_in, w_ffn_out, w_in, mlstm_gate_bias,
           gla_w_up, gla_b, hgrn_gamma, head_norm, w_branch, w_out):
    bp, tp, _ = x_prompt.shape
    bs, ts, _ = x_sample.shape
    depth = w_in.shape[0]

    n_c = 1 + bs
    rows = -(-n_c // 8) * 8
    cvec = jnp.concatenate([c_ctx[None, :], c, jnp.zeros((rows - n_c, D_MODEL), F32)], axis=0)
    mod_all = _ada(cvec, w_ada, b_ada)[:, :n_c].reshape(depth, n_c, 9, D_MODEL)

    w_scan, w_mif_t, w_gate, bias16, wup3 = _prepare_weights(w_in, mlstm_gate_bias, gla_w_up)
    wts = dict(
        npre=norm_pre.reshape(depth, 3, 1, D_MODEL), npost=norm_post.reshape(depth, 3, 1, D_MODEL),
        w1=w_ffn_in.astype(BF16), w2=w_ffn_out.astype(BF16),
        w_scan=w_scan, w_mif_t=w_mif_t, w_gate=w_gate,
        head_norm=head_norm, w_branch=w_branch.astype(BF16), w_out=w_out.astype(BF16),
    )
    cst = dict(_scan_constants(), bias16=bias16, wup3=wup3, glab=gla_b, gamma=hgrn_gamma)

    pos = _grid_position(ts).astype(F32)
    xp = x_prompt.reshape(bp * tp, D_MODEL)
    xs = x_sample.reshape(bs * ts, D_MODEL)
    new_states = []
    for l in range(depth):
        xp, st = _trunk_layer(xp, mod_all[l, 0:1], None, None, wts, cst,
                              layer=l, nbatch=bp, seq_len=tp, emit_state=True)
        new_states.append(st)
        init = (
            state_mlstm_C[:, l].reshape(bs, N_DIR * HEADS, DK, DV),
            state_mlstm_n[:, l].reshape(bs, N_DIR, QK_W, 1),
            jnp.repeat(state_mlstm_m[:, l], DK, axis=-1),
            state_gla_S[:, l].reshape(bs, N_DIR * HEADS, DK, DV),
            state_hgrn_S[:, l].reshape(bs, N_DIR * HEADS, DK, DV),
        )
        xs, _ = _trunk_layer(xs, mod_all[l, 1:], pos, init, wts, cst,
                             layer=l, nbatch=bs, seq_len=ts, emit_state=False)

    blk = (bp, N_DIR, HEADS, DK, DV)
    new_c = jnp.stack([st[0].reshape(blk) for st in new_states], axis=1)
    new_n = jnp.stack([st[1].reshape(bp, N_DIR, HEADS, DK) for st in new_states], axis=1)
    new_m = jnp.stack([st[2][:, :, ::DK] for st in new_states], axis=1)
    new_g = jnp.stack([st[3].reshape(blk) for st in new_states], axis=1)
    new_h = jnp.stack([st[4].reshape(blk) for st in new_states], axis=1)
    return (xp.reshape(bp, tp, D_MODEL), xs.reshape(bs, ts, D_MODEL), new_c, new_n, new_m, new_g, new_h)
```

```python
import functools
import math

import numpy as np
import jax
import jax.numpy as jnp
from jax import lax
from jax.experimental import pallas as pl
from jax.experimental.pallas import tpu as pltpu

F32 = jnp.float32
BF16 = jnp.bfloat16
HIGHEST = lax.Precision.HIGHEST

D_MODEL = 1024
D_FF = 2816
GRID_W = 64
CHUNK = 64
EPS = 1e-6
N_DIR = 2
HEADS = 4
DK = 64
DV = 128
QK_W = HEADS * DK
MIX_W = HEADS * DV
G_RANK = 16
G_TEMP = 16.0
N_MIF = N_DIR * 2 * HEADS
N_GLR = N_DIR * G_RANK
LANE = 128
N_LEVELS = 7
LOG2E = 1.4426950408889634
SEQ_PER_STEP = 4

OFF_MQ, OFF_MK, OFF_MV = 0, 256, 512
OFF_GQ, OFF_GK, OFF_GV = 1024, 1280, 1536
OFF_HQ, OFF_HF, OFF_HV = 2048, 2304, 2816
OFF_SM = 3328
P_W = OFF_SM + LANE
GATE_W = 3 * MIX_W + 3 * D_MODEL

IN_SIZES = (256, 256, 512, 512, N_MIF, 256, 256, 512, 512, N_GLR, 256, 512, 512, 512, 3 * D_MODEL)

FFN_CHUNK = 256
PROJ_CHUNK = 1152
VMEM_LIMIT = 56 * 1024 * 1024


def _const_spec(shape, grid_rank, lead=()):
    index = tuple(lead) + (0,) * len(shape)
    block = (None,) * len(lead) + tuple(shape)
    if grid_rank == 1:
        imap = lambda i: index
    else:
        imap = lambda i, j: index
    return pl.BlockSpec(block, imap, pipeline_mode=pl.Buffered(1))


def _rms(x, w):
    ms = jnp.mean(x * x, axis=-1, keepdims=True)
    return x * lax.rsqrt(ms + EPS) * w


def _modulated(x, mod_ref, npre_ref, mi):
    shift = mod_ref[0, mi:mi + 1, :]
    scale = mod_ref[0, mi + 1:mi + 2, :]
    return _rms(x, npre_ref[...]) * (1.0 + scale) + shift


def _row_tile(t_total, seq_len, per_batch_mod):
    for tm in (512, 256, 128, 64):
        if t_total % tm == 0 and (not per_batch_mod or seq_len % tm == 0):
            return tm
    raise ValueError("token count must be a multiple of 64")


def _ada_kernel(c_ref, w_ref, b_ref, o_ref):
    cv = c_ref[...]
    s = cv * jax.nn.sigmoid(cv)
    o_ref[0] = jnp.dot(s, w_ref[0], precision=HIGHEST, preferred_element_type=F32) + b_ref[0]


def _ada(cvec, w_ada, b_ada):
    depth, _, n = w_ada.shape
    tn = 1536
    rows = cvec.shape[0]
    return pl.pallas_call(
        _ada_kernel,
        grid=(depth, n // tn),
        in_specs=[
            pl.BlockSpec((rows, D_MODEL), lambda l, j: (0, 0)),
            pl.BlockSpec((1, D_MODEL, tn), lambda l, j: (l, 0, j)),
            pl.BlockSpec((1, 1, tn), lambda l, j: (l, 0, j)),
        ],
        out_specs=pl.BlockSpec((1, rows, tn), lambda l, j: (l, 0, j)),
        out_shape=jax.ShapeDtypeStruct((depth, rows, n), F32),
        compiler_params=pltpu.CompilerParams(
            dimension_semantics=("arbitrary", "arbitrary"), vmem_limit_bytes=VMEM_LIMIT),
        name="ada",
    )(cvec, w_ada, b_ada.reshape(depth, 1, n))


def _ffn_kernel(x_ref, mod_ref, npre_ref, npost_ref, w1_ref, w2_ref, o_ref, acc_ref, *, mi):
    x = x_ref[...]
    h = _modulated(x, mod_ref, npre_ref, mi).astype(BF16)
    for k in range(D_FF // FFN_CHUNK):
        lo = k * FFN_CHUNK
        g = jnp.dot(h, w1_ref[:, lo:lo + FFN_CHUNK], preferred_element_type=F32)
        u = jnp.dot(h, w1_ref[:, D_FF + lo:D_FF + lo + FFN_CHUNK], preferred_element_type=F32)
        a = (g * jax.nn.sigmoid(g) * u).astype(BF16)
        part = jnp.dot(a, w2_ref[lo:lo + FFN_CHUNK, :], preferred_element_type=F32)
        if k == 0:
            acc_ref[...] = part
        else:
            acc_ref[...] += part
    gate = mod_ref[0, mi + 2:mi + 3, :]
    o_ref[...] = x + (0.5 * gate) * _rms(acc_ref[...], npost_ref[...])


def _ffn(x, mod, npre, npost, w1, w2, *, layer, half, seq_len):
    mi, ni = 6 * half, 2 * half
    t_total = x.shape[0]
    per_batch = mod.shape[0] > 1
    tm = _row_tile(t_total, seq_len, per_batch)
    mod_map = (lambda i: ((i * tm) // seq_len, 0, 0)) if per_batch else (lambda i: (0, 0, 0))
    return pl.pallas_call(
        functools.partial(_ffn_kernel, mi=mi),
        grid=(t_total // tm,),
        in_specs=[
            pl.BlockSpec((tm, D_MODEL), lambda i: (i, 0)),
            pl.BlockSpec((1, 9, D_MODEL), mod_map),
            _const_spec((1, D_MODEL), 1, (layer, ni)),
            _const_spec((1, D_MODEL), 1, (layer, ni)),
            _const_spec((D_MODEL, 2 * D_FF), 1, (layer, half)),
            _const_spec((D_FF, D_MODEL), 1, (layer, half)),
        ],
        out_specs=pl.BlockSpec((tm, D_MODEL), lambda i: (i, 0)),
        out_shape=jax.ShapeDtypeStruct((t_total, D_MODEL), F32),
        scratch_shapes=[pltpu.VMEM((tm, D_MODEL), F32)],
        compiler_params=pltpu.CompilerParams(
            dimension_semantics=("arbitrary",), vmem_limit_bytes=VMEM_LIMIT),
        name="ffn",
    )(x, mod, npre, npost, w1, w2)


def _proj_kernel(*refs, has_pos):
    if has_pos:
        x_ref, mod_ref, npre_ref, pos_ref, w_ref, wt_ref, o_ref, ot_ref = refs
    else:
        x_ref, mod_ref, npre_ref, w_ref, wt_ref, o_ref, ot_ref = refs
    h = _modulated(x_ref[...], mod_ref, npre_ref, 3)
    if has_pos:
        h = h + pos_ref[...]
    hb = h.astype(BF16)
    for j in range(P_W // PROJ_CHUNK):
        lo = j * PROJ_CHUNK
        o_ref[:, lo:lo + PROJ_CHUNK] = jnp.dot(hb, w_ref[:, lo:lo + PROJ_CHUNK], preferred_element_type=F32)
    ot_ref[...] = lax.dot_general(wt_ref[...], hb, (((1,), (1,)), ((), ())), preferred_element_type=F32)


def _proj(x, mod, npre, pos, w_scan, w_mif_t, *, layer, seq_len):
    t_total = x.shape[0]
    per_batch = mod.shape[0] > 1
    tm = _row_tile(t_total, seq_len, per_batch or pos is not None)
    mod_map = (lambda i: ((i * tm) // seq_len, 0, 0)) if per_batch else (lambda i: (0, 0, 0))
    in_specs = [
        pl.BlockSpec((tm, D_MODEL), lambda i: (i, 0)),
        pl.BlockSpec((1, 9, D_MODEL), mod_map),
        _const_spec((1, D_MODEL), 1, (layer, 1)),
    ]
    args = [x, mod, npre]
    if pos is not None:
        tiles_per_seq = seq_len // tm
        in_specs.append(pl.BlockSpec((tm, D_MODEL), lambda i: (i % tiles_per_seq, 0)))
        args.append(pos)
    in_specs += [_const_spec((D_MODEL, P_W), 1, (layer,)), _const_spec((N_MIF, D_MODEL), 1, (layer,))]
    args += [w_scan, w_mif_t]
    return pl.pallas_call(
        functools.partial(_proj_kernel, has_pos=pos is not None),
        grid=(t_total // tm,),
        in_specs=in_specs,
        out_specs=[pl.BlockSpec((tm, P_W), lambda i: (i, 0)),
                   pl.BlockSpec((N_MIF, tm), lambda i: (0, i))],
        out_shape=[jax.ShapeDtypeStruct((t_total, P_W), F32),
                   jax.ShapeDtypeStruct((N_MIF, t_total), F32)],
        compiler_params=pltpu.CompilerParams(
            dimension_semantics=("arbitrary",), vmem_limit_bytes=VMEM_LIMIT),
        name="proj",
    )(*args)


def _merge_kernel(*refs, has_pos):
    if has_pos:
        (x_ref, mod_ref, npre_ref, npost_ref, pos_ref, of_ref, ob_ref,
         wg_ref, hn_ref, wb_ref, wo_ref, o_ref) = refs
    else:
        (x_ref, mod_ref, npre_ref, npost_ref, of_ref, ob_ref,
         wg_ref, hn_ref, wb_ref, wo_ref, o_ref) = refs
    x = x_ref[...]
    h = _modulated(x, mod_ref, npre_ref, 3)
    if has_pos:
        h = h + pos_ref[...]
    hb = h.astype(BF16)
    gpre = [jnp.dot(hb, wg_ref[:, n * MIX_W:(n + 1) * MIX_W], preferred_element_type=F32) for n in range(3)]
    mgpre = [jnp.dot(hb, wg_ref[:, 3 * MIX_W + n * D_MODEL:3 * MIX_W + (n + 1) * D_MODEL],
                     preferred_element_type=F32) for n in range(3)]
    ons = []
    for n in range(3):
        o = of_ref[:, n * MIX_W:(n + 1) * MIX_W] + ob_ref[:, n * MIX_W:(n + 1) * MIX_W]
        parts = []
        for hh in range(HEADS):
            oh = o[:, hh * DV:(hh + 1) * DV]
            ms = jnp.mean(oh * oh, axis=-1, keepdims=True)
            parts.append(oh * lax.rsqrt(ms + EPS))
        ons.append(jnp.concatenate(parts, axis=1) * hn_ref[n:n + 1, :])
    brs = []
    for n in range(3):
        sg = jax.nn.sigmoid(gpre[n])
        act = sg if n == 0 else gpre[n] * sg
        brs.append(jnp.dot((act * ons[n]).astype(BF16), wb_ref[n], preferred_element_type=F32))
    merged = None
    for n in range(3):
        term = jax.nn.sigmoid(mgpre[n]) * brs[n]
        merged = term if merged is None else merged + term
    out = jnp.dot(merged.astype(BF16), wo_ref[...], preferred_element_type=F32)
    gate = mod_ref[0, 5:6, :]
    o_ref[...] = x + gate * _rms(out, npost_ref[...])


def _merge(x, mod, npre, npost, pos, o_f, o_b, w_gate, head_norm, w_branch, w_out, *, layer, seq_len):
    t_total = x.shape[0]
    per_batch = mod.shape[0] > 1
    tm = _row_tile(t_total, seq_len, per_batch or pos is not None)
    mod_map = (lambda i: ((i * tm) // seq_len, 0, 0)) if per_batch else (lambda i: (0, 0, 0))
    in_specs = [
        pl.BlockSpec((tm, D_MODEL), lambda i: (i, 0)),
        pl.BlockSpec((1, 9, D_MODEL), mod_map),
        _const_spec((1, D_MODEL), 1, (layer, 1)),
        _const_spec((1, D_MODEL), 1, (layer, 1)),
    ]
    args = [x, mod, npre, npost]
    if pos is not None:
        tiles_per_seq = seq_len // tm
        in_specs.append(pl.BlockSpec((tm, D_MODEL), lambda i: (i % tiles_per_seq, 0)))
        args.append(pos)
    in_specs += [
        pl.BlockSpec((tm, 3 * MIX_W), lambda i: (i, 0)),
        pl.BlockSpec((tm, 3 * MIX_W), lambda i: (i, 0)),
        _const_spec((D_MODEL, GATE_W), 1, (layer,)),
        _const_spec((3, MIX_W), 1, (layer,)),
        _const_spec((3, MIX_W, D_MODEL), 1, (layer,)),
        _const_spec((D_MODEL, D_MODEL), 1, (layer,)),
    ]
    args += [o_f, o_b, w_gate, head_norm, w_branch, w_out]
    return pl.pallas_call(
        functools.partial(_merge_kernel, has_pos=pos is not None),
        grid=(t_total // tm,),
        in_specs=in_specs,
        out_specs=pl.BlockSpec((tm, D_MODEL), lambda i: (i, 0)),
        out_shape=jax.ShapeDtypeStruct((t_total, D_MODEL), F32),
        compiler_params=pltpu.CompilerParams(
            dimension_semantics=("arbitrary",), vmem_limit_bytes=VMEM_LIMIT),
        name="merge",
    )(*args)


def _scan_constants():
    t = np.arange(CHUNK)
    lower = (t[None, :] <= t[:, None])
    lm = np.stack([lower, lower.T]).astype(np.float32)
    lvl = np.zeros((N_DIR, N_LEVELS, CHUNK, CHUNK), np.float32)
    lvl[:, 0] = np.eye(CHUNK)
    for p in range(1, N_LEVELS):
        half = 1 << (p - 1)
        same = (t[:, None] >> p) == (t[None, :] >> p)
        fwd = same & ((t[:, None] & half) != 0) & ((t[None, :] & half) == 0)
        lvl[0, p] = fwd
        lvl[1, p] = fwd.T
    tri = lvl.sum(axis=1)
    assert np.array_equal(tri[0], lower) and np.array_equal(tri[1], lower.T)
    lvl = np.tile(lvl, (1, 1, 1, HEADS))
    tri = np.tile(tri, (1, 1, HEADS))
    pair_blk = np.kron(np.eye(2), np.ones((CHUNK, DK)))
    sel = np.zeros((N_DIR, N_MIF, 2 * QK_W), np.float32)
    for d in range(N_DIR):
        for h in range(HEADS):
            sel[d, d * 2 * HEADS + HEADS + h, h * DK:(h + 1) * DK] = 1.0
            sel[d, d * 2 * HEADS + h, QK_W + h * DK:QK_W + (h + 1) * DK] = 1.0
    col_sel = np.zeros((QK_W, LANE), np.float32)
    for h in range(HEADS):
        col_sel[h * DK:(h + 1) * DK, h] = 1.0
    ones3 = np.zeros((16, LANE), np.float32)
    ones3[:3] = 1.0
    return dict(
        lm=jnp.asarray(lm, dtype=BF16), lvl=jnp.asarray(lvl), tri=jnp.asarray(tri),
        sel3=jnp.asarray(np.tile(sel, (1, 3, 1)), dtype=BF16),
        hm2=jnp.asarray(pair_blk, dtype=BF16), onesbd=jnp.asarray(col_sel, dtype=BF16),
        cm=jnp.asarray(col_sel), ones3=jnp.asarray(ones3, dtype=BF16),
    )


_TN = (((0,), (0,)), ((), ()))
_NT = (((1,), (1,)), ((), ()))


def _log_sigmoid(x):
    return jnp.minimum(x, 0.0) - jnp.log(1.0 + jnp.exp(-jnp.abs(x)))


def _split3(x):
    hi = x.astype(BF16)
    r1 = x - hi.astype(F32)
    mid = r1.astype(BF16)
    lo = (r1 - mid.astype(F32)).astype(BF16)
    return hi, mid, lo


def _cumsum_rows_issue(x, lm_bf):
    return jnp.dot(lm_bf, jnp.concatenate(_split3(x), axis=1), preferred_element_type=F32)


def _cumsum_rows_finish(c):
    w = c.shape[1] // 3
    return (c[:, :w] + c[:, w:2 * w]) + c[:, 2 * w:]


def _to_column(row, ones3_ref):
    hi, mid, lo = _split3(row)
    stack = jnp.concatenate([hi.astype(F32), mid.astype(F32), lo.astype(F32),
                             jnp.zeros((13, row.shape[1]), F32)], axis=0).astype(BF16)
    return lax.dot_general(stack, ones3_ref[...], _TN, preferred_element_type=F32)


def _block_diag2(a, b):
    return jnp.concatenate([jnp.concatenate([a, jnp.zeros_like(b)], axis=1),
                            jnp.concatenate([jnp.zeros_like(a), b], axis=1)], axis=0)


def _pair_scores(xq, xk, hm2_ref):
    outs = []
    for j in range(2):
        kj = xk[:, j * LANE:(j + 1) * LANE]
        kbd = jnp.concatenate([kj, kj], axis=0) * hm2_ref[...]
        outs.append(lax.dot_general(xq[:, j * LANE:(j + 1) * LANE], kbd, _NT, preferred_element_type=F32))
    return jnp.concatenate(outs, axis=1)


def _row_bcast(ref, r):
    return jnp.broadcast_to(ref[r:r + 1, :], (8, ref.shape[1]))


def _level_operand(p, d, g2g, qgrp, kgrp, gbuf, iota8):
    blk, half = 1 << p, 1 << (p - 1)
    mid = half if d == 0 else half - 1
    out = []
    for j in range(CHUNK // 8):
        r0 = 8 * j
        if blk >= 16:
            ref = _row_bcast(gbuf, (r0 // blk) * blk + mid)
            query = ((r0 & half) != 0) == (d == 0)
            dlt = g2g[j] - ref if query else ref - g2g[j]
            w = qgrp[j] if query else kgrp[j]
        else:
            ref = _row_bcast(gbuf, r0 + mid)
            for m in range(1, 8 // blk):
                ref = jnp.where(iota8 >= m * blk, _row_bcast(gbuf, r0 + m * blk + mid), ref)
            dlt = -jnp.abs(g2g[j] - ref)
            upper = (iota8 & half) != 0
            w = jnp.where(upper, qgrp[j], kgrp[j]) if d == 0 else jnp.where(upper, kgrp[j], qgrp[j])
        out.append(w * jnp.exp2(dlt))
    return jnp.concatenate(out, axis=0).astype(BF16)


def _gated_dir(q, k, v, la, s_ref, gbuf, slot, d, cst, o_ref, o_lo):
    csum = _cumsum_rows_issue(la, cst["lm"][d])
    vb = v.astype(BF16)
    s_old = [s_ref[slot * HEADS + h] for h in range(HEADS)]
    yield
    g2 = _cumsum_rows_finish(csum) * LOG2E
    last = CHUNK - 1 if d == 0 else 0
    g2_last = g2[last:last + 1, :]
    gbuf[...] = g2
    iota8 = lax.broadcasted_iota(jnp.int32, (8, QK_W), 0)
    groups = [slice(8 * j, 8 * j + 8) for j in range(CHUNK // 8)]
    g2g, qgrp, kgrp = [g2[r] for r in groups], [q[r] for r in groups], [k[r] for r in groups]
    scores = [_pair_scores(q.astype(BF16), k.astype(BF16), cst["hm2"])]
    for p in range(1, N_LEVELS):
        x = _level_operand(p, d, g2g, qgrp, kgrp, gbuf, iota8)
        scores.append(_pair_scores(x, x, cst["hm2"]))
    qg = (q * jnp.exp2(g2)).astype(BF16)
    kg = (k * jnp.exp2(g2_last - g2)).astype(BF16)
    dec = _to_column(jnp.exp2(g2_last), cst["ones3"])
    dss = [lax.dot_general(kg[:, j * LANE:(j + 1) * LANE], vb[:, 2 * j * DV:(2 * j + 2) * DV], _TN,
                           preferred_element_type=F32) for j in range(2)]
    yield
    a = scores[0] * cst["lvl"][d, 0]
    for p in range(1, N_LEVELS):
        a = a + scores[p] * cst["lvl"][d, p]
    ab = a.astype(BF16)
    outs = []
    for j in range(2):
        rhs = jnp.concatenate([
            _block_diag2(vb[:, 2 * j * DV:(2 * j + 1) * DV], vb[:, (2 * j + 1) * DV:(2 * j + 2) * DV]),
            _block_diag2(s_old[2 * j].astype(BF16), s_old[2 * j + 1].astype(BF16)),
        ], axis=0)
        lhs = jnp.concatenate([ab[:, j * LANE:(j + 1) * LANE], qg[:, j * LANE:(j + 1) * LANE]], axis=1)
        outs.append(jnp.dot(lhs, rhs, preferred_element_type=F32))
    yield
    for j in range(2):
        for h2 in range(2):
            h = 2 * j + h2
            s_ref[slot * HEADS + h] = (dec[h * DK:(h + 1) * DK, :] * s_old[h]
                                       + dss[j][h2 * DK:(h2 + 1) * DK, h2 * DV:(h2 + 1) * DV])
    o_ref[:, o_lo:o_lo + MIX_W] = jnp.concatenate(outs, axis=1)


def _mlstm_dir(q, k, v, g16, c_ref, n_ref, m_ref, slot, d, cst, o_ref):
    neg_inf = float("-inf")
    pre = g16 + cst["bias16"][...]
    cum3 = jnp.dot(jnp.concatenate(_split3(_log_sigmoid(pre)), axis=0), cst["lm"][1 - d],
                   preferred_element_type=F32)
    qk = _pair_scores(q.astype(BF16), k.astype(BF16), cst["hm2"])
    vb = v.astype(BF16)
    c_old = [c_ref[slot * HEADS + h] for h in range(HEADS)]
    n_old = n_ref[slot]
    yield
    cum = (cum3[0:16] + cum3[16:32]) + cum3[32:48]
    is_f = (lax.broadcasted_iota(jnp.int32, (N_MIF, CHUNK), 0) & HEADS) != 0
    y = jnp.where(is_f, cum, pre)
    be = lax.dot_general(jnp.concatenate(_split3(y), axis=0), cst["sel3"][d], _TN,
                         preferred_element_type=F32)
    yield
    b_exp, li_exp = be[:, :QK_W], be[:, QK_W:]
    r0 = d * 2 * HEADS
    li_row = jnp.concatenate([y[r0 + h:r0 + h + 1, :] for h in range(HEADS)], axis=1)
    b_row = jnp.concatenate([y[r0 + HEADS + h:r0 + HEADS + h + 1, :] for h in range(HEADS)], axis=1)
    m_prev = m_ref[slot:slot + 1, :]
    inter = b_exp + m_prev
    dmat = jnp.where(cst["tri"][d] > 0.0, b_exp - b_row + li_row, neg_inf)
    lane_head = lax.broadcasted_iota(jnp.int32, (CHUNK, QK_W), 1) // DK
    mt = inter
    for h in range(HEADS):
        sel = lane_head == h
        rm = jnp.max(jnp.where(sel, dmat, neg_inf), axis=-1, keepdims=True)
        mt = jnp.where(sel, jnp.maximum(inter, rm), mt)
    sc = (qk * jnp.exp(dmat - mt)).astype(BF16)
    qe = (q * jnp.exp(inter - mt)).astype(BF16)
    last = CHUNK - 1 if d == 0 else 0
    b_last = b_exp[last:last + 1, :]
    lw = b_last - b_exp + li_exp
    m_new = jnp.maximum(b_last + m_prev, jnp.max(lw, axis=0, keepdims=True))
    kw = (k * jnp.exp(lw - m_new)).astype(BF16)
    dec_col = _to_column(jnp.exp(b_last + m_prev - m_new), cst["ones3"])
    ones_blk = jnp.ones((CHUNK, LANE), BF16)
    nds, dcs = [], []
    for j in range(2):
        rows = slice(j * LANE, (j + 1) * LANE)
        v_pair = vb[:, 2 * j * DV:(2 * j + 2) * DV]
        top = jnp.concatenate([_block_diag2(v_pair[:, :DV], v_pair[:, DV:]), cst["onesbd"][rows, :]], axis=1)
        bot = jnp.concatenate([_block_diag2(c_old[2 * j].astype(BF16), c_old[2 * j + 1].astype(BF16)),
                               n_old[rows, :].astype(BF16)], axis=1)
        lhs = jnp.concatenate([sc[:, rows], qe[:, rows]], axis=1)
        nds.append(jnp.dot(lhs, jnp.concatenate([top, bot], axis=0), preferred_element_type=F32))
        dcs.append(lax.dot_general(kw[:, rows], jnp.concatenate([v_pair, ones_blk], axis=1), _TN,
                                   preferred_element_type=F32))
    yield
    outs = []
    for j in range(2):
        rows = slice(j * LANE, (j + 1) * LANE)
        nd, dc = nds[j], dcs[j]
        for h2 in range(2):
            h = 2 * j + h2
            den = nd[:, 2 * DV + h:2 * DV + h + 1]
            mth = mt[:, h * DK:h * DK + 1]
            outs.append(nd[:, h2 * DV:(h2 + 1) * DV] / jnp.maximum(jnp.abs(den), jnp.exp(-mth)))
            c_ref[slot * HEADS + h] = (dec_col[h * DK:(h + 1) * DK, :] * c_old[h]
                                       + dc[h2 * DK:(h2 + 1) * DK, h2 * DV:(h2 + 1) * DV])
        n_ref[slot, rows, :] = dec_col[rows, :] * n_old[rows, :] + cst["cm"][rows, :] * dc[:, 2 * DV:]
    m_ref[slot:slot + 1, :] = m_new
    o_ref[:, 0:MIX_W] = jnp.concatenate(outs, axis=1)


_DONE = object()
_PER_LAYER = ("bias16", "wup3", "glab")
_SCAN_CONSTS = ("lm", "lvl", "tri", "sel3", "hm2", "onesbd", "cm", "ones3", "bias16", "wup3", "glab", "gamma")


def _scan_kernel(*refs, layer, has_init, emit_state):
    pf_ref, pb_ref, gf_ref, gb_ref = refs[:4]
    pos = 4
    cst = dict(zip(_SCAN_CONSTS, refs[pos:pos + len(_SCAN_CONSTS)]))
    pos += len(_SCAN_CONSTS)
    if has_init:
        c0_ref, n0_ref, m0_ref, g0_ref, h0_ref = refs[pos:pos + 5]
        pos += 5
    of_ref, ob_ref = refs[pos:pos + 2]
    pos += 2
    if emit_state:
        cout_ref, nout_ref, mout_ref, gout_ref, hout_ref = refs[pos:pos + 5]
        pos += 5
    c_scr, n_scr, m_scr, g_scr, h_scr, gbuf = refs[pos:pos + 6]

    ci = pl.program_id(1)

    @pl.when(ci == 0)
    def _():
        if has_init:
            for q in range(SEQ_PER_STEP):
                lo, hi = q * N_DIR * HEADS, (q + 1) * N_DIR * HEADS
                c_scr[lo:hi] = c0_ref[q]
                g_scr[lo:hi] = g0_ref[q]
                h_scr[lo:hi] = h0_ref[q]
                for d in range(N_DIR):
                    n_scr[q * N_DIR + d] = cst["cm"][...] * n0_ref[q, d]
                m_scr[q * N_DIR:(q + 1) * N_DIR, :] = m0_ref[q]
        else:
            c_scr[...] = jnp.zeros(c_scr.shape, F32)
            g_scr[...] = jnp.zeros(g_scr.shape, F32)
            h_scr[...] = jnp.zeros(h_scr.shape, F32)
            n_scr[...] = jnp.zeros(n_scr.shape, F32)
            m_scr[...] = jnp.zeros(m_scr.shape, F32)

    gam = cst["gamma"][...]
    ge = jnp.exp(gam - jnp.max(gam, axis=0, keepdims=True))
    pg = ge / jnp.sum(ge, axis=0, keepdims=True)
    cs = pg[0:1, :]
    for j in range(1, layer + 1):
        cs = cs + pg[j:j + 1, :]
    lb = cs - pg[0:1, :]

    units = []
    for q in range(SEQ_PER_STEP):
        for d, (p4_ref, g4_ref, o4_ref) in enumerate(((pf_ref, gf_ref, of_ref), (pb_ref, gb_ref, ob_ref))):
            slot = q * N_DIR + d
            p_ref, g_ref, o_ref = p4_ref.at[q, 0], g4_ref.at[q], o4_ref.at[q, 0]
            units.append(_mlstm_dir(
                p_ref[:, OFF_MQ:OFF_MQ + QK_W], p_ref[:, OFF_MK:OFF_MK + QK_W], p_ref[:, OFF_MV:OFF_MV + MIX_W],
                g_ref[0], c_scr, n_scr, m_scr, slot, d, cst, o_ref))
            sm = p_ref[:, OFF_SM:OFF_SM + LANE]
            sm_hi = sm.astype(BF16)
            sm_lo = (sm - sm_hi.astype(F32)).astype(BF16)
            pre = jnp.dot(jnp.concatenate([sm_hi, sm_lo, sm_hi], axis=1), cst["wup3"][d],
                          preferred_element_type=F32) + cst["glab"][d:d + 1, :]
            units.append(_gated_dir(
                p_ref[:, OFF_GQ:OFF_GQ + QK_W], p_ref[:, OFF_GK:OFF_GK + QK_W], p_ref[:, OFF_GV:OFF_GV + MIX_W],
                _log_sigmoid(pre) * (1.0 / G_TEMP), g_scr, gbuf.at[2 * slot], slot, d, cst, o_ref, MIX_W))
            zz = p_ref[:, OFF_HF + d * QK_W:OFF_HF + (d + 1) * QK_W]
            ez = jnp.exp(-jnp.abs(zz))
            rz = 1.0 / (1.0 + ez)
            pos_z = zz >= 0.0
            sig = jnp.where(pos_z, rz, ez * rz)
            nsig = jnp.where(pos_z, ez * rz, rz)
            hv = p_ref[:, OFF_HV:OFF_HV + MIX_W]
            units.append(_gated_dir(
                p_ref[:, OFF_HQ:OFF_HQ + QK_W], (1.0 - lb) * nsig, hv * jax.nn.sigmoid(hv),
                jnp.log(lb + (1.0 - lb) * sig), h_scr, gbuf.at[2 * slot + 1], slot, d, cst, o_ref, 2 * MIX_W))
    while units:
        units = [u for u in units if next(u, _DONE) is not _DONE]

    if emit_state:
        @pl.when(ci == pl.num_programs(1) - 1)
        def _():
            for q in range(SEQ_PER_STEP):
                lo, hi = q * N_DIR * HEADS, (q + 1) * N_DIR * HEADS
                cout_ref[q] = c_scr[lo:hi]
                gout_ref[q] = g_scr[lo:hi]
                hout_ref[q] = h_scr[lo:hi]
                for d in range(N_DIR):
                    nout_ref[q, d] = jnp.sum(n_scr[q * N_DIR + d], axis=-1, keepdims=True)
                mout_ref[q] = m_scr[q * N_DIR:(q + 1) * N_DIR, :]


def _scan(p, g_rows, cst, init, *, layer, nbatch, seq_len, emit_state):
    t_total = p.shape[0]
    nc = seq_len // CHUNK
    assert nbatch % SEQ_PER_STEP == 0
    has_init = init is not None

    fwd = lambda b, c: (b, c, 0, 0)
    bwd = lambda b, c: (b, nc - 1 - c, 0, 0)
    p4 = p.reshape(nbatch, nc, CHUNK, P_W)
    g4 = g_rows.reshape(nbatch, nc, N_MIF, CHUNK)
    const_args = [cst[name] for name in _SCAN_CONSTS]
    in_specs = [
        pl.BlockSpec((SEQ_PER_STEP, 1, CHUNK, P_W), fwd),
        pl.BlockSpec((SEQ_PER_STEP, 1, CHUNK, P_W), bwd),
        pl.BlockSpec((SEQ_PER_STEP, 1, N_MIF, CHUNK), fwd),
        pl.BlockSpec((SEQ_PER_STEP, 1, N_MIF, CHUNK), bwd),
    ] + [_const_spec(a.shape[1:], 2, (layer,)) if name in _PER_LAYER else _const_spec(a.shape, 2)
         for name, a in zip(_SCAN_CONSTS, const_args)]
    args = [p4, p4, g4, g4] + const_args
    state_blk = (SEQ_PER_STEP, N_DIR * HEADS, DK, DV)
    state_map = lambda b, c: (b, 0, 0, 0)
    state_specs = [
        pl.BlockSpec(state_blk, state_map),
        pl.BlockSpec((SEQ_PER_STEP, N_DIR, QK_W, 1), state_map),
        pl.BlockSpec((SEQ_PER_STEP, N_DIR, QK_W), lambda b, c: (b, 0, 0)),
        pl.BlockSpec(state_blk, state_map),
        pl.BlockSpec(state_blk, state_map),
    ]
    if has_init:
        in_specs += state_specs
        args += list(init)
    out_specs = [pl.BlockSpec((SEQ_PER_STEP, 1, CHUNK, 3 * MIX_W), fwd),
                 pl.BlockSpec((SEQ_PER_STEP, 1, CHUNK, 3 * MIX_W), bwd)]
    out_shape = [jax.ShapeDtypeStruct((nbatch, nc, CHUNK, 3 * MIX_W), F32)] * 2
    if emit_state:
        out_specs += state_specs
        out_shape += [
            jax.ShapeDtypeStruct((nbatch, N_DIR * HEADS, DK, DV), F32),
            jax.ShapeDtypeStruct((nbatch, N_DIR, QK_W, 1), F32),
            jax.ShapeDtypeStruct((nbatch, N_DIR, QK_W), F32),
            jax.ShapeDtypeStruct((nbatch, N_DIR * HEADS, DK, DV), F32),
            jax.ShapeDtypeStruct((nbatch, N_DIR * HEADS, DK, DV), F32),
        ]
    n_slots = SEQ_PER_STEP * N_DIR
    res = pl.pallas_call(
        functools.partial(_scan_kernel, layer=layer, has_init=has_init, emit_state=emit_state),
        grid=(nbatch // SEQ_PER_STEP, nc),
        in_specs=in_specs,
        out_specs=out_specs,
        out_shape=out_shape,
        scratch_shapes=[
            pltpu.VMEM((n_slots * HEADS, DK, DV), F32),
            pltpu.VMEM((n_slots, QK_W, LANE), F32),
            pltpu.VMEM((8, QK_W), F32),
            pltpu.VMEM((n_slots * HEADS, DK, DV), F32),
            pltpu.VMEM((n_slots * HEADS, DK, DV), F32),
            pltpu.VMEM((2 * n_slots, CHUNK, QK_W), F32),
        ],
        compiler_params=pltpu.CompilerParams(
            dimension_semantics=("arbitrary", "arbitrary"), vmem_limit_bytes=VMEM_LIMIT),
        name="scan",
    )(*args)
    return [res[0].reshape(t_total, 3 * MIX_W), res[1].reshape(t_total, 3 * MIX_W)] + list(res[2:])


def _grid_position(n_tokens):
    rows = n_tokens // GRID_W
    quarter = D_MODEL // 4
    freqs = jnp.exp(-math.log(10000.0) * jnp.arange(quarter, dtype=F32) / quarter)
    r = jnp.arange(rows, dtype=F32)[:, None] * freqs
    cl = jnp.arange(GRID_W, dtype=F32)[:, None] * freqs
    r_emb = jnp.concatenate([jnp.sin(r), jnp.cos(r)], axis=-1)
    c_emb = jnp.concatenate([jnp.sin(cl), jnp.cos(cl)], axis=-1)
    emb = jnp.concatenate([jnp.broadcast_to(r_emb[:, None], (rows, GRID_W, D_MODEL // 2)),
                           jnp.broadcast_to(c_emb[None], (rows, GRID_W, D_MODEL // 2))], axis=-1)
    return emb.reshape(rows * GRID_W, D_MODEL)


def _regroup_kernel(wt_ref, ws_ref, wg_ref):
    offs = [int(o) for o in np.concatenate([[0], np.cumsum(IN_SIZES)])]
    rows = [wt_ref[offs[i]:offs[i + 1], :] for i in range(len(IN_SIZES))]
    (mq, mk, mv, mo, mif, gq, gk, gv, gr, glr, hq, hf, hv, hg, mg) = rows
    small = jnp.concatenate([mif, glr, jnp.zeros((LANE - N_MIF - N_GLR, mif.shape[1]), F32)], axis=0)
    qs = DK ** -0.5
    lo = 0
    for piece in (mq * qs, mk, mv, gq * qs, gk, gv, hq, hf, hv, small):
        ws_ref[:, lo:lo + piece.shape[0]] = piece.T.astype(BF16)
        lo += piece.shape[0]
    lo = 0
    for piece in (mo, gr, hg, mg):
        wg_ref[:, lo:lo + piece.shape[0]] = piece.T.astype(BF16)
        lo += piece.shape[0]


def _regroup(w_in):
    depth, _, n_in = w_in.shape
    tr = 256
    return pl.pallas_call(
        _regroup_kernel,
        grid=(depth, D_MODEL // tr),
        in_specs=[pl.BlockSpec((None, n_in, tr), lambda l, i: (l, 0, i))],
        out_specs=[pl.BlockSpec((None, tr, P_W), lambda l, i: (l, i, 0)),
                   pl.BlockSpec((None, tr, GATE_W), lambda l, i: (l, i, 0))],
        out_shape=[jax.ShapeDtypeStruct((depth, D_MODEL, P_W), BF16),
                   jax.ShapeDtypeStruct((depth, D_MODEL, GATE_W), BF16)],
        compiler_params=pltpu.CompilerParams(
            dimension_semantics=("arbitrary", "arbitrary"), vmem_limit_bytes=VMEM_LIMIT),
        name="regroup",
    )(jnp.swapaxes(w_in, 1, 2))


def _prepare_weights(w_in, gate_bias, gla_w_up):
    depth = w_in.shape[0]
    w_scan, w_gate = _regroup(w_in)
    w_mif_t = jnp.swapaxes(w_scan[:, :, OFF_SM:OFF_SM + N_MIF], 1, 2)
    bias16 =jnp.broadcast_to(gate_bias.reshape(depth, N_MIF, 1), (depth, N_MIF, CHUNK))
    wup = jnp.zeros((depth, N_DIR, LANE, QK_W), F32)
    for d in range(N_DIR):
        wup = wup.at[:, d, N_MIF + d * G_RANK:N_MIF + (d + 1) * G_RANK, :].set(gla_w_up[:, d])
    wup_hi = wup.astype(BF16)
    wup_lo = (wup - wup_hi.astype(F32)).astype(BF16)
    wup3 = jnp.concatenate([wup_hi, wup_hi, wup_lo], axis=2)
    return w_scan, w_mif_t, w_gate, bias16, wup3


def _gate_rows(g_t, t_total):
    return g_t.reshape(N_MIF, t_total // CHUNK, CHUNK).transpose(1, 0, 2)


def _trunk_layer(x, mod, pos, init, wts, cst, *, layer, nbatch, seq_len, emit_state):
    x = _ffn(x, mod, wts["npre"], wts["npost"], wts["w1"], wts["w2"], layer=layer, half=0, seq_len=seq_len)
    p, g_t = _proj(x, mod, wts["npre"], pos, wts["w_scan"], wts["w_mif_t"], layer=layer, seq_len=seq_len)
    res = _scan(p, _gate_rows(g_t, x.shape[0]), cst, init,
                layer=layer, nbatch=nbatch, seq_len=seq_len, emit_state=emit_state)
    x = _merge(x, mod, wts["npre"], wts["npost"], pos, res[0], res[1], wts["w_gate"], wts["head_norm"],
               wts["w_branch"], wts["w_out"], layer=layer, seq_len=seq_len)
    x = _ffn(x, mod, wts["npre"], wts["npost"], wts["w1"], wts["w2"], layer=layer, half=1, seq_len=seq_len)
    return x, res[2:]


def kernel(x_prompt, x_sample, c, state_mlstm_C, state_mlstm_n, state_mlstm_m, state_gla_S, state_hgrn_S,
           c_ctx, w_ada, b_ada, norm_pre, norm_post, w_ffn_in, w_ffn_out, w_in, mlstm_gate_bias,
           gla_w_up, gla_b, hgrn_gamma, head_norm, w_branch, w_out):
    bp, tp, _ = x_prompt.shape
    bs, ts, _ = x_sample.shape
    depth = w_in.shape[0]

    n_c = 1 + bs
    rows = -(-n_c // 8) * 8
    cvec = jnp.concatenate([c_ctx[None, :], c, jnp.zeros((rows - n_c, D_MODEL), F32)], axis=0)
    mod_all = _ada(cvec, w_ada, b_ada)[:, :n_c].reshape(depth, n_c, 9, D_MODEL)

    w_scan, w_mif_t, w_gate, bias16, wup3 = _prepare_weights(w_in, mlstm_gate_bias, gla_w_up)
    wts = dict(
        npre=norm_pre.reshape(depth, 3, 1, D_MODEL), npost=norm_post.reshape(depth, 3, 1, D_MODEL),
        w1=w_ffn_in.astype(BF16), w2=w_ffn_out.astype(BF16),
        w_scan=w_scan, w_mif_t=w_mif_t, w_gate=w_gate,
        head_norm=head_norm, w_branch=w_branch.astype(BF16), w_out=w_out.astype(BF16),
    )
    cst = dict(_scan_constants(), bias16=bias16, wup3=wup3, glab=gla_b, gamma=hgrn_gamma)

    pos = _grid_position(ts).astype(F32)
    xp = x_prompt.reshape(bp * tp, D_MODEL)
    xs = x_sample.reshape(bs * ts, D_MODEL)
    new_states = []
    for l in range(depth):
        xp, st = _trunk_layer(xp, mod_all[l, 0:1], None, None, wts, cst,
                              layer=l, nbatch=bp, seq_len=tp, emit_state=True)
        new_states.append(st)
        init = (
            state_mlstm_C[:, l].reshape(bs, N_DIR * HEADS, DK, DV),
            state_mlstm_n[:, l].reshape(bs, N_DIR, QK_W, 1),
            jnp.repeat(state_mlstm_m[:, l], DK, axis=-1),
            state_gla_S[:, l].reshape(bs, N_DIR * HEADS, DK, DV),
            state_hgrn_S[:, l].reshape(bs, N_DIR * HEADS, DK, DV),
        )
        xs, _ = _trunk_layer(xs, mod_all[l, 1:], pos, init, wts, cst,
                             layer=l, nbatch=bs, seq_len=ts, emit_state=False)

    blk = (bp, N_DIR, HEADS, DK, DV)
    new_c = jnp.stack([st[0].reshape(blk) for st in new_states], axis=1)
    new_n = jnp.stack([st[1].reshape(bp, N_DIR, HEADS, DK) for st in new_states], axis=1)
    new_m = jnp.stack([st[2][:, :, ::DK] for st in new_states], axis=1)
    new_g = jnp.stack([st[3].reshape(blk) for st in new_states], axis=1)
    new_h = jnp.stack([st[4].reshape(blk) for st in new_states], axis=1)
    return (xp.reshape(bp, tp, D_MODEL), xs.reshape(bs, ts, D_MODEL), new_c, new_n, new_m, new_g, new_h)
```

```python
import functools
import math

import numpy as np
import jax
import jax.numpy as jnp
from jax import lax
from jax.experimental import pallas as pl
from jax.experimental.pallas import tpu as pltpu

F32 = jnp.float32
BF16 = jnp.bfloat16
HIGHEST = lax.Precision.HIGHEST

D_MODEL = 1024
D_FF = 2816
GRID_W = 64
CHUNK = 64
EPS = 1e-6
N_DIR = 2
HEADS = 4
DK = 64
DV = 128
QK_W = HEADS * DK
MIX_W = HEADS * DV
G_RANK = 16
G_TEMP = 16.0
N_MIF = N_DIR * 2 * HEADS
N_GLR = N_DIR * G_RANK
LANE = 128
N_LEVELS = 7
LOG2E = 1.4426950408889634
SEQ_PER_STEP = 4

OFF_MQ, OFF_MK, OFF_MV = 0, 256, 512
OFF_GQ, OFF_GK, OFF_GV = 1024, 1280, 1536
OFF_HQ, OFF_HF, OFF_HV = 2048, 2304, 2816
OFF_SM = 3328
P_W = OFF_SM + LANE
GATE_W = 3 * MIX_W + 3 * D_MODEL

IN_SIZES = (256, 256, 512, 512, N_MIF, 256, 256, 512, 512, N_GLR, 256, 512, 512, 512, 3 * D_MODEL)

FFN_CHUNK = 256
PROJ_CHUNK = 1152
VMEM_LIMIT = 56 * 1024 * 1024


def _const_spec(shape, grid_rank, lead=()):
    index = tuple(lead) + (0,) * len(shape)
    block = (None,) * len(lead) + tuple(shape)
    if grid_rank == 1:
        imap = lambda i: index
    else:
        imap = lambda i, j: index
    return pl.BlockSpec(block, imap, pipeline_mode=pl.Buffered(1))


def _rms(x, w):
    ms = jnp.mean(x * x, axis=-1, keepdims=True)
    return x * lax.rsqrt(ms + EPS) * w


def _modulated(x, mod_ref, npre_ref, mi):
    shift = mod_ref[0, mi:mi + 1, :]
    scale = mod_ref[0, mi + 1:mi + 2, :]
    return _rms(x, npre_ref[...]) * (1.0 + scale) + shift


def _row_tile(t_total, seq_len, per_batch_mod):
    for tm in (512, 256, 128, 64):
        if t_total % tm == 0 and (not per_batch_mod or seq_len % tm == 0):
            return tm
    raise ValueError("token count must be a multiple of 64")


def _ada_kernel(c_ref, w_ref, b_ref, o_ref):
    cv = c_ref[...]
    s = cv * jax.nn.sigmoid(cv)
    o_ref[0] = jnp.dot(s, w_ref[0], precision=HIGHEST, preferred_element_type=F32) + b_ref[0]


def _ada(cvec, w_ada, b_ada):
    depth, _, n = w_ada.shape
    tn = 1536
    rows = cvec.shape[0]
    return pl.pallas_call(
        _ada_kernel,
        grid=(depth, n // tn),
        in_specs=[
            pl.BlockSpec((rows, D_MODEL), lambda l, j: (0, 0)),
            pl.BlockSpec((1, D_MODEL, tn), lambda l, j: (l, 0, j)),
            pl.BlockSpec((1, 1, tn), lambda l, j: (l, 0, j)),
        ],
        out_specs=pl.BlockSpec((1, rows, tn), lambda l, j: (l, 0, j)),
        out_shape=jax.ShapeDtypeStruct((depth, rows, n), F32),
        compiler_params=pltpu.CompilerParams(
            dimension_semantics=("arbitrary", "arbitrary"), vmem_limit_bytes=VMEM_LIMIT),
        name="ada",
    )(cvec, w_ada, b_ada.reshape(depth, 1, n))


def _ffn_kernel(x_ref, mod_ref, npre_ref, npost_ref, w1_ref, w2_ref, o_ref, acc_ref, *, mi):
    x = x_ref[...]
    h = _modulated(x, mod_ref, npre_ref, mi).astype(BF16)
    for k in range(D_FF // FFN_CHUNK):
        lo = k * FFN_CHUNK
        g = jnp.dot(h, w1_ref[:, lo:lo + FFN_CHUNK], preferred_element_type=F32)
        u = jnp.dot(h, w1_ref[:, D_FF + lo:D_FF + lo + FFN_CHUNK], preferred_element_type=F32)
        a = (g * jax.nn.sigmoid(g) * u).astype(BF16)
        part = jnp.dot(a, w2_ref[lo:lo + FFN_CHUNK, :], preferred_element_type=F32)
        if k == 0:
            acc_ref[...] = part
        else:
            acc_ref[...] += part
    gate = mod_ref[0, mi + 2:mi + 3, :]
    o_ref[...] = x + (0.5 * gate) * _rms(acc_ref[...], npost_ref[...])


def _ffn(x, mod, npre, npost, w1, w2, *, layer, half, seq_len):
    mi, ni = 6 * half, 2 * half
    t_total = x.shape[0]
    per_batch = mod.shape[0] > 1
    tm = _row_tile(t_total, seq_len, per_batch)
    mod_map = (lambda i: ((i * tm) // seq_len, 0, 0)) if per_batch else (lambda i: (0, 0, 0))
    return pl.pallas_call(
        functools.partial(_ffn_kernel, mi=mi),
        grid=(t_total // tm,),
        in_specs=[
            pl.BlockSpec((tm, D_MODEL), lambda i: (i, 0)),
            pl.BlockSpec((1, 9, D_MODEL), mod_map),
            _const_spec((1, D_MODEL), 1, (layer, ni)),
            _const_spec((1, D_MODEL), 1, (layer, ni)),
            _const_spec((D_MODEL, 2 * D_FF), 1, (layer, half)),
            _const_spec((D_FF, D_MODEL), 1, (layer, half)),
        ],
        out_specs=pl.BlockSpec((tm, D_MODEL), lambda i: (i, 0)),
        out_shape=jax.ShapeDtypeStruct((t_total, D_MODEL), F32),
        scratch_shapes=[pltpu.VMEM((tm, D_MODEL), F32)],
        compiler_params=pltpu.CompilerParams(
            dimension_semantics=("arbitrary",), vmem_limit_bytes=VMEM_LIMIT),
        name="ffn",
    )(x, mod, npre, npost, w1, w2)


def _proj_kernel(*refs, has_pos):
    if has_pos:
        x_ref, mod_ref, npre_ref, pos_ref, w_ref, wt_ref, o_ref, ot_ref = refs
    else:
        x_ref, mod_ref, npre_ref, w_ref, wt_ref, o_ref, ot_ref = refs
    h = _modulated(x_ref[...], mod_ref, npre_ref, 3)
    if has_pos:
        h = h + pos_ref[...]
    hb = h.astype(BF16)
    for j in range(P_W // PROJ_CHUNK):
        lo = j * PROJ_CHUNK
        o_ref[:, lo:lo + PROJ_CHUNK] = jnp.dot(hb, w_ref[:, lo:lo + PROJ_CHUNK], preferred_element_type=F32)
    ot_ref[...] = lax.dot_general(wt_ref[...], hb, (((1,), (1,)), ((), ())), preferred_element_type=F32)


def _proj(x, mod, npre, pos, w_scan, w_mif_t, *, layer, seq_len):
    t_total = x.shape[0]
    per_batch = mod.shape[0] > 1
    tm = _row_tile(t_total, seq_len, per_batch or pos is not None)
    mod_map = (lambda i: ((i * tm) // seq_len, 0, 0)) if per_batch else (lambda i: (0, 0, 0))
    in_specs = [
        pl.BlockSpec((tm, D_MODEL), lambda i: (i, 0)),
        pl.BlockSpec((1, 9, D_MODEL), mod_map),
        _const_spec((1, D_MODEL), 1, (layer, 1)),
    ]
    args = [x, mod, npre]
    if pos is not None:
        tiles_per_seq = seq_len // tm
        in_specs.append(pl.BlockSpec((tm, D_MODEL), lambda i: (i % tiles_per_seq, 0)))
        args.append(pos)
    in_specs += [_const_spec((D_MODEL, P_W), 1, (layer,)), _const_spec((N_MIF, D_MODEL), 1, (layer,))]
    args += [w_scan, w_mif_t]
    return pl.pallas_call(
        functools.partial(_proj_kernel, has_pos=pos is not None),
        grid=(t_total // tm,),
        in_specs=in_specs,
        out_specs=[pl.BlockSpec((tm, P_W), lambda i: (i, 0)),
                   pl.BlockSpec((N_MIF, tm), lambda i: (0, i))],
        out_shape=[jax.ShapeDtypeStruct((t_total, P_W), F32),
                   jax.ShapeDtypeStruct((N_MIF, t_total), F32)],
        compiler_params=pltpu.CompilerParams(
            dimension_semantics=("arbitrary",), vmem_limit_bytes=VMEM_LIMIT),
        name="proj",
    )(*args)


def _merge_kernel(*refs, has_pos):
    if has_pos:
        (x_ref, mod_ref, npre_ref, npost_ref, pos_ref, of_ref, ob_ref,
         wg_ref, hn_ref, wb_ref, wo_ref, o_ref) = refs
    else:
        (x_ref, mod_ref, npre_ref, npost_ref, of_ref, ob_ref,
         wg_ref, hn_ref, wb_ref, wo_ref, o_ref) = refs
    x = x_ref[...]
    h = _modulated(x, mod_ref, npre_ref, 3)
    if has_pos:
        h = h + pos_ref[...]
    hb = h.astype(BF16)
    gpre = [jnp.dot(hb, wg_ref[:, n * MIX_W:(n + 1) * MIX_W], preferred_element_type=F32) for n in range(3)]
    mgpre = [jnp.dot(hb, wg_ref[:, 3 * MIX_W + n * D_MODEL:3 * MIX_W + (n + 1) * D_MODEL],
                     preferred_element_type=F32) for n in range(3)]
    ons = []
    for n in range(3):
        o = of_ref[:, n * MIX_W:(n + 1) * MIX_W] + ob_ref[:, n * MIX_W:(n + 1) * MIX_W]
        parts = []
        for hh in range(HEADS):
            oh = o[:, hh * DV:(hh + 1) * DV]
            ms = jnp.mean(oh * oh, axis=-1, keepdims=True)
            parts.append(oh * lax.rsqrt(ms + EPS))
        ons.append(jnp.concatenate(parts, axis=1) * hn_ref[n:n + 1, :])
    brs = []
    for n in range(3):
        sg = jax.nn.sigmoid(gpre[n])
        act = sg if n == 0 else gpre[n] * sg
        brs.append(jnp.dot((act * ons[n]).astype(BF16), wb_ref[n], preferred_element_type=F32))
    merged = None
    for n in range(3):
        term = jax.nn.sigmoid(mgpre[n]) * brs[n]
        merged = term if merged is None else merged + term
    out = jnp.dot(merged.astype(BF16), wo_ref[...], preferred_element_type=F32)
    gate = mod_ref[0, 5:6, :]
    o_ref[...] = x + gate * _rms(out, npost_ref[...])


def _merge(x, mod, npre, npost, pos, o_f, o_b, w_gate, head_norm, w_branch, w_out, *, layer, seq_len):
    t_total = x.shape[0]
    per_batch = mod.shape[0] > 1
    tm = _row_tile(t_total, seq_len, per_batch or pos is not None)
    mod_map = (lambda i: ((i * tm) // seq_len, 0, 0)) if per_batch else (lambda i: (0, 0, 0))
    in_specs = [
        pl.BlockSpec((tm, D_MODEL), lambda i: (i, 0)),
        pl.BlockSpec((1, 9, D_MODEL), mod_map),
        _const_spec((1, D_MODEL), 1, (layer, 1)),
        _const_spec((1, D_MODEL), 1, (layer, 1)),
    ]
    args = [x, mod, npre, npost]
    if pos is not None:
        tiles_per_seq = seq_len // tm
        in_specs.append(pl.BlockSpec((tm, D_MODEL), lambda i: (i % tiles_per_seq, 0)))
        args.append(pos)
    in_specs += [
        pl.BlockSpec((tm, 3 * MIX_W), lambda i: (i, 0)),
        pl.BlockSpec((tm, 3 * MIX_W), lambda i: (i, 0)),
        _const_spec((D_MODEL, GATE_W), 1, (layer,)),
        _const_spec((3, MIX_W), 1, (layer,)),
        _const_spec((3, MIX_W, D_MODEL), 1, (layer,)),
        _const_spec((D_MODEL, D_MODEL), 1, (layer,)),
    ]
    args += [o_f, o_b, w_gate, head_norm, w_branch, w_out]
    return pl.pallas_call(
        functools.partial(_merge_kernel, has_pos=pos is not None),
        grid=(t_total // tm,),
        in_specs=in_specs,
        out_specs=pl.BlockSpec((tm, D_MODEL), lambda i: (i, 0)),
        out_shape=jax.ShapeDtypeStruct((t_total, D_MODEL), F32),
        compiler_params=pltpu.CompilerParams(
            dimension_semantics=("arbitrary",), vmem_limit_bytes=VMEM_LIMIT),
        name="merge",
    )(*args)


def _scan_constants():
    t = np.arange(CHUNK)
    lower = (t[None, :] <= t[:, None])
    lm = np.stack([lower, lower.T]).astype(np.float32)
    lvl = np.zeros((N_DIR, N_LEVELS, CHUNK, CHUNK), np.float32)
    lvl[:, 0] = np.eye(CHUNK)
    for p in range(1, N_LEVELS):
        half = 1 << (p - 1)
        same = (t[:, None] >> p) == (t[None, :] >> p)
        fwd = same & ((t[:, None] & half) != 0) & ((t[None, :] & half) == 0)
        lvl[0, p] = fwd
        lvl[1, p] = fwd.T
    tri = lvl.sum(axis=1)
    assert np.array_equal(tri[0], lower) and np.array_equal(tri[1], lower.T)
    lvl = np.tile(lvl, (1, 1, 1, HEADS))
    tri = np.tile(tri, (1, 1, HEADS))
    pair_blk = np.kron(np.eye(2), np.ones((CHUNK, DK)))
    sel = np.zeros((N_DIR, N_MIF, 2 * QK_W), np.float32)
    for d in range(N_DIR):
        for h in range(HEADS):
            sel[d, d * 2 * HEADS + HEADS + h, h * DK:(h + 1) * DK] = 1.0
            sel[d, d * 2 * HEADS + h, QK_W + h * DK:QK_W + (h + 1) * DK] = 1.0
    col_sel = np.zeros((QK_W, LANE), np.float32)
    for h in range(HEADS):
        col_sel[h * DK:(h + 1) * DK, h] = 1.0
    return dict(
        lm=jnp.asarray(lm, dtype=BF16), lvl=jnp.asarray(lvl), tri=jnp.asarray(tri),
        sel3=jnp.asarray(np.tile(sel, (1, 3, 1)), dtype=BF16),
        hm2=jnp.asarray(pair_blk, dtype=BF16), onesbd=jnp.asarray(col_sel, dtype=BF16),
        cm=jnp.asarray(col_sel),
    )


_TN = (((0,), (0,)), ((), ()))
_NT = (((1,), (1,)), ((), ()))


def _log_sigmoid(x):
    return jnp.minimum(x, 0.0) - jnp.log(1.0 + jnp.exp(-jnp.abs(x)))


def _split3(x):
    hi = x.astype(BF16)
    r1 = x - hi.astype(F32)
    mid = r1.astype(BF16)
    lo = (r1 - mid.astype(F32)).astype(BF16)
    return hi, mid, lo


def _cumsum_rows_issue(x, lm_bf):
    return jnp.dot(lm_bf, jnp.concatenate(_split3(x), axis=1), preferred_element_type=F32)


def _cumsum_rows_finish(c):
    w = c.shape[1] // 3
    return (c[:, :w] + c[:, w:2 * w]) + c[:, 2 * w:]


def _to_column(row):
    col = jnp.transpose(jnp.broadcast_to(row, (8, row.shape[1])))
    return jnp.broadcast_to(col[:, 0:1], (row.shape[1], LANE))


def _block_diag2(a, b):
    return jnp.concatenate([jnp.concatenate([a, jnp.zeros_like(b)], axis=1),
                            jnp.concatenate([jnp.zeros_like(a), b], axis=1)], axis=0)


def _pair_scores(xq, xk, hm2_ref):
    outs = []
    for j in range(2):
        kj = xk[:, j * LANE:(j + 1) * LANE]
        kbd = jnp.concatenate([kj, kj], axis=0) * hm2_ref[...]
        outs.append(lax.dot_general(xq[:, j * LANE:(j + 1) * LANE], kbd, _NT, preferred_element_type=F32))
    return jnp.concatenate(outs, axis=1)


def _row_bcast(ref, r):
    return jnp.broadcast_to(ref[r:r + 1, :], (8, ref.shape[1]))


def _level_operand(p, d, g2g, qgrp, kgrp, gbuf, iota8):
    blk, half = 1 << p, 1 << (p - 1)
    mid = half if d == 0 else half - 1
    out = []
    for j in range(CHUNK // 8):
        r0 = 8 * j
        if blk >= 16:
            ref = _row_bcast(gbuf, (r0 // blk) * blk + mid)
            query = ((r0 & half) != 0) == (d == 0)
            dlt = g2g[j] - ref if query else ref - g2g[j]
            w = qgrp[j] if query else kgrp[j]
        else:
            ref = _row_bcast(gbuf, r0 + mid)
            for m in range(1, 8 // blk):
                ref = jnp.where(iota8 >= m * blk, _row_bcast(gbuf, r0 + m * blk + mid), ref)
            upper = (iota8 & half) != 0
            dlt = (g2g[j] - ref) * jnp.where(upper == (d == 0), 1.0, -1.0)
            w = jnp.where(upper, qgrp[j], kgrp[j]) if d == 0 else jnp.where(upper, kgrp[j], qgrp[j])
        out.append(w * jnp.exp2(dlt))
    return jnp.concatenate(out, axis=0).astype(BF16)


def _gated_dir(q, k, v, la, s_ref, gbuf, slot, d, cst, o_ref, o_lo):
    csum = _cumsum_rows_issue(la, cst["lm"][d])
    vb = v.astype(BF16)
    s_old = [s_ref[slot * HEADS + h] for h in range(HEADS)]
    yield
    g2 = _cumsum_rows_finish(csum) * LOG2E
    last = CHUNK - 1 if d == 0 else 0
    g2_last = g2[last:last + 1, :]
    gbuf[...] = g2
    iota8 = lax.broadcasted_iota(jnp.int32, (8, QK_W), 0)
    groups = [slice(8 * j, 8 * j + 8) for j in range(CHUNK // 8)]
    g2g, qgrp, kgrp = [g2[r] for r in groups], [q[r] for r in groups], [k[r] for r in groups]
    scores = [_pair_scores(q.astype(BF16), k.astype(BF16), cst["hm2"])]
    for p in range(1, N_LEVELS):
        x = _level_operand(p, d, g2g, qgrp, kgrp, gbuf, iota8)
        scores.append(_pair_scores(x, x, cst["hm2"]))
    qg = (q * jnp.exp2(g2)).astype(BF16)
    kg = (k * jnp.exp2(g2_last - g2)).astype(BF16)
    dec = _to_column(jnp.exp2(g2_last))
    dss = [lax.dot_general(kg[:, j * LANE:(j + 1) * LANE], vb[:, 2 * j * DV:(2 * j + 2) * DV], _TN,
                           preferred_element_type=F32) for j in range(2)]
    yield
    a = scores[0] * cst["lvl"][d, 0]
    for p in range(1, N_LEVELS):
        a = a + scores[p] * cst["lvl"][d, p]
    ab = a.astype(BF16)
    outs = []
    for j in range(2):
        rhs = jnp.concatenate([
            _block_diag2(vb[:, 2 * j * DV:(2 * j + 1) * DV], vb[:, (2 * j + 1) * DV:(2 * j + 2) * DV]),
            _block_diag2(s_old[2 * j].astype(BF16), s_old[2 * j + 1].astype(BF16)),
        ], axis=0)
        lhs = jnp.concatenate([ab[:, j * LANE:(j + 1) * LANE], qg[:, j * LANE:(j + 1) * LANE]], axis=1)
        outs.append(jnp.dot(lhs, rhs, preferred_element_type=F32))
    yield
    for j in range(2):
        for h2 in range(2):
            h = 2 * j + h2
            s_ref[slot * HEADS + h] = (dec[h * DK:(h + 1) * DK, :] * s_old[h]
                                       + dss[j][h2 * DK:(h2 + 1) * DK, h2 * DV:(h2 + 1) * DV])
    o_ref[:, o_lo:o_lo + MIX_W] = jnp.concatenate(outs, axis=1)


def _mlstm_dir(q, k, v, g16, c_ref, n_ref, m_ref, slot, d, cst, o_ref):
    neg_inf = float("-inf")
    pre = g16 + cst["bias16"][...]
    cum3 = jnp.dot(jnp.concatenate(_split3(_log_sigmoid(pre)), axis=0), cst["lm"][1 - d],
                   preferred_element_type=F32)
    qk = _pair_scores(q.astype(BF16), k.astype(BF16), cst["hm2"])
    vb = v.astype(BF16)
    c_old = [c_ref[slot * HEADS + h] for h in range(HEADS)]
    n_old = n_ref[slot]
    yield
    cum = (cum3[0:16] + cum3[16:32]) + cum3[32:48]
    is_f = (lax.broadcasted_iota(jnp.int32, (N_MIF, CHUNK), 0) & HEADS) != 0
    y = jnp.where(is_f, cum, pre)
    be = lax.dot_general(jnp.concatenate(_split3(y), axis=0), cst["sel3"][d], _TN,
                         preferred_element_type=F32)
    yield
    b_exp, li_exp = be[:, :QK_W], be[:, QK_W:]
    r0 = d * 2 * HEADS
    li_row = jnp.concatenate([y[r0 + h:r0 + h + 1, :] for h in range(HEADS)], axis=1)
    b_row = jnp.concatenate([y[r0 + HEADS + h:r0 + HEADS + h + 1, :] for h in range(HEADS)], axis=1)
    m_prev = m_ref[slot:slot + 1, :]
    inter = b_exp + m_prev
    dmat = jnp.where(cst["tri"][d] > 0.0, b_exp - b_row + li_row, neg_inf)
    lane_head = lax.broadcasted_iota(jnp.int32, (CHUNK, QK_W), 1) // DK
    mt = inter
    for h in range(HEADS):
        sel = lane_head == h
        rm = jnp.max(jnp.where(sel, dmat, neg_inf), axis=-1, keepdims=True)
        mt = jnp.where(sel, jnp.maximum(inter, rm), mt)
    sc = (qk * jnp.exp(dmat - mt)).astype(BF16)
    qe = (q * jnp.exp(inter - mt)).astype(BF16)
    last = CHUNK - 1 if d == 0 else 0
    b_last = b_exp[last:last + 1, :]
    lw = b_last - b_exp + li_exp
    m_new = jnp.maximum(b_last + m_prev, jnp.max(lw, axis=0, keepdims=True))
    kw = (k * jnp.exp(lw - m_new)).astype(BF16)
    dec_col = _to_column(jnp.exp(b_last + m_prev - m_new))
    ones_blk = jnp.ones((CHUNK, LANE), BF16)
    nds, dcs = [], []
    for j in range(2):
        rows = slice(j * LANE, (j + 1) * LANE)
        v_pair = vb[:, 2 * j * DV:(2 * j + 2) * DV]
        top = jnp.concatenate([_block_diag2(v_pair[:, :DV], v_pair[:, DV:]), cst["onesbd"][rows, :]], axis=1)
        bot = jnp.concatenate([_block_diag2(c_old[2 * j].astype(BF16), c_old[2 * j + 1].astype(BF16)),
                               n_old[rows, :].astype(BF16)], axis=1)
        lhs = jnp.concatenate([sc[:, rows], qe[:, rows]], axis=1)
        nds.append(jnp.dot(lhs, jnp.concatenate([top, bot], axis=0), preferred_element_type=F32))
        dcs.append(lax.dot_general(kw[:, rows], jnp.concatenate([v_pair, ones_blk], axis=1), _TN,
                                   preferred_element_type=F32))
    yield
    outs = []
    for j in range(2):
        rows = slice(j * LANE, (j + 1) * LANE)
        nd, dc = nds[j], dcs[j]
        for h2 in range(2):
            h = 2 * j + h2
            den = nd[:, 2 * DV + h:2 * DV + h + 1]
            mth = mt[:, h * DK:h * DK + 1]
            outs.append(nd[:, h2 * DV:(h2 + 1) * DV] / jnp.maximum(jnp.abs(den), jnp.exp(-mth)))
            c_ref[slot * HEADS + h] = (dec_col[h * DK:(h + 1) * DK, :] * c_old[h]
                                       + dc[h2 * DK:(h2 + 1) * DK, h2 * DV:(h2 + 1) * DV])
        n_ref[slot, rows, :] = dec_col[rows, :] * n_old[rows, :] + cst["cm"][rows, :] * dc[:, 2 * DV:]
    m_ref[slot:slot + 1, :] = m_new
    o_ref[:, 0:MIX_W] = jnp.concatenate(outs, axis=1)


_DONE = object()
_PER_LAYER = ("bias16", "wup3", "glab")
_SCAN_CONSTS = ("lm", "lvl", "tri", "sel3", "hm2", "onesbd", "cm", "bias16", "wup3", "glab", "gamma")


def _scan_kernel(*refs, layer, has_init, emit_state):
    pf_ref, pb_ref, gf_ref, gb_ref = refs[:4]
    pos = 4
    cst = dict(zip(_SCAN_CONSTS, refs[pos:pos + len(_SCAN_CONSTS)]))
    pos += len(_SCAN_CONSTS)
    if has_init:
        c0_ref, n0_ref, m0_ref, g0_ref, h0_ref = refs[pos:pos + 5]
        pos += 5
    of_ref, ob_ref = refs[pos:pos + 2]
    pos += 2
    if emit_state:
        cout_ref, nout_ref, mout_ref, gout_ref, hout_ref = refs[pos:pos + 5]
        pos += 5
    c_scr, n_scr, m_scr, g_scr, h_scr, gbuf = refs[pos:pos + 6]

    ci = pl.program_id(1)

    @pl.when(ci == 0)
    def _():
        if has_init:
            for q in range(SEQ_PER_STEP):
                lo, hi = q * N_DIR * HEADS, (q + 1) * N_DIR * HEADS
                c_scr[lo:hi] = c0_ref[q]
                g_scr[lo:hi] = g0_ref[q]
                h_scr[lo:hi] = h0_ref[q]
                for d in range(N_DIR):
                    n_scr[q * N_DIR + d] = cst["cm"][...] * n0_ref[q, d]
                m_scr[q * N_DIR:(q + 1) * N_DIR, :] = m0_ref[q]
        else:
            c_scr[...] = jnp.zeros(c_scr.shape, F32)
            g_scr[...] = jnp.zeros(g_scr.shape, F32)
            h_scr[...] = jnp.zeros(h_scr.shape, F32)
            n_scr[...] = jnp.zeros(n_scr.shape, F32)
            m_scr[...] = jnp.zeros(m_scr.shape, F32)

    gam = cst["gamma"][...]
    ge = jnp.exp(gam - jnp.max(gam, axis=0, keepdims=True))
    pg = ge / jnp.sum(ge, axis=0, keepdims=True)
    cs = pg[0:1, :]
    for j in range(1, layer + 1):
        cs = cs + pg[j:j + 1, :]
    lb = cs - pg[0:1, :]

    units = []
    for q in range(SEQ_PER_STEP):
        for d, (p4_ref, g4_ref, o4_ref) in enumerate(((pf_ref, gf_ref, of_ref), (pb_ref, gb_ref, ob_ref))):
            slot = q * N_DIR + d
            p_ref, g_ref, o_ref = p4_ref.at[q, 0], g4_ref.at[q], o4_ref.at[q, 0]
            units.append(_mlstm_dir(
                p_ref[:, OFF_MQ:OFF_MQ + QK_W], p_ref[:, OFF_MK:OFF_MK + QK_W], p_ref[:, OFF_MV:OFF_MV + MIX_W],
                g_ref[0], c_scr, n_scr, m_scr, slot, d, cst, o_ref))
            sm = p_ref[:, OFF_SM:OFF_SM + LANE]
            sm_hi = sm.astype(BF16)
            sm_lo = (sm - sm_hi.astype(F32)).astype(BF16)
            pre = jnp.dot(jnp.concatenate([sm_hi, sm_lo, sm_hi], axis=1), cst["wup3"][d],
                          preferred_element_type=F32) + cst["glab"][d:d + 1, :]
            units.append(_gated_dir(
                p_ref[:, OFF_GQ:OFF_GQ + QK_W], p_ref[:, OFF_GK:OFF_GK + QK_W], p_ref[:, OFF_GV:OFF_GV + MIX_W],
                _log_sigmoid(pre) * (1.0 / G_TEMP), g_scr, gbuf.at[2 * slot], slot, d, cst, o_ref, MIX_W))
            zz = p_ref[:, OFF_HF + d * QK_W:OFF_HF + (d + 1) * QK_W]
            ez = jnp.exp(-jnp.abs(zz))
            rz = 1.0 / (1.0 + ez)
            pos_z = zz >= 0.0
            sig = jnp.where(pos_z, rz, ez * rz)
            nsig = jnp.where(pos_z, ez * rz, rz)
            hv = p_ref[:, OFF_HV:OFF_HV + MIX_W]
            units.append(_gated_dir(
                p_ref[:, OFF_HQ:OFF_HQ + QK_W], (1.0 - lb) * nsig, hv * jax.nn.sigmoid(hv),
                jnp.log(lb + (1.0 - lb) * sig), h_scr, gbuf.at[2 * slot + 1], slot, d, cst, o_ref, 2 * MIX_W))
    while units:
        units = [u for u in units if next(u, _DONE) is not _DONE]

    if emit_state:
        @pl.when(ci == pl.num_programs(1) - 1)
        def _():
            for q in range(SEQ_PER_STEP):
                lo, hi = q * N_DIR * HEADS, (q + 1) * N_DIR * HEADS
                cout_ref[q] = c_scr[lo:hi]
                gout_ref[q] = g_scr[lo:hi]
                hout_ref[q] = h_scr[lo:hi]
                for d in range(N_DIR):
                    nout_ref[q, d] = jnp.sum(n_scr[q * N_DIR + d], axis=-1, keepdims=True)
                mout_ref[q] = m_scr[q * N_DIR:(q + 1) * N_DIR, :]


def _scan(p, g_rows, cst, init, *, layer, nbatch, seq_len, emit_state):
    t_total = p.shape[0]
    nc = seq_len // CHUNK
    assert nbatch % SEQ_PER_STEP == 0
    has_init = init is not None

    fwd = lambda b, c: (b, c, 0, 0)
    bwd = lambda b, c: (b, nc - 1 - c, 0, 0)
    p4 = p.reshape(nbatch, nc, CHUNK, P_W)
    g4 = g_rows.reshape(nbatch, nc, N_MIF, CHUNK)
    const_args = [cst[name] for name in _SCAN_CONSTS]
    in_specs = [
        pl.BlockSpec((SEQ_PER_STEP, 1, CHUNK, P_W), fwd),
        pl.BlockSpec((SEQ_PER_STEP, 1, CHUNK, P_W), bwd),
        pl.BlockSpec((SEQ_PER_STEP, 1, N_MIF, CHUNK), fwd),
        pl.BlockSpec((SEQ_PER_STEP, 1, N_MIF, CHUNK), bwd),
    ] + [_const_spec(a.shape[1:], 2, (layer,)) if name in _PER_LAYER else _const_spec(a.shape, 2)
         for name, a in zip(_SCAN_CONSTS, const_args)]
    args = [p4, p4, g4, g4] + const_args
    state_blk = (SEQ_PER_STEP, N_DIR * HEADS, DK, DV)
    state_map = lambda b, c: (b, 0, 0, 0)
    state_specs = [
        pl.BlockSpec(state_blk, state_map),
        pl.BlockSpec((SEQ_PER_STEP, N_DIR, QK_W, 1), state_map),
        pl.BlockSpec((SEQ_PER_STEP, N_DIR, QK_W), lambda b, c: (b, 0, 0)),
        pl.BlockSpec(state_blk, state_map),
        pl.BlockSpec(state_blk, state_map),
    ]
    if has_init:
        in_specs += state_specs
        args += list(init)
    out_specs = [pl.BlockSpec((SEQ_PER_STEP, 1, CHUNK, 3 * MIX_W), fwd),
                 pl.BlockSpec((SEQ_PER_STEP, 1, CHUNK, 3 * MIX_W), bwd)]
    out_shape = [jax.ShapeDtypeStruct((nbatch, nc, CHUNK, 3 * MIX_W), F32)] * 2
    if emit_state:
        out_specs += state_specs
        out_shape += [
            jax.ShapeDtypeStruct((nbatch, N_DIR * HEADS, DK, DV), F32),
            jax.ShapeDtypeStruct((nbatch, N_DIR, QK_W, 1), F32),
            jax.ShapeDtypeStruct((nbatch, N_DIR, QK_W), F32),
            jax.ShapeDtypeStruct((nbatch, N_DIR * HEADS, DK, DV), F32),
            jax.ShapeDtypeStruct((nbatch, N_DIR * HEADS, DK, DV), F32),
        ]
    n_slots = SEQ_PER_STEP * N_DIR
    res = pl.pallas_call(
        functools.partial(_scan_kernel, layer=layer, has_init=has_init, emit_state=emit_state),
        grid=(nbatch // SEQ_PER_STEP, nc),
        in_specs=in_specs,
        out_specs=out_specs,
        out_shape=out_shape,
        scratch_shapes=[
            pltpu.VMEM((n_slots * HEADS, DK, DV), F32),
            pltpu.VMEM((n_slots, QK_W, LANE), F32),
            pltpu.VMEM((8, QK_W), F32),
            pltpu.VMEM((n_slots * HEADS, DK, DV), F32),
            pltpu.VMEM((n_slots * HEADS, DK, DV), F32),
            pltpu.VMEM((2 * n_slots, CHUNK, QK_W), F32),
        ],
        compiler_params=pltpu.CompilerParams(
            dimension_semantics=("arbitrary", "arbitrary"), vmem_limit_bytes=VMEM_LIMIT),
        name="scan",
    )(*args)
    return [res[0].reshape(t_total, 3 * MIX_W), res[1].reshape(t_total, 3 * MIX_W)] + list(res[2:])


def _grid_position(n_tokens):
    rows = n_tokens // GRID_W
    quarter = D_MODEL // 4
    freqs = jnp.exp(-math.log(10000.0) * jnp.arange(quarter, dtype=F32) / quarter)
    r = jnp.arange(rows, dtype=F32)[:, None] * freqs
    cl = jnp.arange(GRID_W, dtype=F32)[:, None] * freqs
    r_emb = jnp.concatenate([jnp.sin(r), jnp.cos(r)], axis=-1)
    c_emb = jnp.concatenate([jnp.sin(cl), jnp.cos(cl)], axis=-1)
    emb = jnp.concatenate([jnp.broadcast_to(r_emb[:, None], (rows, GRID_W, D_MODEL // 2)),
                           jnp.broadcast_to(c_emb[None], (rows, GRID_W, D_MODEL // 2))], axis=-1)
    return emb.reshape(rows * GRID_W, D_MODEL)


def _regroup_kernel(wt_ref, ws_ref, wg_ref):
    offs = [int(o) for o in np.concatenate([[0], np.cumsum(IN_SIZES)])]
    rows = [wt_ref[offs[i]:offs[i + 1], :] for i in range(len(IN_SIZES))]
    (mq, mk, mv, mo, mif, gq, gk, gv, gr, glr, hq, hf, hv, hg, mg) = rows
    small = jnp.concatenate([mif, glr, jnp.zeros((LANE - N_MIF - N_GLR, mif.shape[1]), F32)], axis=0)
    qs = DK ** -0.5
    lo = 0
    for piece in (mq * qs, mk, mv, gq * qs, gk, gv, hq, hf, hv, small):
        ws_ref[:, lo:lo + piece.shape[0]] = piece.T.astype(BF16)
        lo += piece.shape[0]
    lo = 0
    for piece in (mo, gr, hg, mg):
        wg_ref[:, lo:lo + piece.shape[0]] = piece.T.astype(BF16)
        lo += piece.shape[0]


def _regroup(w_in):
    depth, _, n_in = w_in.shape
    tr = 256
    return pl.pallas_call(
        _regroup_kernel,
        grid=(depth, D_MODEL // tr),
        in_specs=[pl.BlockSpec((None, n_in, tr), lambda l, i: (l, 0, i))],
        out_specs=[pl.BlockSpec((None, tr, P_W), lambda l, i: (l, i, 0)),
                   pl.BlockSpec((None, tr, GATE_W), lambda l, i: (l, i, 0))],
        out_shape=[jax.ShapeDtypeStruct((depth, D_MODEL, P_W), BF16),
                   jax.ShapeDtypeStruct((depth, D_MODEL, GATE_W), BF16)],
        compiler_params=pltpu.CompilerParams(
            dimension_semantics=("arbitrary", "arbitrary"), vmem_limit_bytes=VMEM_LIMIT),
        name="regroup",
    )(jnp.swapaxes(w_in, 1, 2))


def _prepare_weights(w_in, gate_bias, gla_w_up):
    depth = w_in.shape[0]
    w_scan, w_gate = _regroup(w_in)
    w_mif_t = jnp.swapaxes(w_scan[:, :, OFF_SM:OFF_SM + N_MIF], 1, 2)
    bias16 =jnp.broadcast_to(gate_bias.reshape(depth, N_MIF, 1), (depth, N_MIF, CHUNK))
    wup = jnp.zeros((depth, N_DIR, LANE, QK_W), F32)
    for d in range(N_DIR):
        wup = wup.at[:, d, N_MIF + d * G_RANK:N_MIF + (d + 1) * G_RANK, :].set(gla_w_up[:, d])
    wup_hi = wup.astype(BF16)
    wup_lo = (wup - wup_hi.astype(F32)).astype(BF16)
    wup3 = jnp.concatenate([wup_hi, wup_hi, wup_lo], axis=2)
    return w_scan, w_mif_t, w_gate, bias16, wup3


def _gate_rows(g_t, t_total):
    return g_t.reshape(N_MIF, t_total // CHUNK, CHUNK).transpose(1, 0, 2)


def _trunk_layer(x, mod, pos, init, wts, cst, *, layer, nbatch, seq_len, emit_state):
    x = _ffn(x, mod, wts["npre"], wts["npost"], wts["w1"], wts["w2"], layer=layer, half=0, seq_len=seq_len)
    p, g_t = _proj(x, mod, wts["npre"], pos, wts["w_scan"], wts["w_mif_t"], layer=layer, seq_len=seq_len)
    res = _scan(p, _gate_rows(g_t, x.shape[0]), cst, init,
                layer=layer, nbatch=nbatch, seq_len=seq_len, emit_state=emit_state)
    x = _merge(x, mod, wts["npre"], wts["npost"], pos, res[0], res[1], wts["w_gate"], wts["head_norm"],
               wts["w_branch"], wts["w_out"], layer=layer, seq_len=seq_len)
    x = _ffn(x, mod, wts["npre"], wts["npost"], wts["w1"], wts["w2"], layer=layer, half=1, seq_len=seq_len)
    return x, res[2:]


def kernel(x_prompt, x_sample, c, state_mlstm_C, state_mlstm_n, state_mlstm_m, state_gla_S, state_hgrn_S,
           c_ctx, w_ada, b_ada, norm_pre, norm_post, w_ffn_in, w_ffn_out, w_in, mlstm_gate_bias,
           gla_w_up, gla_b, hgrn_gamma, head_norm, w_branch, w_out):
    bp, tp, _ = x_prompt.shape
    bs, ts, _ = x_sample.shape
    depth = w_in.shape[0]

    n_c = 1 + bs
    rows = -(-n_c // 8) * 8
    cvec = jnp.concatenate([c_ctx[None, :], c, jnp.zeros((rows - n_c, D_MODEL), F32)], axis=0)
    mod_all = _ada(cvec, w_ada, b_ada)[:, :n_c].reshape(depth, n_c, 9, D_MODEL)

    w_scan, w_mif_t, w_gate, bias16, wup3 = _prepare_weights(w_in, mlstm_gate_bias, gla_w_up)
    wts = dict(
        npre=norm_pre.reshape(depth, 3, 1, D_MODEL), npost=norm_post.reshape(depth, 3, 1, D_MODEL),
        w1=w_ffn_in.astype(BF16), w2=w_ffn_out.astype(BF16),
        w_scan=w_scan, w_mif_t=w_mif_t, w_gate=w_gate,
        head_norm=head_norm, w_branch=w_branch.astype(BF16), w_out=w_out.astype(BF16),
    )
    cst = dict(_scan_constants(), bias16=bias16, wup3=wup3, glab=gla_b, gamma=hgrn_gamma)

    pos = _grid_position(ts).astype(F32)
    xp = x_prompt.reshape(bp * tp, D_MODEL)
    xs = x_sample.reshape(bs * ts, D_MODEL)
    new_states = []
    for l in range(depth):
        xp, st = _trunk_layer(xp, mod_all[l, 0:1], None, None, wts, cst,
                              layer=l, nbatch=bp, seq_len=tp, emit_state=True)
        new_states.append(st)
        init = (
            state_mlstm_C[:, l].reshape(bs, N_DIR * HEADS, DK, DV),
            state_mlstm_n[:, l].reshape(bs, N_DIR, QK_W, 1),
            jnp.repeat(state_mlstm_m[:, l], DK, axis=-1),
            state_gla_S[:, l].reshape(bs, N_DIR * HEADS, DK, DV),
            state_hgrn_S[:, l].reshape(bs, N_DIR * HEADS, DK, DV),
        )
        xs, _ = _trunk_layer(xs, mod_all[l, 1:], pos, init, wts, cst,
                             layer=l, nbatch=bs, seq_len=ts, emit_state=False)

    blk = (bp, N_DIR, HEADS, DK, DV)
    new_c = jnp.stack([st[0].reshape(blk) for st in new_states], axis=1)
    new_n = jnp.stack([st[1].reshape(bp, N_DIR, HEADS, DK) for st in new_states], axis=1)
    new_m = jnp.stack([st[2][:, :, ::DK] for st in new_states], axis=1)
    new_g = jnp.stack([st[3].reshape(blk) for st in new_states], axis=1)
    new_h = jnp.stack([st[4].reshape(blk) for st in new_states], axis=1)
    return (xp.reshape(bp, tp, D_MODEL), xs.reshape(bs, ts, D_MODEL), new_c, new_n, new_m, new_g, new_h)
```

```python
import functools
import math

import numpy as np
import jax
import jax.numpy as jnp
from jax import lax
from jax.experimental import pallas as pl
from jax.experimental.pallas import tpu as pltpu

F32 = jnp.float32
BF16 = jnp.bfloat16
HIGHEST = lax.Precision.HIGHEST

D_MODEL = 1024
D_FF = 2816
GRID_W = 64
CHUNK = 64
EPS = 1e-6
N_DIR = 2
HEADS = 4
DK = 64
DV = 128
QK_W = HEADS * DK
MIX_W = HEADS * DV
G_RANK = 16
G_TEMP = 16.0
N_MIF = N_DIR * 2 * HEADS
N_GLR = N_DIR * G_RANK
LANE = 128
N_LEVELS = 7
LOG2E = 1.4426950408889634
SEQ_PER_STEP = 4

OFF_MQ, OFF_MK, OFF_MV = 0, 256, 512
OFF_GQ, OFF_GK, OFF_GV = 1024, 1280, 1536
OFF_HQ, OFF_HF, OFF_HV = 2048, 2304, 2816
OFF_SM = 3328
P_W = OFF_SM + LANE
GATE_W = 3 * MIX_W + 3 * D_MODEL

IN_SIZES = (256, 256, 512, 512, N_MIF, 256, 256, 512, 512, N_GLR, 256, 512, 512, 512, 3 * D_MODEL)

FFN_CHUNK = 256
PROJ_CHUNK = 3456
VMEM_LIMIT = 56 * 1024 * 1024


def _const_spec(shape, grid_rank, lead=()):
    index = tuple(lead) + (0,) * len(shape)
    block = (None,) * len(lead) + tuple(shape)
    if grid_rank == 1:
        imap = lambda i: index
    else:
        imap = lambda i, j: index
    return pl.BlockSpec(block, imap, pipeline_mode=pl.Buffered(1))


def _rms(x, w):
    ms = jnp.mean(x * x, axis=-1, keepdims=True)
    return x * lax.rsqrt(ms + EPS) * w


def _modulated(x, mod_ref, npre_ref, mi):
    shift = mod_ref[0, mi:mi + 1, :]
    scale = mod_ref[0, mi + 1:mi + 2, :]
    return _rms(x, npre_ref[...]) * (1.0 + scale) + shift


def _row_tile(t_total, seq_len, per_batch_mod):
    for tm in (512, 256, 128, 64):
        if t_total % tm == 0 and (not per_batch_mod or seq_len % tm == 0):
            return tm
    raise ValueError("token count must be a multiple of 64")


def _ada_kernel(c_ref, w_ref, b_ref, o_ref):
    cv = c_ref[...]
    s = cv * jax.nn.sigmoid(cv)
    o_ref[0] = jnp.dot(s, w_ref[0], precision=HIGHEST, preferred_element_type=F32) + b_ref[0]


def _ada(cvec, w_ada, b_ada):
    depth, _, n = w_ada.shape
    tn = 1536
    rows = cvec.shape[0]
    return pl.pallas_call(
        _ada_kernel,
        grid=(depth, n // tn),
        in_specs=[
            pl.BlockSpec((rows, D_MODEL), lambda l, j: (0, 0)),
            pl.BlockSpec((1, D_MODEL, tn), lambda l, j: (l, 0, j)),
            pl.BlockSpec((1, 1, tn), lambda l, j: (l, 0, j)),
        ],
        out_specs=pl.BlockSpec((1, rows, tn), lambda l, j: (l, 0, j)),
        out_shape=jax.ShapeDtypeStruct((depth, rows, n), F32),
        compiler_params=pltpu.CompilerParams(
            dimension_semantics=("arbitrary", "arbitrary"), vmem_limit_bytes=VMEM_LIMIT),
        name="ada",
    )(cvec, w_ada, b_ada.reshape(depth, 1, n))


def _ffn_kernel(x_ref, mod_ref, npre_ref, npost_ref, w1_ref, w2_ref, o_ref, acc_ref, *, mi):
    x = x_ref[...]
    h = _modulated(x, mod_ref, npre_ref, mi).astype(BF16)
    for k in range(D_FF // FFN_CHUNK):
        lo = k * FFN_CHUNK
        g = jnp.dot(h, w1_ref[:, lo:lo + FFN_CHUNK], preferred_element_type=F32)
        u = jnp.dot(h, w1_ref[:, D_FF + lo:D_FF + lo + FFN_CHUNK], preferred_element_type=F32)
        a = (g * jax.nn.sigmoid(g) * u).astype(BF16)
        part = jnp.dot(a, w2_ref[lo:lo + FFN_CHUNK, :], preferred_element_type=F32)
        if k == 0:
            acc_ref[...] = part
        else:
            acc_ref[...] += part
    gate = mod_ref[0, mi + 2:mi + 3, :]
    o_ref[...] = x + (0.5 * gate) * _rms(acc_ref[...], npost_ref[...])


def _ffn(x, mod, npre, npost, w1, w2, *, layer, half, seq_len):
    mi, ni = 6 * half, 2 * half
    t_total = x.shape[0]
    per_batch = mod.shape[0] > 1
    tm = _row_tile(t_total, seq_len, per_batch)
    mod_map = (lambda i: ((i * tm) // seq_len, 0, 0)) if per_batch else (lambda i: (0, 0, 0))
    return pl.pallas_call(
        functools.partial(_ffn_kernel, mi=mi),
        grid=(t_total // tm,),
        in_specs=[
            pl.BlockSpec((tm, D_MODEL), lambda i: (i, 0)),
            pl.BlockSpec((1, 9, D_MODEL), mod_map),
            _const_spec((1, D_MODEL), 1, (layer, ni)),
            _const_spec((1, D_MODEL), 1, (layer, ni)),
            _const_spec((D_MODEL, 2 * D_FF), 1, (layer, half)),
            _const_spec((D_FF, D_MODEL), 1, (layer, half)),
        ],
        out_specs=pl.BlockSpec((tm, D_MODEL), lambda i: (i, 0)),
        out_shape=jax.ShapeDtypeStruct((t_total, D_MODEL), F32),
        scratch_shapes=[pltpu.VMEM((tm, D_MODEL), F32)],
        compiler_params=pltpu.CompilerParams(
            dimension_semantics=("arbitrary",), vmem_limit_bytes=VMEM_LIMIT),
        name="ffn",
    )(x, mod, npre, npost, w1, w2)


def _proj_kernel(*refs, has_pos):
    if has_pos:
        x_ref, mod_ref, npre_ref, pos_ref, w_ref, o_ref = refs
    else:
        x_ref, mod_ref, npre_ref, w_ref, o_ref = refs
    h = _modulated(x_ref[...], mod_ref, npre_ref, 3)
    if has_pos:
        h = h + pos_ref[...]
    hb = h.astype(BF16)
    for j in range(P_W // PROJ_CHUNK):
        lo = j * PROJ_CHUNK
        o_ref[:, lo:lo + PROJ_CHUNK] = jnp.dot(hb, w_ref[:, lo:lo + PROJ_CHUNK], preferred_element_type=F32)


def _proj(x, mod, npre, pos, w_scan, *, layer, seq_len):
    t_total = x.shape[0]
    per_batch = mod.shape[0] > 1
    tm = _row_tile(t_total, seq_len, per_batch or pos is not None)
    mod_map = (lambda i: ((i * tm) // seq_len, 0, 0)) if per_batch else (lambda i: (0, 0, 0))
    in_specs = [
        pl.BlockSpec((tm, D_MODEL), lambda i: (i, 0)),
        pl.BlockSpec((1, 9, D_MODEL), mod_map),
        _const_spec((1, D_MODEL), 1, (layer, 1)),
    ]
    args = [x, mod, npre]
    if pos is not None:
        tiles_per_seq = seq_len // tm
        in_specs.append(pl.BlockSpec((tm, D_MODEL), lambda i: (i % tiles_per_seq, 0)))
        args.append(pos)
    in_specs.append(_const_spec((D_MODEL, P_W), 1, (layer,)))
    args.append(w_scan)
    return pl.pallas_call(
        functools.partial(_proj_kernel, has_pos=pos is not None),
        grid=(t_total // tm,),
        in_specs=in_specs,
        out_specs=pl.BlockSpec((tm, P_W), lambda i: (i, 0)),
        out_shape=jax.ShapeDtypeStruct((t_total, P_W), F32),
        compiler_params=pltpu.CompilerParams(
            dimension_semantics=("arbitrary",), vmem_limit_bytes=VMEM_LIMIT),
        name="proj",
    )(*args)


def _merge_kernel(*refs, has_pos):
    if has_pos:
        (x_ref, mod_ref, npre_ref, npost_ref, pos_ref, of_ref, ob_ref,
         wg_ref, hn_ref, wb_ref, wo_ref, o_ref) = refs
    else:
        (x_ref, mod_ref, npre_ref, npost_ref, of_ref, ob_ref,
         wg_ref, hn_ref, wb_ref, wo_ref, o_ref) = refs
    x = x_ref[...]
    h = _modulated(x, mod_ref, npre_ref, 3)
    if has_pos:
        h = h + pos_ref[...]
    hb = h.astype(BF16)
    gpre = [jnp.dot(hb, wg_ref[:, n * MIX_W:(n + 1) * MIX_W], preferred_element_type=F32) for n in range(3)]
    mgpre = [jnp.dot(hb, wg_ref[:, 3 * MIX_W + n * D_MODEL:3 * MIX_W + (n + 1) * D_MODEL],
                     preferred_element_type=F32) for n in range(3)]
    ons = []
    for n in range(3):
        o = of_ref[:, n * MIX_W:(n + 1) * MIX_W] + ob_ref[:, n * MIX_W:(n + 1) * MIX_W]
        parts = []
        for hh in range(HEADS):
            oh = o[:, hh * DV:(hh + 1) * DV]
            ms = jnp.mean(oh * oh, axis=-1, keepdims=True)
            parts.append(oh * lax.rsqrt(ms + EPS))
        ons.append(jnp.concatenate(parts, axis=1) * hn_ref[n:n + 1, :])
    brs = []
    for n in range(3):
        sg = jax.nn.sigmoid(gpre[n])
        act = sg if n == 0 else gpre[n] * sg
        brs.append(jnp.dot((act * ons[n]).astype(BF16), wb_ref[n], preferred_element_type=F32))
    merged = None
    for n in range(3):
        term = jax.nn.sigmoid(mgpre[n]) * brs[n]
        merged = term if merged is None else merged + term
    out = jnp.dot(merged.astype(BF16), wo_ref[...], preferred_element_type=F32)
    gate = mod_ref[0, 5:6, :]
    o_ref[...] = x + gate * _rms(out, npost_ref[...])


def _merge(x, mod, npre, npost, pos, o_f, o_b, w_gate, head_norm, w_branch, w_out, *, layer, seq_len):
    t_total = x.shape[0]
    per_batch = mod.shape[0] > 1
    tm = _row_tile(t_total, seq_len, per_batch or pos is not None)
    mod_map = (lambda i: ((i * tm) // seq_len, 0, 0)) if per_batch else (lambda i: (0, 0, 0))
    in_specs = [
        pl.BlockSpec((tm, D_MODEL), lambda i: (i, 0)),
        pl.BlockSpec((1, 9, D_MODEL), mod_map),
        _const_spec((1, D_MODEL), 1, (layer, 1)),
        _const_spec((1, D_MODEL), 1, (layer, 1)),
    ]
    args = [x, mod, npre, npost]
    if pos is not None:
        tiles_per_seq = seq_len // tm
        in_specs.append(pl.BlockSpec((tm, D_MODEL), lambda i: (i % tiles_per_seq, 0)))
        args.append(pos)
    in_specs += [
        pl.BlockSpec((tm, 3 * MIX_W), lambda i: (i, 0)),
        pl.BlockSpec((tm, 3 * MIX_W), lambda i: (i, 0)),
        _const_spec((D_MODEL, GATE_W), 1, (layer,)),
        _const_spec((3, MIX_W), 1, (layer,)),
        _const_spec((3, MIX_W, D_MODEL), 1, (layer,)),
        _const_spec((D_MODEL, D_MODEL), 1, (layer,)),
    ]
    args += [o_f, o_b, w_gate, head_norm, w_branch, w_out]
    return pl.pallas_call(
        functools.partial(_merge_kernel, has_pos=pos is not None),
        grid=(t_total // tm,),
        in_specs=in_specs,
        out_specs=pl.BlockSpec((tm, D_MODEL), lambda i: (i, 0)),
        out_shape=jax.ShapeDtypeStruct((t_total, D_MODEL), F32),
        compiler_params=pltpu.CompilerParams(
            dimension_semantics=("arbitrary",), vmem_limit_bytes=VMEM_LIMIT),
        name="merge",
    )(*args)


def _scan_constants():
    t = np.arange(CHUNK)
    lower = (t[None, :] <= t[:, None])
    lm = np.stack([lower, lower.T]).astype(np.float32)
    lvl = np.zeros((N_DIR, N_LEVELS, CHUNK, CHUNK), np.float32)
    lvl[:, 0] = np.eye(CHUNK)
    for p in range(1, N_LEVELS):
        half = 1 << (p - 1)
        same = (t[:, None] >> p) == (t[None, :] >> p)
        fwd = same & ((t[:, None] & half) != 0) & ((t[None, :] & half) == 0)
        lvl[0, p] = fwd
        lvl[1, p] = fwd.T
    tri = lvl.sum(axis=1)
    assert np.array_equal(tri[0], lower) and np.array_equal(tri[1], lower.T)
    lvl = np.tile(lvl, (1, 1, 1, HEADS))
    tri = np.tile(tri, (1, 1, HEADS))
    pair_blk = np.kron(np.eye(2), np.ones((CHUNK, DK)))
    sel = np.zeros((N_DIR, N_MIF, 2 * QK_W), np.float32)
    for d in range(N_DIR):
        for h in range(HEADS):
            sel[d, d * 2 * HEADS + HEADS + h, h * DK:(h + 1) * DK] = 1.0
            sel[d, d * 2 * HEADS + h, QK_W + h * DK:QK_W + (h + 1) * DK] = 1.0
    col_sel = np.zeros((QK_W, LANE), np.float32)
    for h in range(HEADS):
        col_sel[h * DK:(h + 1) * DK, h] = 1.0
    return dict(
        lm=jnp.asarray(lm, dtype=BF16), lvl=jnp.asarray(lvl), tri=jnp.asarray(tri),
        sel3=jnp.asarray(np.tile(sel, (1, 3, 1)), dtype=BF16),
        hm2=jnp.asarray(pair_blk, dtype=BF16), onesbd=jnp.asarray(col_sel, dtype=BF16),
        cm=jnp.asarray(col_sel),
    )


_TN = (((0,), (0,)), ((), ()))
_NT = (((1,), (1,)), ((), ()))


def _log_sigmoid(x):
    return jnp.minimum(x, 0.0) - jnp.log(1.0 + jnp.exp(-jnp.abs(x)))


def _split3(x):
    hi = x.astype(BF16)
    r1 = x - hi.astype(F32)
    mid = r1.astype(BF16)
    lo = (r1 - mid.astype(F32)).astype(BF16)
    return hi, mid, lo


def _cumsum_rows_issue(x, lm_bf):
    return jnp.dot(lm_bf, jnp.concatenate(_split3(x), axis=1), preferred_element_type=F32)


def _cumsum_rows_finish(c):
    w = c.shape[1] // 3
    return (c[:, :w] + c[:, w:2 * w]) + c[:, 2 * w:]


def _to_column(row):
    col = jnp.transpose(jnp.broadcast_to(row, (8, row.shape[1])))
    return jnp.broadcast_to(col[:, 0:1], (row.shape[1], LANE))


def _block_diag2(a, b):
    return jnp.concatenate([jnp.concatenate([a, jnp.zeros_like(b)], axis=1),
                            jnp.concatenate([jnp.zeros_like(a), b], axis=1)], axis=0)


def _pair_scores(xq, xk, hm2_ref):
    outs = []
    for j in range(2):
        kj = xk[:, j * LANE:(j + 1) * LANE]
        kbd = jnp.concatenate([kj, kj], axis=0) * hm2_ref[...]
        outs.append(lax.dot_general(xq[:, j * LANE:(j + 1) * LANE], kbd, _NT, preferred_element_type=F32))
    return jnp.concatenate(outs, axis=1)


def _row_bcast(ref, r):
    return jnp.broadcast_to(ref[r:r + 1, :], (8, ref.shape[1]))


def _level_operand(p, d, g2g, qgrp, kgrp, gbuf, iota8):
    blk, half = 1 << p, 1 << (p - 1)
    mid = half if d == 0 else half - 1
    out = []
    for j in range(CHUNK // 8):
        r0 = 8 * j
        if blk >= 16:
            ref = _row_bcast(gbuf, (r0 // blk) * blk + mid)
            query = ((r0 & half) != 0) == (d == 0)
            dlt = g2g[j] - ref if query else ref - g2g[j]
            w = qgrp[j] if query else kgrp[j]
        else:
            ref = _row_bcast(gbuf, r0 + mid)
            for m in range(1, 8 // blk):
                ref = jnp.where(iota8 >= m * blk, _row_bcast(gbuf, r0 + m * blk + mid), ref)
            upper = (iota8 & half) != 0
            dlt = (g2g[j] - ref) * jnp.where(upper == (d == 0), 1.0, -1.0)
            w = jnp.where(upper, qgrp[j], kgrp[j]) if d == 0 else jnp.where(upper, kgrp[j], qgrp[j])
        out.append(w * jnp.exp2(dlt))
    return jnp.concatenate(out, axis=0).astype(BF16)


def _gated_dir(q, k, v, la, s_ref, gbuf, slot, d, cst, o_ref, o_lo):
    csum = _cumsum_rows_issue(la, cst["lm"][d])
    vb = v.astype(BF16)
    s_old = [s_ref[slot * HEADS + h] for h in range(HEADS)]
    yield
    g2 = _cumsum_rows_finish(csum) * LOG2E
    last = CHUNK - 1 if d == 0 else 0
    g2_last = g2[last:last + 1, :]
    gbuf[...] = g2
    iota8 = lax.broadcasted_iota(jnp.int32, (8, QK_W), 0)
    groups = [slice(8 * j, 8 * j + 8) for j in range(CHUNK // 8)]
    g2g, qgrp, kgrp = [g2[r] for r in groups], [q[r] for r in groups], [k[r] for r in groups]
    scores = [_pair_scores(q.astype(BF16), k.astype(BF16), cst["hm2"])]
    for p in range(1, N_LEVELS):
        x = _level_operand(p, d, g2g, qgrp, kgrp, gbuf, iota8)
        scores.append(_pair_scores(x, x, cst["hm2"]))
    qg = (q * jnp.exp2(g2)).astype(BF16)
    kg = (k * jnp.exp2(g2_last - g2)).astype(BF16)
    dec = _to_column(jnp.exp2(g2_last))
    dss = [lax.dot_general(kg[:, j * LANE:(j + 1) * LANE], vb[:, 2 * j * DV:(2 * j + 2) * DV], _TN,
                           preferred_element_type=F32) for j in range(2)]
    yield
    a = scores[0] * cst["lvl"][d, 0]
    for p in range(1, N_LEVELS):
        a = a + scores[p] * cst["lvl"][d, p]
    ab = a.astype(BF16)
    outs = []
    for j in range(2):
        rhs = jnp.concatenate([
            _block_diag2(vb[:, 2 * j * DV:(2 * j + 1) * DV], vb[:, (2 * j + 1) * DV:(2 * j + 2) * DV]),
            _block_diag2(s_old[2 * j].astype(BF16), s_old[2 * j + 1].astype(BF16)),
        ], axis=0)
        lhs = jnp.concatenate([ab[:, j * LANE:(j + 1) * LANE], qg[:, j * LANE:(j + 1) * LANE]], axis=1)
        outs.append(jnp.dot(lhs, rhs, preferred_element_type=F32))
    yield
    for j in range(2):
        for h2 in range(2):
            h = 2 * j + h2
            s_ref[slot * HEADS + h] = (dec[h * DK:(h + 1) * DK, :] * s_old[h]
                                       + dss[j][h2 * DK:(h2 + 1) * DK, h2 * DV:(h2 + 1) * DV])
    o_ref[:, o_lo:o_lo + MIX_W] = jnp.concatenate(outs, axis=1)


def _mlstm_dir(q, k, v, g16, c_ref, n_ref, m_ref, slot, d, cst, o_ref):
    neg_inf = float("-inf")
    pre = g16 + cst["bias16"][...]
    cum3 = jnp.dot(jnp.concatenate(_split3(_log_sigmoid(pre)), axis=0), cst["lm"][1 - d],
                   preferred_element_type=F32)
    qk = _pair_scores(q.astype(BF16), k.astype(BF16), cst["hm2"])
    vb = v.astype(BF16)
    c_old = [c_ref[slot * HEADS + h] for h in range(HEADS)]
    n_old = n_ref[slot]
    yield
    cum = (cum3[0:16] + cum3[16:32]) + cum3[32:48]
    is_f = (lax.broadcasted_iota(jnp.int32, (N_MIF, CHUNK), 0) & HEADS) != 0
    y = jnp.where(is_f, cum, pre)
    be = lax.dot_general(jnp.concatenate(_split3(y), axis=0), cst["sel3"][d], _TN,
                         preferred_element_type=F32)
    yield
    b_exp, li_exp = be[:, :QK_W], be[:, QK_W:]
    r0 = d * 2 * HEADS
    li_row = jnp.concatenate([y[r0 + h:r0 + h + 1, :] for h in range(HEADS)], axis=1)
    b_row = jnp.concatenate([y[r0 + HEADS + h:r0 + HEADS + h + 1, :] for h in range(HEADS)], axis=1)
    m_prev = m_ref[slot:slot + 1, :]
    inter = b_exp + m_prev
    dmat = jnp.where(cst["tri"][d] > 0.0, b_exp - b_row + li_row, neg_inf)
    lane_head = lax.broadcasted_iota(jnp.int32, (CHUNK, QK_W), 1) // DK
    mt = inter
    for h in range(HEADS):
        sel = lane_head == h
        rm = jnp.max(jnp.where(sel, dmat, neg_inf), axis=-1, keepdims=True)
        mt = jnp.where(sel, jnp.maximum(inter, rm), mt)
    sc = (qk * jnp.exp(dmat - mt)).astype(BF16)
    qe = (q * jnp.exp(inter - mt)).astype(BF16)
    last = CHUNK - 1 if d == 0 else 0
    b_last = b_exp[last:last + 1, :]
    lw = b_last - b_exp + li_exp
    m_new = jnp.maximum(b_last + m_prev, jnp.max(lw, axis=0, keepdims=True))
    kw = (k * jnp.exp(lw - m_new)).astype(BF16)
    dec_col = _to_column(jnp.exp(b_last + m_prev - m_new))
    ones_blk = jnp.ones((CHUNK, LANE), BF16)
    nds, dcs = [], []
    for j in range(2):
        rows = slice(j * LANE, (j + 1) * LANE)
        v_pair = vb[:, 2 * j * DV:(2 * j + 2) * DV]
        top = jnp.concatenate([_block_diag2(v_pair[:, :DV], v_pair[:, DV:]), cst["onesbd"][rows, :]], axis=1)
        bot = jnp.concatenate([_block_diag2(c_old[2 * j].astype(BF16), c_old[2 * j + 1].astype(BF16)),
                               n_old[rows, :].astype(BF16)], axis=1)
        lhs = jnp.concatenate([sc[:, rows], qe[:, rows]], axis=1)
        nds.append(jnp.dot(lhs, jnp.concatenate([top, bot], axis=0), preferred_element_type=F32))
        dcs.append(lax.dot_general(kw[:, rows], jnp.concatenate([v_pair, ones_blk], axis=1), _TN,
                                   preferred_element_type=F32))
    yield
    outs = []
    for j in range(2):
        rows = slice(j * LANE, (j + 1) * LANE)
        nd, dc = nds[j], dcs[j]
        for h2 in range(2):
            h = 2 * j + h2
            den = nd[:, 2 * DV + h:2 * DV + h + 1]
            mth = mt[:, h * DK:h * DK + 1]
            outs.append(nd[:, h2 * DV:(h2 + 1) * DV] / jnp.maximum(jnp.abs(den), jnp.exp(-mth)))
            c_ref[slot * HEADS + h] = (dec_col[h * DK:(h + 1) * DK, :] * c_old[h]
                                       + dc[h2 * DK:(h2 + 1) * DK, h2 * DV:(h2 + 1) * DV])
        n_ref[slot, rows, :] = dec_col[rows, :] * n_old[rows, :] + cst["cm"][rows, :] * dc[:, 2 * DV:]
    m_ref[slot:slot + 1, :] = m_new
    o_ref[:, 0:MIX_W] = jnp.concatenate(outs, axis=1)


_DONE = object()
_PER_LAYER = ("bias16", "wup3", "glab")
_SCAN_CONSTS = ("lm", "lvl", "tri", "sel3", "hm2", "onesbd", "cm", "bias16", "wup3", "glab", "gamma")


def _scan_kernel(*refs, layer, has_init, emit_state):
    pf_ref, pb_ref = refs[:2]
    pos = 2
    cst = dict(zip(_SCAN_CONSTS, refs[pos:pos + len(_SCAN_CONSTS)]))
    pos += len(_SCAN_CONSTS)
    if has_init:
        c0_ref, n0_ref, m0_ref, g0_ref, h0_ref = refs[pos:pos + 5]
        pos += 5
    of_ref, ob_ref = refs[pos:pos + 2]
    pos += 2
    if emit_state:
        cout_ref, nout_ref, mout_ref, gout_ref, hout_ref = refs[pos:pos + 5]
        pos += 5
    c_scr, n_scr, m_scr, g_scr, h_scr, gbuf = refs[pos:pos + 6]

    ci = pl.program_id(1)

    @pl.when(ci == 0)
    def _():
        if has_init:
            for q in range(SEQ_PER_STEP):
                lo, hi = q * N_DIR * HEADS, (q + 1) * N_DIR * HEADS
                c_scr[lo:hi] = c0_ref[q]
                g_scr[lo:hi] = g0_ref[q]
                h_scr[lo:hi] = h0_ref[q]
                for d in range(N_DIR):
                    n_scr[q * N_DIR + d] = cst["cm"][...] * n0_ref[q, d]
                m_scr[q * N_DIR:(q + 1) * N_DIR, :] = m0_ref[q]
        else:
            c_scr[...] = jnp.zeros(c_scr.shape, F32)
            g_scr[...] = jnp.zeros(g_scr.shape, F32)
            h_scr[...] = jnp.zeros(h_scr.shape, F32)
            n_scr[...] = jnp.zeros(n_scr.shape, F32)
            m_scr[...] = jnp.zeros(m_scr.shape, F32)

    gam = cst["gamma"][...]
    ge = jnp.exp(gam - jnp.max(gam, axis=0, keepdims=True))
    pg = ge / jnp.sum(ge, axis=0, keepdims=True)
    cs = pg[0:1, :]
    for j in range(1, layer + 1):
        cs = cs + pg[j:j + 1, :]
    lb = cs - pg[0:1, :]

    units = []
    for q in range(SEQ_PER_STEP):
        for d, (p4_ref, o4_ref) in enumerate(((pf_ref, of_ref), (pb_ref, ob_ref))):
            slot = q * N_DIR + d
            p_ref, o_ref = p4_ref.at[q, 0], o4_ref.at[q, 0]
            g16 = jnp.transpose(p_ref[:, OFF_SM:OFF_SM + LANE])[0:N_MIF, :]
            units.append(_mlstm_dir(
                p_ref[:, OFF_MQ:OFF_MQ + QK_W], p_ref[:, OFF_MK:OFF_MK + QK_W], p_ref[:, OFF_MV:OFF_MV + MIX_W],
                g16, c_scr, n_scr, m_scr, slot, d, cst, o_ref))
            sm = p_ref[:, OFF_SM:OFF_SM + LANE]
            sm_hi = sm.astype(BF16)
            sm_lo = (sm - sm_hi.astype(F32)).astype(BF16)
            pre = jnp.dot(jnp.concatenate([sm_hi, sm_lo, sm_hi], axis=1), cst["wup3"][d],
                          preferred_element_type=F32) + cst["glab"][d:d + 1, :]
            units.append(_gated_dir(
                p_ref[:, OFF_GQ:OFF_GQ + QK_W], p_ref[:, OFF_GK:OFF_GK + QK_W], p_ref[:, OFF_GV:OFF_GV + MIX_W],
                _log_sigmoid(pre) * (1.0 / G_TEMP), g_scr, gbuf.at[2 * slot], slot, d, cst, o_ref, MIX_W))
            zz = p_ref[:, OFF_HF + d * QK_W:OFF_HF + (d + 1) * QK_W]
            ez = jnp.exp(-jnp.abs(zz))
            rz = 1.0 / (1.0 + ez)
            pos_z = zz >= 0.0
            sig = jnp.where(pos_z, rz, ez * rz)
            nsig = jnp.where(pos_z, ez * rz, rz)
            hv = p_ref[:, OFF_HV:OFF_HV + MIX_W]
            units.append(_gated_dir(
                p_ref[:, OFF_HQ:OFF_HQ + QK_W], (1.0 - lb) * nsig, hv * jax.nn.sigmoid(hv),
                jnp.log(lb + (1.0 - lb) * sig), h_scr, gbuf.at[2 * slot + 1], slot, d, cst, o_ref, 2 * MIX_W))
    while units:
        units = [u for u in units if next(u, _DONE) is not _DONE]

    if emit_state:
        @pl.when(ci == pl.num_programs(1) - 1)
        def _():
            for q in range(SEQ_PER_STEP):
                lo, hi = q * N_DIR * HEADS, (q + 1) * N_DIR * HEADS
                cout_ref[q] = c_scr[lo:hi]
                gout_ref[q] = g_scr[lo:hi]
                hout_ref[q] = h_scr[lo:hi]
                for d in range(N_DIR):
                    nout_ref[q, d] = jnp.sum(n_scr[q * N_DIR + d], axis=-1, keepdims=True)
                mout_ref[q] = m_scr[q * N_DIR:(q + 1) * N_DIR, :]


def _scan(p, cst, init, *, layer, nbatch, seq_len, emit_state):
    t_total = p.shape[0]
    nc = seq_len // CHUNK
    assert nbatch % SEQ_PER_STEP == 0
    has_init = init is not None

    fwd = lambda b, c: (b, c, 0, 0)
    bwd = lambda b, c: (b, nc - 1 - c, 0, 0)
    p4 = p.reshape(nbatch, nc, CHUNK, P_W)
    const_args = [cst[name] for name in _SCAN_CONSTS]
    in_specs = [
        pl.BlockSpec((SEQ_PER_STEP, 1, CHUNK, P_W), fwd),
        pl.BlockSpec((SEQ_PER_STEP, 1, CHUNK, P_W), bwd),
    ] + [_const_spec(a.shape[1:], 2, (layer,)) if name in _PER_LAYER else _const_spec(a.shape, 2)
         for name, a in zip(_SCAN_CONSTS, const_args)]
    args = [p4, p4] + const_args
    state_blk = (SEQ_PER_STEP, N_DIR * HEADS, DK, DV)
    state_map = lambda b, c: (b, 0, 0, 0)
    state_specs = [
        pl.BlockSpec(state_blk, state_map),
        pl.BlockSpec((SEQ_PER_STEP, N_DIR, QK_W, 1), state_map),
        pl.BlockSpec((SEQ_PER_STEP, N_DIR, QK_W), lambda b, c: (b, 0, 0)),
        pl.BlockSpec(state_blk, state_map),
        pl.BlockSpec(state_blk, state_map),
    ]
    if has_init:
        in_specs += state_specs
        args += list(init)
    out_specs = [pl.BlockSpec((SEQ_PER_STEP, 1, CHUNK, 3 * MIX_W), fwd),
                 pl.BlockSpec((SEQ_PER_STEP, 1, CHUNK, 3 * MIX_W), bwd)]
    out_shape = [jax.ShapeDtypeStruct((nbatch, nc, CHUNK, 3 * MIX_W), F32)] * 2
    if emit_state:
        out_specs += state_specs
        out_shape += [
            jax.ShapeDtypeStruct((nbatch, N_DIR * HEADS, DK, DV), F32),
            jax.ShapeDtypeStruct((nbatch, N_DIR, QK_W, 1), F32),
            jax.ShapeDtypeStruct((nbatch, N_DIR, QK_W), F32),
            jax.ShapeDtypeStruct((nbatch, N_DIR * HEADS, DK, DV), F32),
            jax.ShapeDtypeStruct((nbatch, N_DIR * HEADS, DK, DV), F32),
        ]
    n_slots = SEQ_PER_STEP * N_DIR
    res = pl.pallas_call(
        functools.partial(_scan_kernel, layer=layer, has_init=has_init, emit_state=emit_state),
        grid=(nbatch // SEQ_PER_STEP, nc),
        in_specs=in_specs,
        out_specs=out_specs,
        out_shape=out_shape,
        scratch_shapes=[
            pltpu.VMEM((n_slots * HEADS, DK, DV), F32),
            pltpu.VMEM((n_slots, QK_W, LANE), F32),
            pltpu.VMEM((8, QK_W), F32),
            pltpu.VMEM((n_slots * HEADS, DK, DV), F32),
            pltpu.VMEM((n_slots * HEADS, DK, DV), F32),
            pltpu.VMEM((2 * n_slots, CHUNK, QK_W), F32),
        ],
        compiler_params=pltpu.CompilerParams(
            dimension_semantics=("arbitrary", "arbitrary"), vmem_limit_bytes=VMEM_LIMIT),
        name="scan",
    )(*args)
    return [res[0].reshape(t_total, 3 * MIX_W), res[1].reshape(t_total, 3 * MIX_W)] + list(res[2:])


def _grid_position(n_tokens):
    rows = n_tokens // GRID_W
    quarter = D_MODEL // 4
    freqs = jnp.exp(-math.log(10000.0) * jnp.arange(quarter, dtype=F32) / quarter)
    r = jnp.arange(rows, dtype=F32)[:, None] * freqs
    cl = jnp.arange(GRID_W, dtype=F32)[:, None] * freqs
    r_emb = jnp.concatenate([jnp.sin(r), jnp.cos(r)], axis=-1)
    c_emb = jnp.concatenate([jnp.sin(cl), jnp.cos(cl)], axis=-1)
    emb = jnp.concatenate([jnp.broadcast_to(r_emb[:, None], (rows, GRID_W, D_MODEL // 2)),
                           jnp.broadcast_to(c_emb[None], (rows, GRID_W, D_MODEL // 2))], axis=-1)
    return emb.reshape(rows * GRID_W, D_MODEL)


def _regroup_kernel(wt_ref, ws_ref, wg_ref):
    offs = [int(o) for o in np.concatenate([[0], np.cumsum(IN_SIZES)])]
    rows = [wt_ref[offs[i]:offs[i + 1], :] for i in range(len(IN_SIZES))]
    (mq, mk, mv, mo, mif, gq, gk, gv, gr, glr, hq, hf, hv, hg, mg) = rows
    small = jnp.concatenate([mif, glr, jnp.zeros((LANE - N_MIF - N_GLR, mif.shape[1]), F32)], axis=0)
    qs = DK ** -0.5
    lo = 0
    for piece in (mq * qs, mk, mv, gq * qs, gk, gv, hq, hf, hv, small):
        ws_ref[:, lo:lo + piece.shape[0]] = piece.T.astype(BF16)
        lo += piece.shape[0]
    lo = 0
    for piece in (mo, gr, hg, mg):
        wg_ref[:, lo:lo + piece.shape[0]] = piece.T.astype(BF16)
        lo += piece.shape[0]


def _regroup(w_in):
    depth, _, n_in = w_in.shape
    tr = 256
    return pl.pallas_call(
        _regroup_kernel,
        grid=(depth, D_MODEL // tr),
        in_specs=[pl.BlockSpec((None, n_in, tr), lambda l, i: (l, 0, i))],
        out_specs=[pl.BlockSpec((None, tr, P_W), lambda l, i: (l, i, 0)),
                   pl.BlockSpec((None, tr, GATE_W), lambda l, i: (l, i, 0))],
        out_shape=[jax.ShapeDtypeStruct((depth, D_MODEL, P_W), BF16),
                   jax.ShapeDtypeStruct((depth, D_MODEL, GATE_W), BF16)],
        compiler_params=pltpu.CompilerParams(
            dimension_semantics=("arbitrary", "arbitrary"), vmem_limit_bytes=VMEM_LIMIT),
        name="regroup",
    )(jnp.swapaxes(w_in, 1, 2))


def _prepare_weights(w_in, gate_bias, gla_w_up):
    depth = w_in.shape[0]
    w_scan, w_gate = _regroup(w_in)
    bias16 =jnp.broadcast_to(gate_bias.reshape(depth, N_MIF, 1), (depth, N_MIF, CHUNK))
    wup = jnp.zeros((depth, N_DIR, LANE, QK_W), F32)
    for d in range(N_DIR):
        wup = wup.at[:, d, N_MIF + d * G_RANK:N_MIF + (d + 1) * G_RANK, :].set(gla_w_up[:, d])
    wup_hi = wup.astype(BF16)
    wup_lo = (wup - wup_hi.astype(F32)).astype(BF16)
    wup3 = jnp.concatenate([wup_hi, wup_hi, wup_lo], axis=2)
    return w_scan, w_gate, bias16, wup3


def _trunk_layer(x, mod, pos, init, wts, cst, *, layer, nbatch, seq_len, emit_state):
    x = _ffn(x, mod, wts["npre"], wts["npost"], wts["w1"], wts["w2"], layer=layer, half=0, seq_len=seq_len)
    p = _proj(x, mod, wts["npre"], pos, wts["w_scan"], layer=layer, seq_len=seq_len)
    res = _scan(p, cst, init, layer=layer, nbatch=nbatch, seq_len=seq_len, emit_state=emit_state)
    x = _merge(x, mod, wts["npre"], wts["npost"], pos, res[0], res[1], wts["w_gate"], wts["head_norm"],
               wts["w_branch"], wts["w_out"], layer=layer, seq_len=seq_len)
    x = _ffn(x, mod, wts["npre"], wts["npost"], wts["w1"], wts["w2"], layer=layer, half=1, seq_len=seq_len)
    return x, res[2:]


def kernel(x_prompt, x_sample, c, state_mlstm_C, state_mlstm_n, state_mlstm_m, state_gla_S, state_hgrn_S,
           c_ctx, w_ada, b_ada, norm_pre, norm_post, w_ffn_in, w_ffn_out, w_in, mlstm_gate_bias,
           gla_w_up, gla_b, hgrn_gamma, head_norm, w_branch, w_out):
    bp, tp, _ = x_prompt.shape
    bs, ts, _ = x_sample.shape
    depth = w_in.shape[0]

    n_c = 1 + bs
    rows = -(-n_c // 8) * 8
    cvec = jnp.concatenate([c_ctx[None, :], c, jnp.zeros((rows - n_c, D_MODEL), F32)], axis=0)
    mod_all = _ada(cvec, w_ada, b_ada)[:, :n_c].reshape(depth, n_c, 9, D_MODEL)

    w_scan, w_gate, bias16, wup3 = _prepare_weights(w_in, mlstm_gate_bias, gla_w_up)
    wts = dict(
        npre=norm_pre.reshape(depth, 3, 1, D_MODEL), npost=norm_post.reshape(depth, 3, 1, D_MODEL),
        w1=w_ffn_in.astype(BF16), w2=w_ffn_out.astype(BF16),
        w_scan=w_scan, w_gate=w_gate,
        head_norm=head_norm, w_branch=w_branch.astype(BF16), w_out=w_out.astype(BF16),
    )
    cst = dict(_scan_constants(), bias16=bias16, wup3=wup3, glab=gla_b, gamma=hgrn_gamma)

    pos = _grid_position(ts).astype(F32)
    xp = x_prompt.reshape(bp * tp, D_MODEL)
    xs = x_sample.reshape(bs * ts, D_MODEL)
    new_states = []
    for l in range(depth):
        xp, st = _trunk_layer(xp, mod_all[l, 0:1], None, None, wts, cst,
                              layer=l, nbatch=bp, seq_len=tp, emit_state=True)
        new_states.append(st)
        init = (
            state_mlstm_C[:, l].reshape(bs, N_DIR * HEADS, DK, DV),
            state_mlstm_n[:, l].reshape(bs, N_DIR, QK_W, 1),
            jnp.repeat(state_mlstm_m[:, l], DK, axis=-1),
            state_gla_S[:, l].reshape(bs, N_DIR * HEADS, DK, DV),
            state_hgrn_S[:, l].reshape(bs, N_DIR * HEADS, DK, DV),
        )
        xs, _ = _trunk_layer(xs, mod_all[l, 1:], pos, init, wts, cst,
                             layer=l, nbatch=bs, seq_len=ts, emit_state=False)

    blk = (bp, N_DIR, HEADS, DK, DV)
    new_c = jnp.stack([st[0].reshape(blk) for st in new_states], axis=1)
    new_n = jnp.stack([st[1].reshape(bp, N_DIR, HEADS, DK) for st in new_states], axis=1)
    new_m = jnp.stack([st[2][:, :, ::DK] for st in new_states], axis=1)
    new_g = jnp.stack([st[3].reshape(blk) for st in new_states], axis=1)
    new_h = jnp.stack([st[4].reshape(blk) for st in new_states], axis=1)
    return (xp.reshape(bp, tp, D_MODEL), xs.reshape(bs, ts, D_MODEL), new_c, new_n, new_m, new_g, new_h)
```

```python
import functools
import math

import numpy as np
import jax
import jax.numpy as jnp
from jax import lax
from jax.experimental import pallas as pl
from jax.experimental.pallas import tpu as pltpu

F32 = jnp.float32
BF16 = jnp.bfloat16
HIGHEST = lax.Precision.HIGHEST

D_MODEL = 1024
D_FF = 2816
GRID_W = 64
CHUNK = 64
EPS = 1e-6
N_DIR = 2
HEADS = 4
DK = 64
DV = 128
QK_W = HEADS * DK
MIX_W = HEADS * DV
G_RANK = 16
G_TEMP = 16.0
N_MIF = N_DIR * 2 * HEADS
N_GLR = N_DIR * G_RANK
LANE = 128
N_LEVELS = 7
LOG2E = 1.4426950408889634
SEQ_PER_STEP = 4

OFF_MQ, OFF_MK, OFF_MV = 0, 256, 512
OFF_GQ, OFF_GK, OFF_GV = 1024, 1280, 1536
OFF_HQ, OFF_HF, OFF_HV = 2048, 2304, 2816
OFF_SM = 3328
P_W = OFF_SM + LANE
GATE_W = 3 * MIX_W + 3 * D_MODEL

IN_SIZES = (256, 256, 512, 512, N_MIF, 256, 256, 512, 512, N_GLR, 256, 512, 512, 512, 3 * D_MODEL)

FFN_CHUNK = 256
FFN_SUB = 512
PROJ_CHUNK = 3456
VMEM_LIMIT = 56 * 1024 * 1024


def _const_spec(shape, grid_rank, lead=()):
    index = tuple(lead) + (0,) * len(shape)
    block = (None,) * len(lead) + tuple(shape)
    if grid_rank == 1:
        imap = lambda i: index
    else:
        imap = lambda i, j: index
    return pl.BlockSpec(block, imap, pipeline_mode=pl.Buffered(1))


def _rms(x, w):
    ms = jnp.mean(x * x, axis=-1, keepdims=True)
    return x * lax.rsqrt(ms + EPS) * w


def _modulated(x, mod_ref, npre_ref, mi):
    shift = mod_ref[0, mi:mi + 1, :]
    scale = mod_ref[0, mi + 1:mi + 2, :]
    return _rms(x, npre_ref[...]) * (1.0 + scale) + shift


def _row_tile(t_total, seq_len, per_batch_mod, sizes=(512, 256, 128, 64)):
    for tm in sizes:
        if t_total % tm == 0 and (not per_batch_mod or seq_len % tm == 0):
            return tm
    raise ValueError("token count must be a multiple of 64")


def _ada_kernel(c_ref, w_ref, b_ref, o_ref):
    cv = c_ref[...]
    s = cv * jax.nn.sigmoid(cv)
    o_ref[0] = jnp.dot(s, w_ref[0], precision=HIGHEST, preferred_element_type=F32) + b_ref[0]


def _ada(cvec, w_ada, b_ada):
    depth, _, n = w_ada.shape
    tn = 1536
    rows = cvec.shape[0]
    return pl.pallas_call(
        _ada_kernel,
        grid=(depth, n // tn),
        in_specs=[
            pl.BlockSpec((rows, D_MODEL), lambda l, j: (0, 0)),
            pl.BlockSpec((1, D_MODEL, tn), lambda l, j: (l, 0, j)),
            pl.BlockSpec((1, 1, tn), lambda l, j: (l, 0, j)),
        ],
        out_specs=pl.BlockSpec((1, rows, tn), lambda l, j: (l, 0, j)),
        out_shape=jax.ShapeDtypeStruct((depth, rows, n), F32),
        compiler_params=pltpu.CompilerParams(
            dimension_semantics=("arbitrary", "arbitrary"), vmem_limit_bytes=VMEM_LIMIT),
        name="ada",
    )(cvec, w_ada, b_ada.reshape(depth, 1, n))


def _ffn_kernel(x_ref, mod_ref, npre_ref, npost_ref, w1_ref, w2_ref, o_ref, acc_ref, *, mi):
    tm = x_ref.shape[0]
    sub = min(FFN_SUB, tm)
    subs = [slice(i * sub, (i + 1) * sub) for i in range(tm // sub)]
    n_chunks = D_FF // FFN_CHUNK

    def gate_up(hh, k):
        lo = k * FFN_CHUNK
        return (jnp.dot(hh, w1_ref[:, lo:lo + FFN_CHUNK], preferred_element_type=F32),
                jnp.dot(hh, w1_ref[:, D_FF + lo:D_FF + lo + FFN_CHUNK], preferred_element_type=F32))

    hs, gus = [], []
    for r in subs:
        hs.append(_modulated(x_ref[r, :], mod_ref, npre_ref, mi).astype(BF16))
        gus.append(gate_up(hs[-1], 0))
    for k in range(n_chunks):
        for i, r in enumerate(subs):
            g, u = gus[i]
            a = (g * jax.nn.sigmoid(g) * u).astype(BF16)
            part = jnp.dot(a, w2_ref[k * FFN_CHUNK:(k + 1) * FFN_CHUNK, :], preferred_element_type=F32)
            if k + 1 < n_chunks:
                gus[i] = gate_up(hs[i], k + 1)
            if k == 0:
                acc_ref[r, :] = part
            else:
                acc_ref[r, :] += part
    gate = mod_ref[0, mi + 2:mi + 3, :]
    for r in subs:
        o_ref[r, :] = x_ref[r, :] + (0.5 * gate) * _rms(acc_ref[r, :], npost_ref[...])


def _ffn(x, mod, npre, npost, w1, w2, *, layer, half, seq_len):
    mi, ni = 6 * half, 2 * half
    t_total = x.shape[0]
    per_batch = mod.shape[0] > 1
    tm = _row_tile(t_total, seq_len, per_batch, (2 * FFN_SUB, 512, 256, 128, 64))
    mod_map = (lambda i: ((i * tm) // seq_len, 0, 0)) if per_batch else (lambda i: (0, 0, 0))
    return pl.pallas_call(
        functools.partial(_ffn_kernel, mi=mi),
        grid=(t_total // tm,),
        in_specs=[
            pl.BlockSpec((tm, D_MODEL), lambda i: (i, 0)),
            pl.BlockSpec((1, 9, D_MODEL), mod_map),
            _const_spec((1, D_MODEL), 1, (layer, ni)),
            _const_spec((1, D_MODEL), 1, (layer, ni)),
            _const_spec((D_MODEL, 2 * D_FF), 1, (layer, half)),
            _const_spec((D_FF, D_MODEL), 1, (layer, half)),
        ],
        out_specs=pl.BlockSpec((tm, D_MODEL), lambda i: (i, 0)),
        out_shape=jax.ShapeDtypeStruct((t_total, D_MODEL), F32),
        scratch_shapes=[pltpu.VMEM((tm, D_MODEL), F32)],
        compiler_params=pltpu.CompilerParams(
            dimension_semantics=("arbitrary",), vmem_limit_bytes=VMEM_LIMIT),
        name="ffn",
    )(x, mod, npre, npost, w1, w2)


def _proj_kernel(*refs, has_pos):
    if has_pos:
        x_ref, mod_ref, npre_ref, pos_ref, w_ref, o_ref = refs
    else:
        x_ref, mod_ref, npre_ref, w_ref, o_ref = refs
    h = _modulated(x_ref[...], mod_ref, npre_ref, 3)
    if has_pos:
        h = h + pos_ref[...]
    hb = h.astype(BF16)
    for j in range(P_W // PROJ_CHUNK):
        lo = j * PROJ_CHUNK
        o_ref[:, lo:lo + PROJ_CHUNK] = jnp.dot(hb, w_ref[:, lo:lo + PROJ_CHUNK], preferred_element_type=F32)


def _proj(x, mod, npre, pos, w_scan, *, layer, seq_len):
    t_total = x.shape[0]
    per_batch = mod.shape[0] > 1
    tm = _row_tile(t_total, seq_len, per_batch or pos is not None)
    mod_map = (lambda i: ((i * tm) // seq_len, 0, 0)) if per_batch else (lambda i: (0, 0, 0))
    in_specs = [
        pl.BlockSpec((tm, D_MODEL), lambda i: (i, 0)),
        pl.BlockSpec((1, 9, D_MODEL), mod_map),
        _const_spec((1, D_MODEL), 1, (layer, 1)),
    ]
    args = [x, mod, npre]
    if pos is not None:
        tiles_per_seq = seq_len // tm
        in_specs.append(pl.BlockSpec((tm, D_MODEL), lambda i: (i % tiles_per_seq, 0)))
        args.append(pos)
    in_specs.append(_const_spec((D_MODEL, P_W), 1, (layer,)))
    args.append(w_scan)
    return pl.pallas_call(
        functools.partial(_proj_kernel, has_pos=pos is not None),
        grid=(t_total // tm,),
        in_specs=in_specs,
        out_specs=pl.BlockSpec((tm, P_W), lambda i: (i, 0)),
        out_shape=jax.ShapeDtypeStruct((t_total, P_W), F32),
        compiler_params=pltpu.CompilerParams(
            dimension_semantics=("arbitrary",), vmem_limit_bytes=VMEM_LIMIT),
        name="proj",
    )(*args)


def _merge_kernel(*refs, has_pos):
    if has_pos:
        (x_ref, mod_ref, npre_ref, npost_ref, pos_ref, of_ref, ob_ref,
         wg_ref, hn_ref, wb_ref, wo_ref, o_ref) = refs
    else:
        (x_ref, mod_ref, npre_ref, npost_ref, of_ref, ob_ref,
         wg_ref, hn_ref, wb_ref, wo_ref, o_ref) = refs
    x = x_ref[...]
    h = _modulated(x, mod_ref, npre_ref, 3)
    if has_pos:
        h = h + pos_ref[...]
    hb = h.astype(BF16)
    gpre = [jnp.dot(hb, wg_ref[:, n * MIX_W:(n + 1) * MIX_W], preferred_element_type=F32) for n in range(3)]
    mgpre = [jnp.dot(hb, wg_ref[:, 3 * MIX_W + n * D_MODEL:3 * MIX_W + (n + 1) * D_MODEL],
                     preferred_element_type=F32) for n in range(3)]
    ons = []
    for n in range(3):
        o = of_ref[:, n * MIX_W:(n + 1) * MIX_W] + ob_ref[:, n * MIX_W:(n + 1) * MIX_W]
        parts = []
        for hh in range(HEADS):
            oh = o[:, hh * DV:(hh + 1) * DV]
            ms = jnp.mean(oh * oh, axis=-1, keepdims=True)
            parts.append(oh * lax.rsqrt(ms + EPS))
        ons.append(jnp.concatenate(parts, axis=1) * hn_ref[n:n + 1, :])
    brs = []
    for n in range(3):
        sg = jax.nn.sigmoid(gpre[n])
        act = sg if n == 0 else gpre[n] * sg
        brs.append(jnp.dot((act * ons[n]).astype(BF16), wb_ref[n], preferred_element_type=F32))
    merged = None
    for n in range(3):
        term = jax.nn.sigmoid(mgpre[n]) * brs[n]
        merged = term if merged is None else merged + term
    out = jnp.dot(merged.astype(BF16), wo_ref[...], preferred_element_type=F32)
    gate = mod_ref[0, 5:6, :]
    o_ref[...] = x + gate * _rms(out, npost_ref[...])


def _merge(x, mod, npre, npost, pos, o_f, o_b, w_gate, head_norm, w_branch, w_out, *, layer, seq_len):
    t_total = x.shape[0]
    per_batch = mod.shape[0] > 1
    tm = _row_tile(t_total, seq_len, per_batch or pos is not None)
    mod_map = (lambda i: ((i * tm) // seq_len, 0, 0)) if per_batch else (lambda i: (0, 0, 0))
    in_specs = [
        pl.BlockSpec((tm, D_MODEL), lambda i: (i, 0)),
        pl.BlockSpec((1, 9, D_MODEL), mod_map),
        _const_spec((1, D_MODEL), 1, (layer, 1)),
        _const_spec((1, D_MODEL), 1, (layer, 1)),
    ]
    args = [x, mod, npre, npost]
    if pos is not None:
        tiles_per_seq = seq_len // tm
        in_specs.append(pl.BlockSpec((tm, D_MODEL), lambda i: (i % tiles_per_seq, 0)))
        args.append(pos)
    in_specs += [
        pl.BlockSpec((tm, 3 * MIX_W), lambda i: (i, 0)),
        pl.BlockSpec((tm, 3 * MIX_W), lambda i: (i, 0)),
        _const_spec((D_MODEL, GATE_W), 1, (layer,)),
        _const_spec((3, MIX_W), 1, (layer,)),
        _const_spec((3, MIX_W, D_MODEL), 1, (layer,)),
        _const_spec((D_MODEL, D_MODEL), 1, (layer,)),
    ]
    args += [o_f, o_b, w_gate, head_norm, w_branch, w_out]
    return pl.pallas_call(
        functools.partial(_merge_kernel, has_pos=pos is not None),
        grid=(t_total // tm,),
        in_specs=in_specs,
        out_specs=pl.BlockSpec((tm, D_MODEL), lambda i: (i, 0)),
        out_shape=jax.ShapeDtypeStruct((t_total, D_MODEL), F32),
        compiler_params=pltpu.CompilerParams(
            dimension_semantics=("arbitrary",), vmem_limit_bytes=VMEM_LIMIT),
        name="merge",
    )(*args)


def _scan_constants():
    t = np.arange(CHUNK)
    lower = (t[None, :] <= t[:, None])
    lm = np.stack([lower, lower.T]).astype(np.float32)
    lvl = np.zeros((N_DIR, N_LEVELS, CHUNK, CHUNK), np.float32)
    lvl[:, 0] = np.eye(CHUNK)
    for p in range(1, N_LEVELS):
        half = 1 << (p - 1)
        same = (t[:, None] >> p) == (t[None, :] >> p)
        fwd = same & ((t[:, None] & half) != 0) & ((t[None, :] & half) == 0)
        lvl[0, p] = fwd
        lvl[1, p] = fwd.T
    tri = lvl.sum(axis=1)
    assert np.array_equal(tri[0], lower) and np.array_equal(tri[1], lower.T)
    lvl = np.tile(lvl, (1, 1, 1, HEADS))
    tri = np.tile(tri, (1, 1, HEADS))
    pair_blk = np.kron(np.eye(2), np.ones((CHUNK, DK)))
    sel = np.zeros((N_DIR, N_MIF, 2 * QK_W), np.float32)
    for d in range(N_DIR):
        for h in range(HEADS):
            sel[d, d * 2 * HEADS + HEADS + h, h * DK:(h + 1) * DK] = 1.0
            sel[d, d * 2 * HEADS + h, QK_W + h * DK:QK_W + (h + 1) * DK] = 1.0
    col_sel = np.zeros((QK_W, LANE), np.float32)
    for h in range(HEADS):
        col_sel[h * DK:(h + 1) * DK, h] = 1.0
    return dict(
        lm=jnp.asarray(lm, dtype=BF16), lvl=jnp.asarray(lvl), tri=jnp.asarray(tri),
        sel3=jnp.asarray(np.tile(sel, (1, 3, 1)), dtype=BF16),
        hm2=jnp.asarray(pair_blk, dtype=BF16), onesbd=jnp.asarray(col_sel, dtype=BF16),
        cm=jnp.asarray(col_sel),
    )


_TN = (((0,), (0,)), ((), ()))
_NT = (((1,), (1,)), ((), ()))


def _log_sigmoid(x):
    return jnp.minimum(x, 0.0) - jnp.log(1.0 + jnp.exp(-jnp.abs(x)))


def _split3(x):
    hi = x.astype(BF16)
    r1 = x - hi.astype(F32)
    mid = r1.astype(BF16)
    lo = (r1 - mid.astype(F32)).astype(BF16)
    return hi, mid, lo


def _cumsum_rows_issue(x, lm_bf):
    return jnp.dot(lm_bf, jnp.concatenate(_split3(x), axis=1), preferred_element_type=F32)


def _cumsum_rows_finish(c):
    w = c.shape[1] // 3
    return (c[:, :w] + c[:, w:2 * w]) + c[:, 2 * w:]


def _to_column(row):
    col = jnp.transpose(jnp.broadcast_to(row, (8, row.shape[1])))
    return jnp.broadcast_to(col[:, 0:1], (row.shape[1], LANE))


def _block_diag2(a, b):
    return jnp.concatenate([jnp.concatenate([a, jnp.zeros_like(b)], axis=1),
                            jnp.concatenate([jnp.zeros_like(a), b], axis=1)], axis=0)


def _pair_scores(xq, xk, hm2_ref):
    outs = []
    for j in range(2):
        kj = xk[:, j * LANE:(j + 1) * LANE]
        kbd = jnp.concatenate([kj, kj], axis=0) * hm2_ref[...]
        outs.append(lax.dot_general(xq[:, j * LANE:(j + 1) * LANE], kbd, _NT, preferred_element_type=F32))
    return jnp.concatenate(outs, axis=1)


def _row_bcast(ref, r):
    return jnp.broadcast_to(ref[r:r + 1, :], (8, ref.shape[1]))


def _level_operand(p, d, g2g, qgrp, kgrp, gbuf, iota8):
    blk, half = 1 << p, 1 << (p - 1)
    mid = half if d == 0 else half - 1
    out = []
    for j in range(CHUNK // 8):
        r0 = 8 * j
        if blk >= 16:
            ref = _row_bcast(gbuf, (r0 // blk) * blk + mid)
            query = ((r0 & half) != 0) == (d == 0)
            dlt = g2g[j] - ref if query else ref - g2g[j]
            w = qgrp[j] if query else kgrp[j]
        else:
            ref = _row_bcast(gbuf, r0 + mid)
            for m in range(1, 8 // blk):
                ref = jnp.where(iota8 >= m * blk, _row_bcast(gbuf, r0 + m * blk + mid), ref)
            upper = (iota8 & half) != 0
            dlt = (g2g[j] - ref) * jnp.where(upper == (d == 0), 1.0, -1.0)
            w = jnp.where(upper, qgrp[j], kgrp[j]) if d == 0 else jnp.where(upper, kgrp[j], qgrp[j])
        out.append(w * jnp.exp2(dlt))
    return jnp.concatenate(out, axis=0).astype(BF16)


def _gated_dir(q, k, v, la, s_ref, gbuf, slot, d, cst, o_ref, o_lo):
    csum = _cumsum_rows_issue(la, cst["lm"][d])
    vb = v.astype(BF16)
    s_old = [s_ref[slot * HEADS + h] for h in range(HEADS)]
    yield
    g2 = _cumsum_rows_finish(csum) * LOG2E
    last = CHUNK - 1 if d == 0 else 0
    g2_last = g2[last:last + 1, :]
    gbuf[...] = g2
    iota8 = lax.broadcasted_iota(jnp.int32, (8, QK_W), 0)
    groups = [slice(8 * j, 8 * j + 8) for j in range(CHUNK // 8)]
    g2g, qgrp, kgrp = [g2[r] for r in groups], [q[r] for r in groups], [k[r] for r in groups]
    scores = [_pair_scores(q.astype(BF16), k.astype(BF16), cst["hm2"])]
    for p in range(1, N_LEVELS):
        x = _level_operand(p, d, g2g, qgrp, kgrp, gbuf, iota8)
        scores.append(_pair_scores(x, x, cst["hm2"]))
    qg = (q * jnp.exp2(g2)).astype(BF16)
    kg = (k * jnp.exp2(g2_last - g2)).astype(BF16)
    dec = _to_column(jnp.exp2(g2_last))
    dss = [lax.dot_general(kg[:, j * LANE:(j + 1) * LANE], vb[:, 2 * j * DV:(2 * j + 2) * DV], _TN,
                           preferred_element_type=F32) for j in range(2)]
    yield
    a = scores[0] * cst["lvl"][d, 0]
    for p in range(1, N_LEVELS):
        a = a + scores[p] * cst["lvl"][d, p]
    ab = a.astype(BF16)
    outs = []
    for j in range(2):
        rhs = jnp.concatenate([
            _block_diag2(vb[:, 2 * j * DV:(2 * j + 1) * DV], vb[:, (2 * j + 1) * DV:(2 * j + 2) * DV]),
            _block_diag2(s_old[2 * j].astype(BF16), s_old[2 * j + 1].astype(BF16)),
        ], axis=0)
        lhs = jnp.concatenate([ab[:, j * LANE:(j + 1) * LANE], qg[:, j * LANE:(j + 1) * LANE]], axis=1)
        outs.append(jnp.dot(lhs, rhs, preferred_element_type=F32))
    yield
    for j in range(2):
        for h2 in range(2):
            h = 2 * j + h2
            s_ref[slot * HEADS + h] = (dec[h * DK:(h + 1) * DK, :] * s_old[h]
                                       + dss[j][h2 * DK:(h2 + 1) * DK, h2 * DV:(h2 + 1) * DV])
    o_ref[:, o_lo:o_lo + MIX_W] = jnp.concatenate(outs, axis=1)


def _mlstm_dir(q, k, v, g16, c_ref, n_ref, m_ref, slot, d, cst, o_ref):
    neg_inf = float("-inf")
    pre = g16 + cst["bias16"][...]
    cum3 = jnp.dot(jnp.concatenate(_split3(_log_sigmoid(pre)), axis=0), cst["lm"][1 - d],
                   preferred_element_type=F32)
    qk = _pair_scores(q.astype(BF16), k.astype(BF16), cst["hm2"])
    vb = v.astype(BF16)
    c_old = [c_ref[slot * HEADS + h] for h in range(HEADS)]
    n_old = n_ref[slot]
    yield
    cum = (cum3[0:16] + cum3[16:32]) + cum3[32:48]
    is_f = (lax.broadcasted_iota(jnp.int32, (N_MIF, CHUNK), 0) & HEADS) != 0
    y = jnp.where(is_f, cum, pre)
    be = lax.dot_general(jnp.concatenate(_split3(y), axis=0), cst["sel3"][d], _TN,
                         preferred_element_type=F32)
    yield
    b_exp, li_exp = be[:, :QK_W], be[:, QK_W:]
    r0 = d * 2 * HEADS
    li_row = jnp.concatenate([y[r0 + h:r0 + h + 1, :] for h in range(HEADS)], axis=1)
    b_row = jnp.concatenate([y[r0 + HEADS + h:r0 + HEADS + h + 1, :] for h in range(HEADS)], axis=1)
    m_prev = m_ref[slot:slot + 1, :]
    inter = b_exp + m_prev
    dmat = jnp.where(cst["tri"][d] > 0.0, b_exp - b_row + li_row, neg_inf)
    lane_head = lax.broadcasted_iota(jnp.int32, (CHUNK, QK_W), 1) // DK
    mt = inter
    for h in range(HEADS):
        sel = lane_head == h
        rm = jnp.max(jnp.where(sel, dmat, neg_inf), axis=-1, keepdims=True)
        mt = jnp.where(sel, jnp.maximum(inter, rm), mt)
    sc = (qk * jnp.exp(dmat - mt)).astype(BF16)
    qe = (q * jnp.exp(inter - mt)).astype(BF16)
    last = CHUNK - 1 if d == 0 else 0
    b_last = b_exp[last:last + 1, :]
    lw = b_last - b_exp + li_exp
    m_new = jnp.maximum(b_last + m_prev, jnp.max(lw, axis=0, keepdims=True))
    kw = (k * jnp.exp(lw - m_new)).astype(BF16)
    dec_col = _to_column(jnp.exp(b_last + m_prev - m_new))
    ones_blk = jnp.ones((CHUNK, LANE), BF16)
    nds, dcs = [], []
    for j in range(2):
        rows = slice(j * LANE, (j + 1) * LANE)
        v_pair = vb[:, 2 * j * DV:(2 * j + 2) * DV]
        top = jnp.concatenate([_block_diag2(v_pair[:, :DV], v_pair[:, DV:]), cst["onesbd"][rows, :]], axis=1)
        bot = jnp.concatenate([_block_diag2(c_old[2 * j].astype(BF16), c_old[2 * j + 1].astype(BF16)),
                               n_old[rows, :].astype(BF16)], axis=1)
        lhs = jnp.concatenate([sc[:, rows], qe[:, rows]], axis=1)
        nds.append(jnp.dot(lhs, jnp.concatenate([top, bot], axis=0), preferred_element_type=F32))
        dcs.append(lax.dot_general(kw[:, rows], jnp.concatenate([v_pair, ones_blk], axis=1), _TN,
                                   preferred_element_type=F32))
    yield
    outs = []
    for j in range(2):
        rows = slice(j * LANE, (j + 1) * LANE)
        nd, dc = nds[j], dcs[j]
        for h2 in range(2):
            h = 2 * j + h2
            den = nd[:, 2 * DV + h:2 * DV + h + 1]
            mth = mt[:, h * DK:h * DK + 1]
            outs.append(nd[:, h2 * DV:(h2 + 1) * DV] / jnp.maximum(jnp.abs(den), jnp.exp(-mth)))
            c_ref[slot * HEADS + h] = (dec_col[h * DK:(h + 1) * DK, :] * c_old[h]
                                       + dc[h2 * DK:(h2 + 1) * DK, h2 * DV:(h2 + 1) * DV])
        n_ref[slot, rows, :] = dec_col[rows, :] * n_old[rows, :] + cst["cm"][rows, :] * dc[:, 2 * DV:]
    m_ref[slot:slot + 1, :] = m_new
    o_ref[:, 0:MIX_W] = jnp.concatenate(outs, axis=1)


_DONE = object()
_PER_LAYER = ("bias16", "wup3", "glab")
_SCAN_CONSTS = ("lm", "lvl", "tri", "sel3", "hm2", "onesbd", "cm", "bias16", "wup3", "glab", "gamma")


def _scan_kernel(*refs, layer, has_init, emit_state):
    pf_ref, pb_ref = refs[:2]
    pos = 2
    cst = dict(zip(_SCAN_CONSTS, refs[pos:pos + len(_SCAN_CONSTS)]))
    pos += len(_SCAN_CONSTS)
    if has_init:
        c0_ref, n0_ref, m0_ref, g0_ref, h0_ref = refs[pos:pos + 5]
        pos += 5
    of_ref, ob_ref = refs[pos:pos + 2]
    pos += 2
    if emit_state:
        cout_ref, nout_ref, mout_ref, gout_ref, hout_ref = refs[pos:pos + 5]
        pos += 5
    c_scr, n_scr, m_scr, g_scr, h_scr, gbuf = refs[pos:pos + 6]

    ci = pl.program_id(1)

    @pl.when(ci == 0)
    def _():
        if has_init:
            for q in range(SEQ_PER_STEP):
                lo, hi = q * N_DIR * HEADS, (q + 1) * N_DIR * HEADS
                c_scr[lo:hi] = c0_ref[q]
                g_scr[lo:hi] = g0_ref[q]
                h_scr[lo:hi] = h0_ref[q]
                for d in range(N_DIR):
                    n_scr[q * N_DIR + d] = cst["cm"][...] * n0_ref[q, d]
                m_scr[q * N_DIR:(q + 1) * N_DIR, :] = m0_ref[q]
        else:
            c_scr[...] = jnp.zeros(c_scr.shape, F32)
            g_scr[...] = jnp.zeros(g_scr.shape, F32)
            h_scr[...] = jnp.zeros(h_scr.shape, F32)
            n_scr[...] = jnp.zeros(n_scr.shape, F32)
            m_scr[...] = jnp.zeros(m_scr.shape, F32)

    gam = cst["gamma"][...]
    ge = jnp.exp(gam - jnp.max(gam, axis=0, keepdims=True))
    pg = ge / jnp.sum(ge, axis=0, keepdims=True)
    cs = pg[0:1, :]
    for j in range(1, layer + 1):
        cs = cs + pg[j:j + 1, :]
    lb = cs - pg[0:1, :]

    units = []
    for q in range(SEQ_PER_STEP):
        for d, (p4_ref, o4_ref) in enumerate(((pf_ref, of_ref), (pb_ref, ob_ref))):
            slot = q * N_DIR + d
            p_ref, o_ref = p4_ref.at[q, 0], o4_ref.at[q, 0]
            g16 = jnp.transpose(p_ref[:, OFF_SM:OFF_SM + LANE])[0:N_MIF, :]
            units.append(_mlstm_dir(
                p_ref[:, OFF_MQ:OFF_MQ + QK_W], p_ref[:, OFF_MK:OFF_MK + QK_W], p_ref[:, OFF_MV:OFF_MV + MIX_W],
                g16, c_scr, n_scr, m_scr, slot, d, cst, o_ref))
            sm = p_ref[:, OFF_SM:OFF_SM + LANE]
            sm_hi = sm.astype(BF16)
            sm_lo = (sm - sm_hi.astype(F32)).astype(BF16)
            pre = jnp.dot(jnp.concatenate([sm_hi, sm_lo, sm_hi], axis=1), cst["wup3"][d],
                          preferred_element_type=F32) + cst["glab"][d:d + 1, :]
            units.append(_gated_dir(
                p_ref[:, OFF_GQ:OFF_GQ + QK_W], p_ref[:, OFF_GK:OFF_GK + QK_W], p_ref[:, OFF_GV:OFF_GV + MIX_W],
                _log_sigmoid(pre) * (1.0 / G_TEMP), g_scr, gbuf.at[2 * slot], slot, d, cst, o_ref, MIX_W))
            zz = p_ref[:, OFF_HF + d * QK_W:OFF_HF + (d + 1) * QK_W]
            ez = jnp.exp(-jnp.abs(zz))
            rz = 1.0 / (1.0 + ez)
            pos_z = zz >= 0.0
            sig = jnp.where(pos_z, rz, ez * rz)
            nsig = jnp.where(pos_z, ez * rz, rz)
            hv = p_ref[:, OFF_HV:OFF_HV + MIX_W]
            units.append(_gated_dir(
                p_ref[:, OFF_HQ:OFF_HQ + QK_W], (1.0 - lb) * nsig, hv * jax.nn.sigmoid(hv),
                jnp.log(lb + (1.0 - lb) * sig), h_scr, gbuf.at[2 * slot + 1], slot, d, cst, o_ref, 2 * MIX_W))
    while units:
        units = [u for u in units if next(u, _DONE) is not _DONE]

    if emit_state:
        @pl.when(ci == pl.num_programs(1) - 1)
        def _():
            for q in range(SEQ_PER_STEP):
                lo, hi = q * N_DIR * HEADS, (q + 1) * N_DIR * HEADS
                cout_ref[q] = c_scr[lo:hi]
                gout_ref[q] = g_scr[lo:hi]
                hout_ref[q] = h_scr[lo:hi]
                for d in range(N_DIR):
                    nout_ref[q, d] = jnp.sum(n_scr[q * N_DIR + d], axis=-1, keepdims=True)
                mout_ref[q] = m_scr[q * N_DIR:(q + 1) * N_DIR, :]


def _scan(p, cst, init, *, layer, nbatch, seq_len, emit_state):
    t_total = p.shape[0]
    nc = seq_len // CHUNK
    assert nbatch % SEQ_PER_STEP == 0
    has_init = init is not None

    fwd = lambda b, c: (b, c, 0, 0)
    bwd = lambda b, c: (b, nc - 1 - c, 0, 0)
    p4 = p.reshape(nbatch, nc, CHUNK, P_W)
    const_args = [cst[name] for name in _SCAN_CONSTS]
    in_specs = [
        pl.BlockSpec((SEQ_PER_STEP, 1, CHUNK, P_W), fwd),
        pl.BlockSpec((SEQ_PER_STEP, 1, CHUNK, P_W), bwd),
    ] + [_const_spec(a.shape[1:], 2, (layer,)) if name in _PER_LAYER else _const_spec(a.shape, 2)
         for name, a in zip(_SCAN_CONSTS, const_args)]
    args = [p4, p4] + const_args
    state_blk = (SEQ_PER_STEP, N_DIR * HEADS, DK, DV)
    state_map = lambda b, c: (b, 0, 0, 0)
    state_specs = [
        pl.BlockSpec(state_blk, state_map),
        pl.BlockSpec((SEQ_PER_STEP, N_DIR, QK_W, 1), state_map),
        pl.BlockSpec((SEQ_PER_STEP, N_DIR, QK_W), lambda b, c: (b, 0, 0)),
        pl.BlockSpec(state_blk, state_map),
        pl.BlockSpec(state_blk, state_map),
    ]
    if has_init:
        in_specs += state_specs
        args += list(init)
    out_specs = [pl.BlockSpec((SEQ_PER_STEP, 1, CHUNK, 3 * MIX_W), fwd),
                 pl.BlockSpec((SEQ_PER_STEP, 1, CHUNK, 3 * MIX_W), bwd)]
    out_shape = [jax.ShapeDtypeStruct((nbatch, nc, CHUNK, 3 * MIX_W), F32)] * 2
    if emit_state:
        out_specs += state_specs
        out_shape += [
            jax.ShapeDtypeStruct((nbatch, N_DIR * HEADS, DK, DV), F32),
            jax.ShapeDtypeStruct((nbatch, N_DIR, QK_W, 1), F32),
            jax.ShapeDtypeStruct((nbatch, N_DIR, QK_W), F32),
            jax.ShapeDtypeStruct((nbatch, N_DIR * HEADS, DK, DV), F32),
            jax.ShapeDtypeStruct((nbatch, N_DIR * HEADS, DK, DV), F32),
        ]
    n_slots = SEQ_PER_STEP * N_DIR
    res = pl.pallas_call(
        functools.partial(_scan_kernel, layer=layer, has_init=has_init, emit_state=emit_state),
        grid=(nbatch // SEQ_PER_STEP, nc),
        in_specs=in_specs,
        out_specs=out_specs,
        out_shape=out_shape,
        scratch_shapes=[
            pltpu.VMEM((n_slots * HEADS, DK, DV), F32),
            pltpu.VMEM((n_slots, QK_W, LANE), F32),
            pltpu.VMEM((8, QK_W), F32),
            pltpu.VMEM((n_slots * HEADS, DK, DV), F32),
            pltpu.VMEM((n_slots * HEADS, DK, DV), F32),
            pltpu.VMEM((2 * n_slots, CHUNK, QK_W), F32),
        ],
        compiler_params=pltpu.CompilerParams(
            dimension_semantics=("arbitrary", "arbitrary"), vmem_limit_bytes=VMEM_LIMIT),
        name="scan",
    )(*args)
    return [res[0].reshape(t_total, 3 * MIX_W), res[1].reshape(t_total, 3 * MIX_W)] + list(res[2:])


def _grid_position(n_tokens):
    rows = n_tokens // GRID_W
    quarter = D_MODEL // 4
    freqs = jnp.exp(-math.log(10000.0) * jnp.arange(quarter, dtype=F32) / quarter)
    r = jnp.arange(rows, dtype=F32)[:, None] * freqs
    cl = jnp.arange(GRID_W, dtype=F32)[:, None] * freqs
    r_emb = jnp.concatenate([jnp.sin(r), jnp.cos(r)], axis=-1)
    c_emb = jnp.concatenate([jnp.sin(cl), jnp.cos(cl)], axis=-1)
    emb = jnp.concatenate([jnp.broadcast_to(r_emb[:, None], (rows, GRID_W, D_MODEL // 2)),
                           jnp.broadcast_to(c_emb[None], (rows, GRID_W, D_MODEL // 2))], axis=-1)
    return emb.reshape(rows * GRID_W, D_MODEL)


def _regroup_kernel(wt_ref, ws_ref, wg_ref):
    offs = [int(o) for o in np.concatenate([[0], np.cumsum(IN_SIZES)])]
    rows = [wt_ref[offs[i]:offs[i + 1], :] for i in range(len(IN_SIZES))]
    (mq, mk, mv, mo, mif, gq, gk, gv, gr, glr, hq, hf, hv, hg, mg) = rows
    small = jnp.concatenate([mif, glr, jnp.zeros((LANE - N_MIF - N_GLR, mif.shape[1]), F32)], axis=0)
    qs = DK ** -0.5
    lo = 0
    for piece in (mq * qs, mk, mv, gq * qs, gk, gv, hq, hf, hv, small):
        ws_ref[:, lo:lo + piece.shape[0]] = piece.T.astype(BF16)
        lo += piece.shape[0]
    lo = 0
    for piece in (mo, gr, hg, mg):
        wg_ref[:, lo:lo + piece.shape[0]] = piece.T.astype(BF16)
        lo += piece.shape[0]


def _regroup(w_in):
    depth, _, n_in = w_in.shape
    tr = 256
    return pl.pallas_call(
        _regroup_kernel,
        grid=(depth, D_MODEL // tr),
        in_specs=[pl.BlockSpec((None, n_in, tr), lambda l, i: (l, 0, i))],
        out_specs=[pl.BlockSpec((None, tr, P_W), lambda l, i: (l, i, 0)),
                   pl.BlockSpec((None, tr, GATE_W), lambda l, i: (l, i, 0))],
        out_shape=[jax.ShapeDtypeStruct((depth, D_MODEL, P_W), BF16),
                   jax.ShapeDtypeStruct((depth, D_MODEL, GATE_W), BF16)],
        compiler_params=pltpu.CompilerParams(
            dimension_semantics=("arbitrary", "arbitrary"), vmem_limit_bytes=VMEM_LIMIT),
        name="regroup",
    )(jnp.swapaxes(w_in, 1, 2))


def _prepare_weights(w_in, gate_bias, gla_w_up):
    depth = w_in.shape[0]
    w_scan, w_gate = _regroup(w_in)
    bias16 =jnp.broadcast_to(gate_bias.reshape(depth, N_MIF, 1), (depth, N_MIF, CHUNK))
    wup = jnp.zeros((depth, N_DIR, LANE, QK_W), F32)
    for d in range(N_DIR):
        wup = wup.at[:, d, N_MIF + d * G_RANK:N_MIF + (d + 1) * G_RANK, :].set(gla_w_up[:, d])
    wup_hi = wup.astype(BF16)
    wup_lo = (wup - wup_hi.astype(F32)).astype(BF16)
    wup3 = jnp.concatenate([wup_hi, wup_hi, wup_lo], axis=2)
    return w_scan, w_gate, bias16, wup3


def _trunk_layer(x, mod, pos, init, wts, cst, *, layer, nbatch, seq_len, emit_state):
    x = _ffn(x, mod, wts["npre"], wts["npost"], wts["w1"], wts["w2"], layer=layer, half=0, seq_len=seq_len)
    p = _proj(x, mod, wts["npre"], pos, wts["w_scan"], layer=layer, seq_len=seq_len)
    res = _scan(p, cst, init, layer=layer, nbatch=nbatch, seq_len=seq_len, emit_state=emit_state)
    x = _merge(x, mod, wts["npre"], wts["npost"], pos, res[0], res[1], wts["w_gate"], wts["head_norm"],
               wts["w_branch"], wts["w_out"], layer=layer, seq_len=seq_len)
    x = _ffn(x, mod, wts["npre"], wts["npost"], wts["w1"], wts["w2"], layer=layer, half=1, seq_len=seq_len)
    return x, res[2:]


def kernel(x_prompt, x_sample, c, state_mlstm_C, state_mlstm_n, state_mlstm_m, state_gla_S, state_hgrn_S,
           c_ctx, w_ada, b_ada, norm_pre, norm_post, w_ffn_in, w_ffn_out, w_in, mlstm_gate_bias,
           gla_w_up, gla_b, hgrn_gamma, head_norm, w_branch, w_out):
    bp, tp, _ = x_prompt.shape
    bs, ts, _ = x_sample.shape
    depth = w_in.shape[0]

    n_c = 1 + bs
    rows = -(-n_c // 8) * 8
    cvec = jnp.concatenate([c_ctx[None, :], c, jnp.zeros((rows - n_c, D_MODEL), F32)], axis=0)
    mod_all = _ada(cvec, w_ada, b_ada)[:, :n_c].reshape(depth, n_c, 9, D_MODEL)

    w_scan, w_gate, bias16, wup3 = _prepare_weights(w_in, mlstm_gate_bias, gla_w_up)
    wts = dict(
        npre=norm_pre.reshape(depth, 3, 1, D_MODEL), npost=norm_post.reshape(depth, 3, 1, D_MODEL),
        w1=w_ffn_in.astype(BF16), w2=w_ffn_out.astype(BF16),
        w_scan=w_scan, w_gate=w_gate,
        head_norm=head_norm, w_branch=w_branch.astype(BF16), w_out=w_out.astype(BF16),
    )
    cst = dict(_scan_constants(), bias16=bias16, wup3=wup3, glab=gla_b, gamma=hgrn_gamma)

    pos = _grid_position(ts).astype(F32)
    xp = x_prompt.reshape(bp * tp, D_MODEL)
    xs = x_sample.reshape(bs * ts, D_MODEL)
    new_states = []
    for l in range(depth):
        xp, st = _trunk_layer(xp, mod_all[l, 0:1], None, None, wts, cst,
                              layer=l, nbatch=bp, seq_len=tp, emit_state=True)
        new_states.append(st)
        init = (
            state_mlstm_C[:, l].reshape(bs, N_DIR * HEADS, DK, DV),
            state_mlstm_n[:, l].reshape(bs, N_DIR, QK_W, 1),
            jnp.repeat(state_mlstm_m[:, l], DK, axis=-1),
            state_gla_S[:, l].reshape(bs, N_DIR * HEADS, DK, DV),
            state_hgrn_S[:, l].reshape(bs, N_DIR * HEADS, DK, DV),
        )
        xs, _ = _trunk_layer(xs, mod_all[l, 1:], pos, init, wts, cst,
                             layer=l, nbatch=bs, seq_len=ts, emit_state=False)

    blk = (bp, N_DIR, HEADS, DK, DV)
    new_c = jnp.stack([st[0].reshape(blk) for st in new_states], axis=1)
    new_n = jnp.stack([st[1].reshape(bp, N_DIR, HEADS, DK) for st in new_states], axis=1)
    new_m = jnp.stack([st[2][:, :, ::DK] for st in new_states], axis=1)
    new_g = jnp.stack([st[3].reshape(blk) for st in new_states], axis=1)
    new_h = jnp.stack([st[4].reshape(blk) for st in new_states], axis=1)
    return (xp.reshape(bp, tp, D_MODEL), xs.reshape(bs, ts, D_MODEL), new_c, new_n, new_m, new_g, new_h)
```

```python
import functools
import math

import numpy as np
import jax
import jax.numpy as jnp
from jax import lax
from jax.experimental import pallas as pl
from jax.experimental.pallas import tpu as pltpu

F32 = jnp.float32
BF16 = jnp.bfloat16
HIGHEST = lax.Precision.HIGHEST

D_MODEL = 1024
D_FF = 2816
GRID_W = 64
CHUNK = 64
EPS = 1e-6
N_DIR = 2
HEADS = 4
DK = 64
DV = 128
QK_W = HEADS * DK
MIX_W = HEADS * DV
G_RANK = 16
G_TEMP = 16.0
N_MIF = N_DIR * 2 * HEADS
N_GLR = N_DIR * G_RANK
LANE = 128
N_LEVELS = 7
LOG2E = 1.4426950408889634
SEQ_PER_STEP = 4

OFF_MQ, OFF_MK, OFF_MV = 0, 256, 512
OFF_GQ, OFF_GK, OFF_GV = 1024, 1280, 1536
OFF_HQ, OFF_HF, OFF_HV = 2048, 2304, 2816
OFF_SM = 3328
P_W = OFF_SM + LANE
GATE_W = 3 * MIX_W + 3 * D_MODEL

IN_SIZES = (256, 256, 512, 512, N_MIF, 256, 256, 512, 512, N_GLR, 256, 512, 512, 512, 3 * D_MODEL)

FFN_CHUNK = 256
PROJ_CHUNK = 3456
VMEM_LIMIT = 56 * 1024 * 1024


def _const_spec(shape, grid_rank, lead=()):
    index = tuple(lead) + (0,) * len(shape)
    block = (None,) * len(lead) + tuple(shape)
    if grid_rank == 1:
        imap = lambda i: index
    else:
        imap = lambda i, j: index
    return pl.BlockSpec(block, imap, pipeline_mode=pl.Buffered(1))


def _rms(x, w):
    ms = jnp.mean(x * x, axis=-1, keepdims=True)
    return x * lax.rsqrt(ms + EPS) * w


def _modulated(x, mod_ref, npre_ref, mi):
    shift = mod_ref[0, mi:mi + 1, :]
    scale = mod_ref[0, mi + 1:mi + 2, :]
    return _rms(x, npre_ref[...]) * (1.0 + scale) + shift


def _row_tile(t_total, seq_len, per_batch_mod):
    for tm in (512, 256, 128, 64):
        if t_total % tm == 0 and (not per_batch_mod or seq_len % tm == 0):
            return tm
    raise ValueError("token count must be a multiple of 64")


def _ada_kernel(c_ref, w_ref, b_ref, o_ref):
    cv = c_ref[...]
    s = cv * jax.nn.sigmoid(cv)
    o_ref[0] = jnp.dot(s, w_ref[0], precision=HIGHEST, preferred_element_type=F32) + b_ref[0]


def _ada(cvec, w_ada, b_ada):
    depth, _, n = w_ada.shape
    tn = 1536
    rows = cvec.shape[0]
    return pl.pallas_call(
        _ada_kernel,
        grid=(depth, n // tn),
        in_specs=[
            pl.BlockSpec((rows, D_MODEL), lambda l, j: (0, 0)),
            pl.BlockSpec((1, D_MODEL, tn), lambda l, j: (l, 0, j)),
            pl.BlockSpec((1, 1, tn), lambda l, j: (l, 0, j)),
        ],
        out_specs=pl.BlockSpec((1, rows, tn), lambda l, j: (l, 0, j)),
        out_shape=jax.ShapeDtypeStruct((depth, rows, n), F32),
        compiler_params=pltpu.CompilerParams(
            dimension_semantics=("arbitrary", "arbitrary"), vmem_limit_bytes=VMEM_LIMIT),
        name="ada",
    )(cvec, w_ada, b_ada.reshape(depth, 1, n))


def _ffn_kernel(x_ref, mod_ref, npre_ref, npost_ref, w1_ref, w2_ref, o_ref, acc_ref, *, mi):
    x = x_ref[...]
    h = _modulated(x, mod_ref, npre_ref, mi).astype(BF16)
    for k in range(D_FF // FFN_CHUNK):
        lo = k * FFN_CHUNK
        g = jnp.dot(h, w1_ref[:, lo:lo + FFN_CHUNK], preferred_element_type=F32)
        u = jnp.dot(h, w1_ref[:, D_FF + lo:D_FF + lo + FFN_CHUNK], preferred_element_type=F32)
        a = (g * jax.nn.sigmoid(g) * u).astype(BF16)
        part = jnp.dot(a, w2_ref[lo:lo + FFN_CHUNK, :], preferred_element_type=F32)
        if k == 0:
            acc_ref[...] = part
        else:
            acc_ref[...] += part
    gate = mod_ref[0, mi + 2:mi + 3, :]
    o_ref[...] = x + (0.5 * gate) * _rms(acc_ref[...], npost_ref[...])


def _ffn(x, mod, npre, npost, w1, w2, *, layer, half, seq_len):
    mi, ni = 6 * half, 2 * half
    t_total = x.shape[0]
    per_batch = mod.shape[0] > 1
    tm = _row_tile(t_total, seq_len, per_batch)
    mod_map = (lambda i: ((i * tm) // seq_len, 0, 0)) if per_batch else (lambda i: (0, 0, 0))
    return pl.pallas_call(
        functools.partial(_ffn_kernel, mi=mi),
        grid=(t_total // tm,),
        in_specs=[
            pl.BlockSpec((tm, D_MODEL), lambda i: (i, 0)),
            pl.BlockSpec((1, 9, D_MODEL), mod_map),
            _const_spec((1, D_MODEL), 1, (layer, ni)),
            _const_spec((1, D_MODEL), 1, (layer, ni)),
            _const_spec((D_MODEL, 2 * D_FF), 1, (layer, half)),
            _const_spec((D_FF, D_MODEL), 1, (layer, half)),
        ],
        out_specs=pl.BlockSpec((tm, D_MODEL), lambda i: (i, 0)),
        out_shape=jax.ShapeDtypeStruct((t_total, D_MODEL), F32),
        scratch_shapes=[pltpu.VMEM((tm, D_MODEL), F32)],
        compiler_params=pltpu.CompilerParams(
            dimension_semantics=("arbitrary",), vmem_limit_bytes=VMEM_LIMIT),
        name="ffn",
    )(x, mod, npre, npost, w1, w2)


def _proj_kernel(*refs, has_pos):
    if has_pos:
        x_ref, mod_ref, npre_ref, pos_ref, w_ref, o_ref = refs
    else:
        x_ref, mod_ref, npre_ref, w_ref, o_ref = refs
    h = _modulated(x_ref[...], mod_ref, npre_ref, 3)
    if has_pos:
        h = h + pos_ref[...]
    hb = h.astype(BF16)
    for j in range(P_W // PROJ_CHUNK):
        lo = j * PROJ_CHUNK
        o_ref[:, lo:lo + PROJ_CHUNK] = jnp.dot(hb, w_ref[:, lo:lo + PROJ_CHUNK], preferred_element_type=F32)


def _proj(x, mod, npre, pos, w_scan, *, layer, seq_len):
    t_total = x.shape[0]
    per_batch = mod.shape[0] > 1
    tm = _row_tile(t_total, seq_len, per_batch or pos is not None)
    mod_map = (lambda i: ((i * tm) // seq_len, 0, 0)) if per_batch else (lambda i: (0, 0, 0))
    in_specs = [
        pl.BlockSpec((tm, D_MODEL), lambda i: (i, 0)),
        pl.BlockSpec((1, 9, D_MODEL), mod_map),
        _const_spec((1, D_MODEL), 1, (layer, 1)),
    ]
    args = [x, mod, npre]
    if pos is not None:
        tiles_per_seq = seq_len // tm
        in_specs.append(pl.BlockSpec((tm, D_MODEL), lambda i: (i % tiles_per_seq, 0)))
        args.append(pos)
    in_specs.append(_const_spec((D_MODEL, P_W), 1, (layer,)))
    args.append(w_scan)
    return pl.pallas_call(
        functools.partial(_proj_kernel, has_pos=pos is not None),
        grid=(t_total // tm,),
        in_specs=in_specs,
        out_specs=pl.BlockSpec((tm, P_W), lambda i: (i, 0)),
        out_shape=jax.ShapeDtypeStruct((t_total, P_W), F32),
        compiler_params=pltpu.CompilerParams(
            dimension_semantics=("arbitrary",), vmem_limit_bytes=VMEM_LIMIT),
        name="proj",
    )(*args)


def _merge_kernel(*refs, has_pos):
    if has_pos:
        (x_ref, mod_ref, npre_ref, npost_ref, pos_ref, of_ref, ob_ref,
         wg_ref, hn_ref, wb_ref, wo_ref, o_ref) = refs
    else:
        (x_ref, mod_ref, npre_ref, npost_ref, of_ref, ob_ref,
         wg_ref, hn_ref, wb_ref, wo_ref, o_ref) = refs
    x = x_ref[...]
    h = _modulated(x, mod_ref, npre_ref, 3)
    if has_pos:
        h = h + pos_ref[...]
    hb = h.astype(BF16)
    gpre = [jnp.dot(hb, wg_ref[:, n * MIX_W:(n + 1) * MIX_W], preferred_element_type=F32) for n in range(3)]
    mgpre = [jnp.dot(hb, wg_ref[:, 3 * MIX_W + n * D_MODEL:3 * MIX_W + (n + 1) * D_MODEL],
                     preferred_element_type=F32) for n in range(3)]
    ons = []
    for n in range(3):
        o = of_ref[:, n * MIX_W:(n + 1) * MIX_W] + ob_ref[:, n * MIX_W:(n + 1) * MIX_W]
        parts = []
        for hh in range(HEADS):
            oh = o[:, hh * DV:(hh + 1) * DV]
            ms = jnp.mean(oh * oh, axis=-1, keepdims=True)
            parts.append(oh * lax.rsqrt(ms + EPS))
        ons.append(jnp.concatenate(parts, axis=1) * hn_ref[n:n + 1, :])
    brs = []
    for n in range(3):
        sg = jax.nn.sigmoid(gpre[n])
        act = sg if n == 0 else gpre[n] * sg
        brs.append(jnp.dot((act * ons[n]).astype(BF16), wb_ref[n], preferred_element_type=F32))
    merged = None
    for n in range(3):
        term = jax.nn.sigmoid(mgpre[n]) * brs[n]
        merged = term if merged is None else merged + term
    out = jnp.dot(merged.astype(BF16), wo_ref[...], preferred_element_type=F32)
    gate = mod_ref[0, 5:6, :]
    o_ref[...] = x + gate * _rms(out, npost_ref[...])


def _merge(x, mod, npre, npost, pos, o_f, o_b, w_gate, head_norm, w_branch, w_out, *, layer, seq_len):
    t_total = x.shape[0]
    per_batch = mod.shape[0] > 1
    tm = _row_tile(t_total, seq_len, per_batch or pos is not None)
    mod_map = (lambda i: ((i * tm) // seq_len, 0, 0)) if per_batch else (lambda i: (0, 0, 0))
    in_specs = [
        pl.BlockSpec((tm, D_MODEL), lambda i: (i, 0)),
        pl.BlockSpec((1, 9, D_MODEL), mod_map),
        _const_spec((1, D_MODEL), 1, (layer, 1)),
        _const_spec((1, D_MODEL), 1, (layer, 1)),
    ]
    args = [x, mod, npre, npost]
    if pos is not None:
        tiles_per_seq = seq_len // tm
        in_specs.append(pl.BlockSpec((tm, D_MODEL), lambda i: (i % tiles_per_seq, 0)))
        args.append(pos)
    in_specs += [
        pl.BlockSpec((tm, 3 * MIX_W), lambda i: (i, 0)),
        pl.BlockSpec((tm, 3 * MIX_W), lambda i: (i, 0)),
        _const_spec((D_MODEL, GATE_W), 1, (layer,)),
        _const_spec((3, MIX_W), 1, (layer,)),
        _const_spec((3, MIX_W, D_MODEL), 1, (layer,)),
        _const_spec((D_MODEL, D_MODEL), 1, (layer,)),
    ]
    args += [o_f, o_b, w_gate, head_norm, w_branch, w_out]
    return pl.pallas_call(
        functools.partial(_merge_kernel, has_pos=pos is not None),
        grid=(t_total // tm,),
        in_specs=in_specs,
        out_specs=pl.BlockSpec((tm, D_MODEL), lambda i: (i, 0)),
        out_shape=jax.ShapeDtypeStruct((t_total, D_MODEL), F32),
        compiler_params=pltpu.CompilerParams(
            dimension_semantics=("arbitrary",), vmem_limit_bytes=VMEM_LIMIT),
        name="merge",
    )(*args)


def _scan_constants():
    t = np.arange(CHUNK)
    lower = (t[None, :] <= t[:, None])
    lm = np.stack([lower, lower.T]).astype(np.float32)
    lvl = np.zeros((N_DIR, N_LEVELS, CHUNK, CHUNK), np.float32)
    lvl[:, 0] = np.eye(CHUNK)
    for p in range(1, N_LEVELS):
        half = 1 << (p - 1)
        same = (t[:, None] >> p) == (t[None, :] >> p)
        fwd = same & ((t[:, None] & half) != 0) & ((t[None, :] & half) == 0)
        lvl[0, p] = fwd
        lvl[1, p] = fwd.T
    tri = lvl.sum(axis=1)
    assert np.array_equal(tri[0], lower) and np.array_equal(tri[1], lower.T)
    lvl = np.tile(lvl, (1, 1, 1, HEADS))
    tri = np.tile(tri, (1, 1, HEADS))
    pair_blk = np.kron(np.eye(2), np.ones((CHUNK, DK)))
    sel = np.zeros((N_DIR, N_MIF, 2 * QK_W), np.float32)
    for d in range(N_DIR):
        for h in range(HEADS):
            sel[d, d * 2 * HEADS + HEADS + h, h * DK:(h + 1) * DK] = 1.0
            sel[d, d * 2 * HEADS + h, QK_W + h * DK:QK_W + (h + 1) * DK] = 1.0
    col_sel = np.zeros((QK_W, LANE), np.float32)
    for h in range(HEADS):
        col_sel[h * DK:(h + 1) * DK, h] = 1.0
    return dict(
        lm=jnp.asarray(lm, dtype=BF16), lvl=jnp.asarray(lvl), tri=jnp.asarray(tri),
        sel3=jnp.asarray(np.tile(sel, (1, 3, 1)), dtype=BF16),
        hm2=jnp.asarray(pair_blk, dtype=BF16), onesbd=jnp.asarray(col_sel, dtype=BF16),
        cm=jnp.asarray(col_sel),
    )


_TN = (((0,), (0,)), ((), ()))
_NT = (((1,), (1,)), ((), ()))


def _log_sigmoid(x):
    return jnp.minimum(x, 0.0) - jnp.log(1.0 + jnp.exp(-jnp.abs(x)))


def _split3(x):
    hi = x.astype(BF16)
    r1 = x - hi.astype(F32)
    mid = r1.astype(BF16)
    lo = (r1 - mid.astype(F32)).astype(BF16)
    return hi, mid, lo


def _cumsum_rows_issue(x, lm_bf):
    return jnp.dot(lm_bf, jnp.concatenate(_split3(x), axis=1), preferred_element_type=F32)


def _cumsum_rows_finish(c):
    w = c.shape[1] // 3
    return (c[:, :w] + c[:, w:2 * w]) + c[:, 2 * w:]


def _to_column(row):
    col = jnp.transpose(jnp.broadcast_to(row, (8, row.shape[1])))
    return jnp.broadcast_to(col[:, 0:1], (row.shape[1], LANE))


def _block_diag2(a, b):
    return jnp.concatenate([jnp.concatenate([a, jnp.zeros_like(b)], axis=1),
                            jnp.concatenate([jnp.zeros_like(a), b], axis=1)], axis=0)


def _pair_scores(xq, xk, hm2_ref):
    outs = []
    for j in range(2):
        kj = xk[:, j * LANE:(j + 1) * LANE]
        kbd = jnp.concatenate([kj, kj], axis=0) * hm2_ref[...]
        outs.append(lax.dot_general(xq[:, j * LANE:(j + 1) * LANE], kbd, _NT, preferred_element_type=F32))
    return jnp.concatenate(outs, axis=1)


def _row_bcast(ref, r):
    return jnp.broadcast_to(ref[r:r + 1, :], (8, ref.shape[1]))


def _level_operand(p, d, g2g, qgrp, kgrp, gbuf, iota8):
    blk, half = 1 << p, 1 << (p - 1)
    mid = half if d == 0 else half - 1
    out = []
    for j in range(CHUNK // 8):
        r0 = 8 * j
        if blk >= 16:
            ref = _row_bcast(gbuf, (r0 // blk) * blk + mid)
            query = ((r0 & half) != 0) == (d == 0)
            dlt = g2g[j] - ref if query else ref - g2g[j]
            w = qgrp[j] if query else kgrp[j]
        else:
            ref = _row_bcast(gbuf, r0 + mid)
            for m in range(1, 8 // blk):
                ref = jnp.where(iota8 >= m * blk, _row_bcast(gbuf, r0 + m * blk + mid), ref)
            upper = (iota8 & half) != 0
            dlt = (g2g[j] - ref) * jnp.where(upper == (d == 0), 1.0, -1.0)
            w = jnp.where(upper, qgrp[j], kgrp[j]) if d == 0 else jnp.where(upper, kgrp[j], qgrp[j])
        out.append(w * jnp.exp2(dlt))
    return jnp.concatenate(out, axis=0).astype(BF16)


def _gated_dir(q, k, v, la, s_ref, gbuf, slot, d, cst, o_ref, o_lo):
    csum = _cumsum_rows_issue(la, cst["lm"][d])
    vb = v.astype(BF16)
    s_old = [s_ref[slot * HEADS + h] for h in range(HEADS)]
    yield
    g2 = _cumsum_rows_finish(csum) * LOG2E
    last = CHUNK - 1 if d == 0 else 0
    g2_last = g2[last:last + 1, :]
    gbuf[...] = g2
    iota8 = lax.broadcasted_iota(jnp.int32, (8, QK_W), 0)
    groups = [slice(8 * j, 8 * j + 8) for j in range(CHUNK // 8)]
    g2g, qgrp, kgrp = [g2[r] for r in groups], [q[r] for r in groups], [k[r] for r in groups]
    scores = [_pair_scores(q.astype(BF16), k.astype(BF16), cst["hm2"])]
    for p in range(1, N_LEVELS):
        x = _level_operand(p, d, g2g, qgrp, kgrp, gbuf, iota8)
        scores.append(_pair_scores(x, x, cst["hm2"]))
    qg = (q * jnp.exp2(g2)).astype(BF16)
    kg = (k * jnp.exp2(g2_last - g2)).astype(BF16)
    dec = _to_column(jnp.exp2(g2_last))
    dss = [lax.dot_general(kg[:, j * LANE:(j + 1) * LANE], vb[:, 2 * j * DV:(2 * j + 2) * DV], _TN,
                           preferred_element_type=F32) for j in range(2)]
    yield
    a = scores[0] * cst["lvl"][d, 0]
    for p in range(1, N_LEVELS):
        a = a + scores[p] * cst["lvl"][d, p]
    ab = a.astype(BF16)
    outs = []
    for j in range(2):
        rhs = jnp.concatenate([
            _block_diag2(vb[:, 2 * j * DV:(2 * j + 1) * DV], vb[:, (2 * j + 1) * DV:(2 * j + 2) * DV]),
            _block_diag2(s_old[2 * j].astype(BF16), s_old[2 * j + 1].astype(BF16)),
        ], axis=0)
        lhs = jnp.concatenate([ab[:, j * LANE:(j + 1) * LANE], qg[:, j * LANE:(j + 1) * LANE]], axis=1)
        outs.append(jnp.dot(lhs, rhs, preferred_element_type=F32))
    yield
    for j in range(2):
        for h2 in range(2):
            h = 2 * j + h2
            s_ref[slot * HEADS + h] = (dec[h * DK:(h + 1) * DK, :] * s_old[h]
                                       + dss[j][h2 * DK:(h2 + 1) * DK, h2 * DV:(h2 + 1) * DV])
    o_ref[:, o_lo:o_lo + MIX_W] = jnp.concatenate(outs, axis=1)


def _mlstm_dir(q, k, v, g16, c_ref, n_ref, m_ref, slot, d, cst, o_ref):
    neg_inf = float("-inf")
    pre = g16 + cst["bias16"][...]
    cum3 = jnp.dot(jnp.concatenate(_split3(_log_sigmoid(pre)), axis=0), cst["lm"][1 - d],
                   preferred_element_type=F32)
    qk = _pair_scores(q.astype(BF16), k.astype(BF16), cst["hm2"])
    vb = v.astype(BF16)
    c_old = [c_ref[slot * HEADS + h] for h in range(HEADS)]
    n_old = n_ref[slot]
    yield
    cum = (cum3[0:16] + cum3[16:32]) + cum3[32:48]
    is_f = (lax.broadcasted_iota(jnp.int32, (N_MIF, CHUNK), 0) & HEADS) != 0
    y = jnp.where(is_f, cum, pre)
    be = lax.dot_general(jnp.concatenate(_split3(y), axis=0), cst["sel3"][d], _TN,
                         preferred_element_type=F32)
    yield
    b_exp, li_exp = be[:, :QK_W], be[:, QK_W:]
    r0 = d * 2 * HEADS
    li_row = jnp.concatenate([y[r0 + h:r0 + h + 1, :] for h in range(HEADS)], axis=1)
    b_row = jnp.concatenate([y[r0 + HEADS + h:r0 + HEADS + h + 1, :] for h in range(HEADS)], axis=1)
    m_prev = m_ref[slot:slot + 1, :]
    inter = b_exp + m_prev
    dmat = jnp.where(cst["tri"][d] > 0.0, b_exp - b_row + li_row, neg_inf)
    lane_head = lax.broadcasted_iota(jnp.int32, (CHUNK, QK_W), 1) // DK
    mt = inter
    for h in range(HEADS):
        sel = lane_head == h
        rm = jnp.max(jnp.where(sel, dmat, neg_inf), axis=-1, keepdims=True)
        mt = jnp.where(sel, jnp.maximum(inter, rm), mt)
    sc = (qk * jnp.exp(dmat - mt)).astype(BF16)
    qe = (q * jnp.exp(inter - mt)).astype(BF16)
    last = CHUNK - 1 if d == 0 else 0
    b_last = b_exp[last:last + 1, :]
    lw = b_last - b_exp + li_exp
    m_new = jnp.maximum(b_last + m_prev, jnp.max(lw, axis=0, keepdims=True))
    kw = (k * jnp.exp(lw - m_new)).astype(BF16)
    dec_col = _to_column(jnp.exp(b_last + m_prev - m_new))
    ones_blk = jnp.ones((CHUNK, LANE), BF16)
    nds, dcs = [], []
    for j in range(2):
        rows = slice(j * LANE, (j + 1) * LANE)
        v_pair = vb[:, 2 * j * DV:(2 * j + 2) * DV]
        top = jnp.concatenate([_block_diag2(v_pair[:, :DV], v_pair[:, DV:]), cst["onesbd"][rows, :]], axis=1)
        bot = jnp.concatenate([_block_diag2(c_old[2 * j].astype(BF16), c_old[2 * j + 1].astype(BF16)),
                               n_old[rows, :].astype(BF16)], axis=1)
        lhs = jnp.concatenate([sc[:, rows], qe[:, rows]], axis=1)
        nds.append(jnp.dot(lhs, jnp.concatenate([top, bot], axis=0), preferred_element_type=F32))
        dcs.append(lax.dot_general(kw[:, rows], jnp.concatenate([v_pair, ones_blk], axis=1), _TN,
                                   preferred_element_type=F32))
    yield
    outs = []
    for j in range(2):
        rows = slice(j * LANE, (j + 1) * LANE)
        nd, dc = nds[j], dcs[j]
        for h2 in range(2):
            h = 2 * j + h2
            den = nd[:, 2 * DV + h:2 * DV + h + 1]
            mth = mt[:, h * DK:h * DK + 1]
            outs.append(nd[:, h2 * DV:(h2 + 1) * DV] / jnp.maximum(jnp.abs(den), jnp.exp(-mth)))
            c_ref[slot * HEADS + h] = (dec_col[h * DK:(h + 1) * DK, :] * c_old[h]
                                       + dc[h2 * DK:(h2 + 1) * DK, h2 * DV:(h2 + 1) * DV])
        n_ref[slot, rows, :] = dec_col[rows, :] * n_old[rows, :] + cst["cm"][rows, :] * dc[:, 2 * DV:]
    m_ref[slot:slot + 1, :] = m_new
    o_ref[:, 0:MIX_W] = jnp.concatenate(outs, axis=1)


_DONE = object()
_PER_LAYER = ("bias16", "wup3", "glab")
_SCAN_CONSTS = ("lm", "lvl", "tri", "sel3", "hm2", "onesbd", "cm", "bias16", "wup3", "glab", "gamma")


def _scan_kernel(*refs, layer, has_init, emit_state, n_prev):
    pf_ref, pb_ref = refs[:2]
    pos = 2
    cst = dict(zip(_SCAN_CONSTS, refs[pos:pos + len(_SCAN_CONSTS)]))
    pos += len(_SCAN_CONSTS)
    if has_init:
        c0_ref, n0_ref, m0_ref, g0_ref, h0_ref = refs[pos:pos + 5]
        pos += 5
    if n_prev:
        prev_refs = refs[pos:pos + 3]
        pos += 3
    of_ref, ob_ref = refs[pos:pos + 2]
    pos += 2
    if emit_state:
        cout_ref, nout_ref, mout_ref, gout_ref, hout_ref = refs[pos:pos + 5]
        pos += 5
    c_scr, n_scr, m_scr, g_scr, h_scr, gbuf = refs[pos:pos + 6]

    ci = pl.program_id(1)

    @pl.when(ci == 0)
    def _():
        if has_init:
            for q in range(SEQ_PER_STEP):
                lo, hi = q * N_DIR * HEADS, (q + 1) * N_DIR * HEADS
                c_scr[lo:hi] = c0_ref[q]
                g_scr[lo:hi] = g0_ref[q]
                h_scr[lo:hi] = h0_ref[q]
                for d in range(N_DIR):
                    n_scr[q * N_DIR + d] = cst["cm"][...] * n0_ref[q, d]
                m_scr[q * N_DIR:(q + 1) * N_DIR, :] = m0_ref[q]
        else:
            c_scr[...] = jnp.zeros(c_scr.shape, F32)
            g_scr[...] = jnp.zeros(g_scr.shape, F32)
            h_scr[...] = jnp.zeros(h_scr.shape, F32)
            n_scr[...] = jnp.zeros(n_scr.shape, F32)
            m_scr[...] = jnp.zeros(m_scr.shape, F32)

    gam = cst["gamma"][...]
    ge = jnp.exp(gam - jnp.max(gam, axis=0, keepdims=True))
    pg = ge / jnp.sum(ge, axis=0, keepdims=True)
    cs = pg[0:1, :]
    for j in range(1, layer + 1):
        cs = cs + pg[j:j + 1, :]
    lb = cs - pg[0:1, :]

    units = []
    for q in range(SEQ_PER_STEP):
        for d, (p4_ref, o4_ref) in enumerate(((pf_ref, of_ref), (pb_ref, ob_ref))):
            slot = q * N_DIR + d
            p_ref, o_ref = p4_ref.at[q, 0], o4_ref.at[q, 0]
            g16 = jnp.transpose(p_ref[:, OFF_SM:OFF_SM + LANE])[0:N_MIF, :]
            units.append(_mlstm_dir(
                p_ref[:, OFF_MQ:OFF_MQ + QK_W], p_ref[:, OFF_MK:OFF_MK + QK_W], p_ref[:, OFF_MV:OFF_MV + MIX_W],
                g16, c_scr, n_scr, m_scr, slot, d, cst, o_ref))
            sm = p_ref[:, OFF_SM:OFF_SM + LANE]
            sm_hi = sm.astype(BF16)
            sm_lo = (sm - sm_hi.astype(F32)).astype(BF16)
            pre = jnp.dot(jnp.concatenate([sm_hi, sm_lo, sm_hi], axis=1), cst["wup3"][d],
                          preferred_element_type=F32) + cst["glab"][d:d + 1, :]
            units.append(_gated_dir(
                p_ref[:, OFF_GQ:OFF_GQ + QK_W], p_ref[:, OFF_GK:OFF_GK + QK_W], p_ref[:, OFF_GV:OFF_GV + MIX_W],
                _log_sigmoid(pre) * (1.0 / G_TEMP), g_scr, gbuf.at[2 * slot], slot, d, cst, o_ref, MIX_W))
            zz = p_ref[:, OFF_HF + d * QK_W:OFF_HF + (d + 1) * QK_W]
            ez = jnp.exp(-jnp.abs(zz))
            rz = 1.0 / (1.0 + ez)
            pos_z = zz >= 0.0
            sig = jnp.where(pos_z, rz, ez * rz)
            nsig = jnp.where(pos_z, ez * rz, rz)
            hv = p_ref[:, OFF_HV:OFF_HV + MIX_W]
            units.append(_gated_dir(
                p_ref[:, OFF_HQ:OFF_HQ + QK_W], (1.0 - lb) * nsig, hv * jax.nn.sigmoid(hv),
                jnp.log(lb + (1.0 - lb) * sig), h_scr, gbuf.at[2 * slot + 1], slot, d, cst, o_ref, 2 * MIX_W))
    while units:
        units = [u for u in units if next(u, _DONE) is not _DONE]

    if emit_state:
        @pl.when(ci == pl.num_programs(1) - 1)
        def _():
            for q in range(SEQ_PER_STEP):
                lo, hi = q * N_DIR * HEADS, (q + 1) * N_DIR * HEADS
                for out_ref, scr, k in ((cout_ref, c_scr, 0), (gout_ref, g_scr, 1), (hout_ref, h_scr, 2)):
                    if n_prev:
                        out_ref[q, 0:n_prev] = prev_refs[k][q]
                    out_ref[q, n_prev] = scr[lo:hi]
                for d in range(N_DIR):
                    nout_ref[q, d] = jnp.sum(n_scr[q * N_DIR + d], axis=-1, keepdims=True)
                mout_ref[q] = m_scr[q * N_DIR:(q + 1) * N_DIR, :]


def _scan(p, cst, init, prev_states, *, layer, nbatch, seq_len, emit_state):
    t_total = p.shape[0]
    nc = seq_len // CHUNK
    assert nbatch % SEQ_PER_STEP == 0
    has_init = init is not None

    fwd = lambda b, c: (b, c, 0, 0)
    bwd = lambda b, c: (b, nc - 1 - c, 0, 0)
    p4 = p.reshape(nbatch, nc, CHUNK, P_W)
    const_args = [cst[name] for name in _SCAN_CONSTS]
    in_specs = [
        pl.BlockSpec((SEQ_PER_STEP, 1, CHUNK, P_W), fwd),
        pl.BlockSpec((SEQ_PER_STEP, 1, CHUNK, P_W), bwd),
    ] + [_const_spec(a.shape[1:], 2, (layer,)) if name in _PER_LAYER else _const_spec(a.shape, 2)
         for name, a in zip(_SCAN_CONSTS, const_args)]
    args = [p4, p4] + const_args
    state_blk = (SEQ_PER_STEP, N_DIR * HEADS, DK, DV)
    state_map = lambda b, c: (b, 0, 0, 0)
    state_specs = [
        pl.BlockSpec(state_blk, state_map),
        pl.BlockSpec((SEQ_PER_STEP, N_DIR, QK_W, 1), state_map),
        pl.BlockSpec((SEQ_PER_STEP, N_DIR, QK_W), lambda b, c: (b, 0, 0)),
        pl.BlockSpec(state_blk, state_map),
        pl.BlockSpec(state_blk, state_map),
    ]
    if has_init:
        in_specs += state_specs
        args += list(init)
    n_prev = prev_states[0].shape[1] if prev_states else 0
    if n_prev:
        prev_blk = (SEQ_PER_STEP, n_prev, N_DIR * HEADS, DK, DV)
        in_specs += [pl.BlockSpec(prev_blk, lambda b, c: (b, 0, 0, 0, 0))] * 3
        args += list(prev_states)
    out_specs = [pl.BlockSpec((SEQ_PER_STEP, 1, CHUNK, 3 * MIX_W), fwd),
                 pl.BlockSpec((SEQ_PER_STEP, 1, CHUNK, 3 * MIX_W), bwd)]
    out_shape = [jax.ShapeDtypeStruct((nbatch, nc, CHUNK, 3 * MIX_W), F32)] * 2
    if emit_state:
        stk_blk = (SEQ_PER_STEP, n_prev + 1, N_DIR * HEADS, DK, DV)
        stk_spec = pl.BlockSpec(stk_blk, lambda b, c: (b, 0, 0, 0, 0))
        stk_shape = jax.ShapeDtypeStruct((nbatch, n_prev + 1, N_DIR * HEADS, DK, DV), F32)
        out_specs += [stk_spec, state_specs[1], state_specs[2], stk_spec, stk_spec]
        out_shape += [
            stk_shape,
            jax.ShapeDtypeStruct((nbatch, N_DIR, QK_W, 1), F32),
            jax.ShapeDtypeStruct((nbatch, N_DIR, QK_W), F32),
            stk_shape,
            stk_shape,
        ]
    n_slots = SEQ_PER_STEP * N_DIR
    res = pl.pallas_call(
        functools.partial(_scan_kernel, layer=layer, has_init=has_init, emit_state=emit_state, n_prev=n_prev),
        grid=(nbatch // SEQ_PER_STEP, nc),
        in_specs=in_specs,
        out_specs=out_specs,
        out_shape=out_shape,
        scratch_shapes=[
            pltpu.VMEM((n_slots * HEADS, DK, DV), F32),
            pltpu.VMEM((n_slots, QK_W, LANE), F32),
            pltpu.VMEM((8, QK_W), F32),
            pltpu.VMEM((n_slots * HEADS, DK, DV), F32),
            pltpu.VMEM((n_slots * HEADS, DK, DV), F32),
            pltpu.VMEM((2 * n_slots, CHUNK, QK_W), F32),
        ],
        compiler_params=pltpu.CompilerParams(
            dimension_semantics=("arbitrary", "arbitrary"), vmem_limit_bytes=VMEM_LIMIT),
        name="scan",
    )(*args)
    return [res[0].reshape(t_total, 3 * MIX_W), res[1].reshape(t_total, 3 * MIX_W)] + list(res[2:])


def _grid_position(n_tokens):
    rows = n_tokens // GRID_W
    quarter = D_MODEL // 4
    freqs = jnp.exp(-math.log(10000.0) * jnp.arange(quarter, dtype=F32) / quarter)
    r = jnp.arange(rows, dtype=F32)[:, None] * freqs
    cl = jnp.arange(GRID_W, dtype=F32)[:, None] * freqs
    r_emb = jnp.concatenate([jnp.sin(r), jnp.cos(r)], axis=-1)
    c_emb = jnp.concatenate([jnp.sin(cl), jnp.cos(cl)], axis=-1)
    emb = jnp.concatenate([jnp.broadcast_to(r_emb[:, None], (rows, GRID_W, D_MODEL // 2)),
                           jnp.broadcast_to(c_emb[None], (rows, GRID_W, D_MODEL // 2))], axis=-1)
    return emb.reshape(rows * GRID_W, D_MODEL)


def _regroup_kernel(wt_ref, ws_ref, wg_ref):
    offs = [int(o) for o in np.concatenate([[0], np.cumsum(IN_SIZES)])]
    rows = [wt_ref[offs[i]:offs[i + 1], :] for i in range(len(IN_SIZES))]
    (mq, mk, mv, mo, mif, gq, gk, gv, gr, glr, hq, hf, hv, hg, mg) = rows
    small = jnp.concatenate([mif, glr, jnp.zeros((LANE - N_MIF - N_GLR, mif.shape[1]), F32)], axis=0)
    qs = DK ** -0.5
    lo = 0
    for piece in (mq * qs, mk, mv, gq * qs, gk, gv, hq, hf, hv, small):
        ws_ref[:, lo:lo + piece.shape[0]] = piece.T.astype(BF16)
        lo += piece.shape[0]
    lo = 0
    for piece in (mo, gr, hg, mg):
        wg_ref[:, lo:lo + piece.shape[0]] = piece.T.astype(BF16)
        lo += piece.shape[0]


def _regroup(w_in):
    depth, _, n_in = w_in.shape
    tr = 256
    return pl.pallas_call(
        _regroup_kernel,
        grid=(depth, D_MODEL // tr),
        in_specs=[pl.BlockSpec((None, n_in, tr), lambda l, i: (l, 0, i))],
        out_specs=[pl.BlockSpec((None, tr, P_W), lambda l, i: (l, i, 0)),
                   pl.BlockSpec((None, tr, GATE_W), lambda l, i: (l, i, 0))],
        out_shape=[jax.ShapeDtypeStruct((depth, D_MODEL, P_W), BF16),
                   jax.ShapeDtypeStruct((depth, D_MODEL, GATE_W), BF16)],
        compiler_params=pltpu.CompilerParams(
            dimension_semantics=("arbitrary", "arbitrary"), vmem_limit_bytes=VMEM_LIMIT),
        name="regroup",
    )(jnp.swapaxes(w_in, 1, 2))


def _prepare_weights(w_in, gate_bias, gla_w_up):
    depth = w_in.shape[0]
    w_scan, w_gate = _regroup(w_in)
    bias16 =jnp.broadcast_to(gate_bias.reshape(depth, N_MIF, 1), (depth, N_MIF, CHUNK))
    wup = jnp.zeros((depth, N_DIR, LANE, QK_W), F32)
    for d in range(N_DIR):
        wup = wup.at[:, d, N_MIF + d * G_RANK:N_MIF + (d + 1) * G_RANK, :].set(gla_w_up[:, d])
    wup_hi = wup.astype(BF16)
    wup_lo = (wup - wup_hi.astype(F32)).astype(BF16)
    wup3 = jnp.concatenate([wup_hi, wup_hi, wup_lo], axis=2)
    return w_scan, w_gate, bias16, wup3


def _trunk_layer(x, mod, pos, init, prev_states, wts, cst, *, layer, nbatch, seq_len, emit_state):
    x = _ffn(x, mod, wts["npre"], wts["npost"], wts["w1"], wts["w2"], layer=layer, half=0, seq_len=seq_len)
    p = _proj(x, mod, wts["npre"], pos, wts["w_scan"], layer=layer, seq_len=seq_len)
    res = _scan(p, cst, init, prev_states, layer=layer, nbatch=nbatch, seq_len=seq_len, emit_state=emit_state)
    x = _merge(x, mod, wts["npre"], wts["npost"], pos, res[0], res[1], wts["w_gate"], wts["head_norm"],
               wts["w_branch"], wts["w_out"], layer=layer, seq_len=seq_len)
    x = _ffn(x, mod, wts["npre"], wts["npost"], wts["w1"], wts["w2"], layer=layer, half=1, seq_len=seq_len)
    return x, res[2:]


def kernel(x_prompt, x_sample, c, state_mlstm_C, state_mlstm_n, state_mlstm_m, state_gla_S, state_hgrn_S,
           c_ctx, w_ada, b_ada, norm_pre, norm_post, w_ffn_in, w_ffn_out, w_in, mlstm_gate_bias,
           gla_w_up, gla_b, hgrn_gamma, head_norm, w_branch, w_out):
    bp, tp, _ = x_prompt.shape
    bs, ts, _ = x_sample.shape
    depth = w_in.shape[0]

    n_c = 1 + bs
    rows = -(-n_c // 8) * 8
    cvec = jnp.concatenate([c_ctx[None, :], c, jnp.zeros((rows - n_c, D_MODEL), F32)], axis=0)
    mod_all = _ada(cvec, w_ada, b_ada)[:, :n_c].reshape(depth, n_c, 9, D_MODEL)

    w_scan, w_gate, bias16, wup3 = _prepare_weights(w_in, mlstm_gate_bias, gla_w_up)
    wts = dict(
        npre=norm_pre.reshape(depth, 3, 1, D_MODEL), npost=norm_post.reshape(depth, 3, 1, D_MODEL),
        w1=w_ffn_in.astype(BF16), w2=w_ffn_out.astype(BF16),
        w_scan=w_scan, w_gate=w_gate,
        head_norm=head_norm, w_branch=w_branch.astype(BF16), w_out=w_out.astype(BF16),
    )
    cst = dict(_scan_constants(), bias16=bias16, wup3=wup3, glab=gla_b, gamma=hgrn_gamma)

    pos = _grid_position(ts).astype(F32)
    xp = x_prompt.reshape(bp * tp, D_MODEL)
    xs = x_sample.reshape(bs * ts, D_MODEL)
    new_states = []
    for l in range(depth):
        prev = (new_states[-1][0], new_states[-1][3], new_states[-1][4]) if new_states else None
        xp, st = _trunk_layer(xp, mod_all[l, 0:1], None, None, prev, wts, cst,
                              layer=l, nbatch=bp, seq_len=tp, emit_state=True)
        new_states.append(st)
        init = (
            state_mlstm_C[:, l].reshape(bs, N_DIR * HEADS, DK, DV),
            state_mlstm_n[:, l].reshape(bs, N_DIR, QK_W, 1),
            jnp.repeat(state_mlstm_m[:, l], DK, axis=-1),
            state_gla_S[:, l].reshape(bs, N_DIR * HEADS, DK, DV),
            state_hgrn_S[:, l].reshape(bs, N_DIR * HEADS, DK, DV),
        )
        xs, _ = _trunk_layer(xs, mod_all[l, 1:], pos, init, None, wts, cst,
                             layer=l, nbatch=bs, seq_len=ts, emit_state=False)

    blk = (bp, depth, N_DIR, HEADS, DK, DV)
    last = new_states[-1]
    new_c, new_g, new_h = last[0].reshape(blk), last[3].reshape(blk), last[4].reshape(blk)
    new_n = jnp.stack([st[1].reshape(bp, N_DIR, HEADS, DK) for st in new_states], axis=1)
    new_m = jnp.stack([st[2][:, :, ::DK] for st in new_states], axis=1)
    return (xp.reshape(bp, tp, D_MODEL), xs.reshape(bs, ts, D_MODEL), new_c, new_n, new_m, new_g, new_h)
```
